```python
import math
import jax, jax.numpy as jnp
from jax import lax
import numpy as np

D_MODEL = 1024
BATCH = 4
SEQ = 4096
DEPTH = 4

GRID_W = 64
CTX_LEN = 256
HEAD_DIM = 64
LRU_WIDTH = 256
LRU_BLOCKS = 4
LRU_BLOCK = LRU_WIDTH // LRU_BLOCKS
CONV_W = 4
LRU_C = 8.0
SWA_Q_HEADS = 8
SWA_KV_HEADS = 2
SWA_GROUP = SWA_Q_HEADS // SWA_KV_HEADS
SWA_WIDTH = SWA_Q_HEADS * HEAD_DIM
WINDOW = 128
BLOCK = 128
DIFF_HEADS = 4
DIFF_QK = HEAD_DIM // 2
DIFF_WIDTH = DIFF_HEADS * HEAD_DIM
MIX_WIDTH = LRU_WIDTH + SWA_WIDTH + DIFF_WIDTH
IN_SPLIT = (LRU_WIDTH, LRU_WIDTH, SWA_Q_HEADS * HEAD_DIM, SWA_KV_HEADS * HEAD_DIM, SWA_KV_HEADS * HEAD_DIM,
            DIFF_HEADS * 2 * DIFF_QK, DIFF_HEADS * 2 * DIFF_QK, DIFF_HEADS * HEAD_DIM)
IN_COLS = sum(IN_SPLIT)
D_FF = 2816
N_EXPERTS = 8
TOP_K = 2
EXPERT_FF = 2816
N_DENSE = (DEPTH + 1) // 2
N_MOE = DEPTH // 2
ROPE_BASE = 10000.0
LN_EPS = 1e-5
NEG_INF = -1e30
DEEPNORM_ALPHA = (2.0 * DEPTH) ** 0.25
DEEPNORM_BETA = (8.0 * DEPTH) ** -0.25

kernel_name = "hymba_style_hybrid_diffusion_trunk"

F32 = jnp.float32


def layer_norm(x, g, b):
    xf = x.astype(F32)
    mu = jnp.mean(xf, -1, keepdims=True)
    var = jnp.mean(jnp.square(xf - mu), -1, keepdims=True)
    return ((xf - mu) * lax.rsqrt(var + LN_EPS)).astype(x.dtype) * g + b


def axial_rope(rows, rot_dim):
    n_freq = rot_dim // 4
    inv = ROPE_BASE ** (-jnp.arange(n_freq, dtype=F32) / n_freq)
    row = jnp.repeat(jnp.arange(rows, dtype=F32), GRID_W)
    col = jnp.tile(jnp.arange(GRID_W, dtype=F32), rows)
    ang = jnp.concatenate([row[:, None] * inv, col[:, None] * inv], -1)
    return jnp.cos(ang), jnp.sin(ang)


def apply_rope(t, cos, sin):
    shp = (cos.shape[0],) + (1,) * (t.ndim - 3) + (cos.shape[1],)
    cos = cos.reshape(shp).astype(t.dtype)
    sin = sin.reshape(shp).astype(t.dtype)
    half = t.shape[-1] // 2
    t1, t2 = t[..., :half], t[..., half:]
    return jnp.concatenate([t1 * cos - t2 * sin, t2 * cos + t1 * sin], -1)


def split_cols(p):
    outs, o = [], 0
    for n in IN_SPLIT:
        outs.append(p[..., o:o + n])
        o += n
    return outs


def centred_dwconv(u, w, b):
    pad_l = (CONV_W - 1) // 2
    pad_r = CONV_W - 1 - pad_l
    out = lax.conv_general_dilated(u, w[:, None, :], window_strides=(1,), padding=[(pad_l, pad_r)],
                                   dimension_numbers=('NWC', 'WIO', 'NWC'), feature_group_count=u.shape[-1])
    return out + b


def block_diag_linear(u, w, b):
    ub = u.reshape(u.shape[:-1] + (LRU_BLOCKS, LRU_BLOCK))
    return jnp.einsum('bnki,kio->bnko', ub, w.astype(F32)).reshape(u.shape) + b.astype(F32)


def rglru_coeffs(u, w_a, b_a, w_x, b_x, lam):
    r = jax.nn.sigmoid(block_diag_linear(u, w_a, b_a))
    i = jax.nn.sigmoid(block_diag_linear(u, w_x, b_x))
    log_a = -LRU_C * r * jax.nn.softplus(-lam.astype(F32))
    a = jnp.exp(log_a)
    mult = jnp.sqrt(-jnp.expm1(2.0 * log_a))
    return a, mult * (i * u)


def linear_scan(a, bx, h0, reverse):
    def comb(l, r):
        return r[0] * l[0], r[0] * l[1] + r[1]
    a_cum, b_cum = lax.associative_scan(comb, (a, bx), axis=1, reverse=reverse)
    return a_cum * h0[:, None, :] + b_cum


def rglru_group(x_ctx, x_lat, conv_w, conv_b, gate_w, gate_b, lam):
    uc = centred_dwconv(x_ctx, conv_w, conv_b).astype(F32)
    ul = centred_dwconv(x_lat, conv_w, conv_b).astype(F32)
    ys_c, ys_l = [], []
    for d, rev in enumerate((False, True)):
        a_c, b_c = rglru_coeffs(uc, gate_w[d, 0], gate_b[d, 0], gate_w[d, 1], gate_b[d, 1], lam[d])
        h_c = linear_scan(a_c, b_c, jnp.zeros_like(uc[:, 0]), rev)
        h_fin = h_c[:, 0] if rev else h_c[:, -1]
        a_l, b_l = rglru_coeffs(ul, gate_w[d, 0], gate_b[d, 0], gate_w[d, 1], gate_b[d, 1], lam[d])
        h_l = linear_scan(a_l, b_l, h_fin, rev)
        ys_c.append(h_c)
        ys_l.append(h_l)
    return (ys_c[0] + ys_c[1]).astype(x_ctx.dtype), (ys_l[0] + ys_l[1]).astype(x_lat.dtype)


def windowed_attention(q, k, v, k_ctx, v_ctx, sink_hg):
    B, S, Hk, G, d = q.shape
    nb = S // BLOCK
    L = k_ctx.shape[1]
    scale = d ** -0.5
    qb = q.reshape(B, nb, BLOCK, Hk, G, d)

    def band(t):
        tb = t.reshape(B, nb, BLOCK, Hk, d)
        tp = jnp.pad(tb, ((0, 0), (1, 1), (0, 0), (0, 0), (0, 0)))
        return jnp.concatenate([tp[:, :-2], tp[:, 1:-1], tp[:, 2:]], axis=2)

    kb, vb = band(k), band(v)
    s_band = jnp.einsum('bnqhgd,bnkhd->bnhgqk', qb, kb, preferred_element_type=F32) * scale
    q_abs = jnp.arange(nb)[:, None, None] * BLOCK + jnp.arange(BLOCK)[None, :, None]
    k_abs = jnp.arange(nb)[:, None, None] * BLOCK - BLOCK + jnp.arange(3 * BLOCK)[None, None, :]
    valid = (jnp.abs(k_abs - q_abs) <= WINDOW) & (k_abs >= 0) & (k_abs < S)
    s_band = jnp.where(valid[None, :, None, None], s_band, NEG_INF)
    s_ctx = jnp.einsum('bnqhgd,bkhd->bnhgqk', qb, k_ctx, preferred_element_type=F32) * scale
    s_sink = jnp.broadcast_to(sink_hg.astype(F32)[None, None, :, :, None, None], s_ctx.shape[:-1] + (1,))
    p = jax.nn.softmax(jnp.concatenate([s_ctx, s_band, s_sink], -1), axis=-1)
    o = (jnp.einsum('bnhgqk,bkhd->bnqhgd', p[..., :L], v_ctx.astype(F32))
         + jnp.einsum('bnhgqk,bnkhd->bnqhgd', p[..., L:-1], vb.astype(F32)))
    return o.reshape(B, S, Hk * G * d).astype(v.dtype)


def context_sink_attention(q, k, v, sink_hg):
    B, L, Hk, G, d = q.shape
    s = jnp.einsum('bqhgd,bkhd->bhgqk', q, k, preferred_element_type=F32) * (d ** -0.5)
    s_sink = jnp.broadcast_to(sink_hg.astype(F32)[None, :, :, None, None], s.shape[:-1] + (1,))
    p = jax.nn.softmax(jnp.concatenate([s, s_sink], -1), axis=-1)[..., :-1]
    o = jnp.einsum('bhgqk,bkhd->bqhgd', p, v.astype(F32))
    return o.reshape(B, L, Hk * G * d).astype(v.dtype)


def diff_attention(q, k, v, lam):
    s = jnp.einsum('bqhmd,bkhmd->bhmqk', q, k, preferred_element_type=F32) * (DIFF_QK ** -0.5)
    p = jax.nn.softmax(s, axis=-1)
    w = p[:, :, 0] - lam * p[:, :, 1]
    return jnp.einsum('bhqk,bkhd->bqhd', w, v.astype(F32)).astype(v.dtype)


def diff_latent(q, k_all, v_all, lam):
    B, S = q.shape[:2]
    nb = S // BLOCK
    qb = jnp.moveaxis(q.reshape((B, nb, BLOCK) + q.shape[2:]), 1, 0)
    ob = lax.map(lambda qblk: diff_attention(qblk, k_all, v_all, lam), qb)
    return jnp.moveaxis(ob, 0, 1).reshape((B, S) + ob.shape[3:])


def diff_head_norm(o, g, lam_init):
    of = o.astype(F32)
    of = of * lax.rsqrt(jnp.mean(jnp.square(of), -1, keepdims=True) + LN_EPS)
    return (of * (1.0 - lam_init)).astype(o.dtype) * g


def token_mix(u_ctx, u_lat, w_in, w_out, conv_w, conv_b, gate_w, gate_b, lru_lam, sink, lam_vec, norm_g,
              lam_init, rope64, rope32, need_ctx):
    B, S, _ = u_lat.shape
    L = u_ctx.shape[1]
    pc = split_cols(u_ctx @ w_in)
    pl = split_cols(u_lat @ w_in)

    lru_c, lru_l = rglru_group(pc[0], pl[0], conv_w, conv_b, gate_w, gate_b, lru_lam)
    lru_c = lru_c * jax.nn.gelu(pc[1])
    lru_l = lru_l * jax.nn.gelu(pl[1])

    sink_hg = sink.reshape(SWA_KV_HEADS, SWA_GROUP)
    qs_l = apply_rope(pl[2].reshape(B, S, SWA_Q_HEADS, HEAD_DIM), *rope64).reshape(B, S, SWA_KV_HEADS, SWA_GROUP, HEAD_DIM)
    ks_l = apply_rope(pl[3].reshape(B, S, SWA_KV_HEADS, HEAD_DIM), *rope64)
    vs_l = pl[4].reshape(B, S, SWA_KV_HEADS, HEAD_DIM)
    ks_c = pc[3].reshape(B, L, SWA_KV_HEADS, HEAD_DIM)
    vs_c = pc[4].reshape(B, L, SWA_KV_HEADS, HEAD_DIM)
    swa_l = windowed_attention(qs_l, ks_l, vs_l, ks_c, vs_c, sink_hg)

    lam_f = lam_vec.astype(F32)
    lam = jnp.exp(jnp.sum(lam_f[0] * lam_f[1])) - jnp.exp(jnp.sum(lam_f[2] * lam_f[3])) + lam_init
    qd_l = apply_rope(pl[5].reshape(B, S, DIFF_HEADS, 2, DIFF_QK), *rope32)
    kd_l = apply_rope(pl[6].reshape(B, S, DIFF_HEADS, 2, DIFF_QK), *rope32)
    vd_l = pl[7].reshape(B, S, DIFF_HEADS, HEAD_DIM)
    kd_c = pc[6].reshape(B, L, DIFF_HEADS, 2, DIFF_QK)
    vd_c = pc[7].reshape(B, L, DIFF_HEADS, HEAD_DIM)
    k_all = jnp.concatenate([kd_c, kd_l], axis=1)
    v_all = jnp.concatenate([vd_c, vd_l], axis=1)
    diff_l = diff_head_norm(diff_latent(qd_l, k_all, v_all, lam), norm_g, lam_init).reshape(B, S, DIFF_WIDTH)

    out_l = jnp.concatenate([lru_l, swa_l, diff_l], axis=-1) @ w_out
    if not need_ctx:
        return None, out_l
    qs_c = pc[2].reshape(B, L, SWA_KV_HEADS, SWA_GROUP, HEAD_DIM)
    swa_c = context_sink_attention(qs_c, ks_c, vs_c, sink_hg)
    qd_c = pc[5].reshape(B, L, DIFF_HEADS, 2, DIFF_QK)
    diff_c = diff_head_norm(diff_attention(qd_c, kd_c, vd_c, lam), norm_g, lam_init).reshape(B, L, DIFF_WIDTH)
    out_c = jnp.concatenate([lru_c, swa_c, diff_c], axis=-1) @ w_out
    return out_c, out_l


def swiglu(u, w1, w3, w2):
    return (jax.nn.silu(u @ w1) * (u @ w3)) @ w2


def moe_swiglu(u, router_w, router_b, w1, w3, w2):
    logits = (u @ router_w + router_b).astype(F32)
    top_v, top_i = lax.top_k(logits, TOP_K)
    gates = jax.nn.softmax(top_v, axis=-1)
    combine = jnp.sum(jax.nn.one_hot(top_i, N_EXPERTS, dtype=F32) * gates[..., None], axis=-2)
    out = jnp.zeros_like(u)
    for e in range(N_EXPERTS):
        out = out + combine[..., e:e + 1].astype(u.dtype) * swiglu(u, w1[e], w3[e], w2[e])
    return out


def setup_inputs(seed: int = 0) -> dict:
    key = jax.random.key(seed)
    ks = jax.random.split(key, 32)
    nrm = jax.random.normal
    D = D_MODEL
    a0 = jax.random.uniform(ks[10], (DEPTH, 2, LRU_WIDTH), minval=0.9, maxval=0.999)
    root = a0 ** (1.0 / LRU_C)
    return {
        "x": nrm(ks[0], (BATCH, SEQ, D), F32),
        "c": nrm(ks[1], (BATCH, D), F32),
        "ctx": nrm(ks[2], (BATCH, CTX_LEN, D), F32),
        "c_ctx": nrm(ks[3], (D,), F32),
        "ada_w": nrm(ks[4], (DEPTH, D, 6 * D), F32) * (0.5 * D ** -0.5),
        "ada_b": nrm(ks[5], (DEPTH, 6 * D), F32) * 0.02,
        "w_in": nrm(ks[6], (DEPTH, D, IN_COLS), F32) * D ** -0.5,
        "w_out": nrm(ks[7], (DEPTH, MIX_WIDTH, D), F32) * (MIX_WIDTH ** -0.5 * DEEPNORM_BETA),
        "lru_conv_w": nrm(ks[8], (DEPTH, CONV_W, LRU_WIDTH), F32) * CONV_W ** -0.5,
        "lru_conv_b": nrm(ks[9], (DEPTH, LRU_WIDTH), F32) * 0.02,
        "lru_gate_w": nrm(ks[11], (DEPTH, 2, 2, LRU_BLOCKS, LRU_BLOCK, LRU_BLOCK), F32) * LRU_BLOCK ** -0.5,
        "lru_gate_b": nrm(ks[12], (DEPTH, 2, 2, LRU_WIDTH), F32) * 0.02,
        "lru_lam": jnp.log(root) - jnp.log1p(-root),
        "swa_sink": nrm(ks[13], (DEPTH, SWA_Q_HEADS), F32) * 0.5,
        "diff_lam": nrm(ks[14], (DEPTH, 4, DIFF_QK), F32) * 0.1,
        "diff_norm_g": 1.0 + 0.02 * nrm(ks[15], (DEPTH, HEAD_DIM), F32),
        "ln_g": 1.0 + 0.02 * nrm(ks[16], (DEPTH, 2, D), F32),
        "ln_b": 0.02 * nrm(ks[17], (DEPTH, 2, D), F32),
        "ffn_w1": nrm(ks[18], (N_DENSE, D, D_FF), F32) * D ** -0.5,
        "ffn_w3": nrm(ks[19], (N_DENSE, D, D_FF), F32) * D ** -0.5,
        "ffn_w2": nrm(ks[20], (N_DENSE, D_FF, D), F32) * (D_FF ** -0.5 * DEEPNORM_BETA),
        "moe_router_w": nrm(ks[21], (N_MOE, D, N_EXPERTS), F32) * D ** -0.5,
        "moe_router_b": nrm(ks[22], (N_MOE, N_EXPERTS), F32) * 0.01,
        "moe_w1": nrm(ks[23], (N_MOE, N_EXPERTS, D, EXPERT_FF), F32) * D ** -0.5,
        "moe_w3": nrm(ks[24], (N_MOE, N_EXPERTS, D, EXPERT_FF), F32) * D ** -0.5,
        "moe_w2": nrm(ks[25], (N_MOE, N_EXPERTS, EXPERT_FF, D), F32) * (EXPERT_FF ** -0.5 * DEEPNORM_BETA),
    }


def reference(x, c, ctx, c_ctx, ada_w, ada_b, w_in, w_out, lru_conv_w, lru_conv_b, lru_gate_w, lru_gate_b,
              lru_lam, swa_sink, diff_lam, diff_norm_g, ln_g, ln_b, ffn_w1, ffn_w3, ffn_w2,
              moe_router_w, moe_router_b, moe_w1, moe_w3, moe_w2):
    S = x.shape[1]
    rows = S // GRID_W
    rope64 = axial_rope(rows, HEAD_DIM)
    rope32 = axial_rope(rows, DIFF_QK)
    silu_c = jax.nn.silu(c)
    silu_cc = jax.nn.silu(c_ctx)
    h_lat, h_ctx = x, ctx
    for layer in range(DEPTH):
        need_ctx = layer < DEPTH - 1
        lam_init = 0.8 - 0.6 * math.exp(-0.3 * layer)
        mod_l = jnp.split((silu_c @ ada_w[layer] + ada_b[layer])[:, None, :], 6, axis=-1)
        mod_c = jnp.split(silu_cc @ ada_w[layer] + ada_b[layer], 6, axis=-1)

        u_lat = h_lat * (1.0 + mod_l[1]) + mod_l[0]
        u_ctx = h_ctx * (1.0 + mod_c[1]) + mod_c[0]
        m_ctx, m_lat = token_mix(u_ctx, u_lat, w_in[layer], w_out[layer], lru_conv_w[layer], lru_conv_b[layer],
                                 lru_gate_w[layer], lru_gate_b[layer], lru_lam[layer], swa_sink[layer],
                                 diff_lam[layer], diff_norm_g[layer], lam_init, rope64, rope32, need_ctx)
        h_lat = layer_norm(DEEPNORM_ALPHA * h_lat + mod_l[2] * m_lat, ln_g[layer, 0], ln_b[layer, 0])
        if need_ctx:
            h_ctx = layer_norm(DEEPNORM_ALPHA * h_ctx + mod_c[2] * m_ctx, ln_g[layer, 0], ln_b[layer, 0])

        u_lat = h_lat * (1.0 + mod_l[4]) + mod_l[3]
        u_ctx = h_ctx * (1.0 + mod_c[4]) + mod_c[3]
        j = layer // 2
        if layer % 2 == 0:
            f_lat = swiglu(u_lat, ffn_w1[j], ffn_w3[j], ffn_w2[j])
            f_ctx = swiglu(u_ctx, ffn_w1[j], ffn_w3[j], ffn_w2[j]) if need_ctx else None
        else:
            f_lat = moe_swiglu(u_lat, moe_router_w[j], moe_router_b[j], moe_w1[j], moe_w3[j], moe_w2[j])
            f_ctx = moe_swiglu(u_ctx, moe_router_w[j], moe_router_b[j], moe_w1[j], moe_w3[j], moe_w2[j]) if need_ctx else None
        h_lat = layer_norm(DEEPNORM_ALPHA * h_lat + mod_l[5] * f_lat, ln_g[layer, 1], ln_b[layer, 1])
        if need_ctx:
            h_ctx = layer_norm(DEEPNORM_ALPHA * h_ctx + mod_c[5] * f_ctx, ln_g[layer, 1], ln_b[layer, 1])
    return h_lat
```

```python
import functools
import math

import jax
import jax.numpy as jnp
from jax import lax
from jax.experimental import pallas as pl
from jax.experimental.pallas import tpu as pltpu

F32 = jnp.float32
BF16 = jnp.bfloat16

GRID_W = 64
HEAD_DIM = 64
LRU_WIDTH = 256
LRU_BLOCKS = 4
LRU_BLOCK = LRU_WIDTH // LRU_BLOCKS
CONV_W = 4
LRU_C = 8.0
SWA_Q_HEADS = 8
SWA_KV_HEADS = 2
SWA_GROUP = SWA_Q_HEADS // SWA_KV_HEADS
SWA_WIDTH = SWA_Q_HEADS * HEAD_DIM
ATTN_BLOCK = 128
DIFF_HEADS = 4
DIFF_QK = HEAD_DIM // 2
DIFF_WIDTH = DIFF_HEADS * HEAD_DIM
MIX_WIDTH = LRU_WIDTH + SWA_WIDTH + DIFF_WIDTH
TOP_K = 2
ROPE_BASE = 10000.0
LN_EPS = 1e-5
NEG_INF = -1e30
COL_LRU_X, COL_LRU_G, COL_SWA_Q, COL_SWA_K, COL_SWA_V, COL_DIFF_Q, COL_DIFF_K, COL_DIFF_V, IN_COLS = (
    0, 256, 512, 1024, 1152, 1280, 1536, 1792, 2048)
ATTN_COLS = IN_COLS - COL_SWA_Q
ROW_SWA_Q, ROW_SWA_KV, ROW_DIFF_Q, ROW_DIFF_K, ROW_DIFF_V = 0, 512, 768, 1024, 1280

LANES = 128
TOK_TILE = 256
VMEM_LIMIT = 56 * 1024 * 1024


def _cparams(sem, vmem=VMEM_LIMIT):
    return pltpu.CompilerParams(dimension_semantics=sem, vmem_limit_bytes=vmem)


def _const_spec(shape):
    nd = len(shape)
    return pl.BlockSpec(shape, lambda *_: (0,) * nd, pipeline_mode=pl.Buffered(1))


def _ada_kernel(c_ref, w_ref, b_ref, o_ref):
    c = c_ref[...]
    s = (c * jax.nn.sigmoid(c)).astype(BF16)
    o_ref[...] = jnp.dot(s, w_ref[...].astype(BF16), preferred_element_type=F32) + b_ref[...]


def ada_modulation(cvec, ada_w, ada_b):
    depth, d, n = ada_w.shape
    r = cvec.shape[0]
    tn = 1536
    assert n % tn == 0
    return pl.pallas_call(
        _ada_kernel,
        out_shape=jax.ShapeDtypeStruct((depth, r, n), F32),
        grid=(depth, n // tn),
        in_specs=[pl.BlockSpec((r, d), lambda l, j: (0, 0)),
                  pl.BlockSpec((None, d, tn), lambda l, j: (l, 0, j)),
                  pl.BlockSpec((None, 1, tn), lambda l, j: (l, 0, j))],
        out_specs=pl.BlockSpec((None, r, tn), lambda l, j: (l, 0, j)),
        compiler_params=_cparams(("arbitrary", "arbitrary")),
        name="ada_modulation",
    )(cvec, ada_w, ada_b.reshape(depth, 1, n))


def _rope_group(x, cos, sin_signed, half, lane_mod):
    swapped = jnp.where(lane_mod < half, pltpu.roll(x, LANES - half, axis=1), pltpu.roll(x, half, axis=1))
    return x * cos + swapped * sin_signed


def _inproj_kernel(h_ref, mod_ref, w_ref, rope_ref, lru_ref, attn_ref, *, swa_scale, diff_scale):
    h = h_ref[...]
    u = (h * (1.0 + mod_ref[1:2, :]) + mod_ref[0:1, :]).astype(BF16)
    acc = jnp.dot(u, w_ref[...], preferred_element_type=F32)
    lru_ref[...] = acc[:, :COL_SWA_Q]
    lane = lax.broadcasted_iota(jnp.int32, (1, LANES), 1)
    cos64, sin64 = rope_ref[:, 0:LANES], rope_ref[:, LANES:2 * LANES]
    cos32, sin32 = rope_ref[:, 2 * LANES:3 * LANES], rope_ref[:, 3 * LANES:4 * LANES]
    mod64, mod32 = lane % HEAD_DIM, lane % DIFF_QK
    for c0 in range(COL_SWA_Q, IN_COLS, LANES):
        x = acc[:, c0:c0 + LANES]
        if c0 < COL_SWA_V:
            x = _rope_group(x, cos64, sin64, HEAD_DIM // 2, mod64)
            if c0 < COL_SWA_K:
                x = x * swa_scale
        elif COL_DIFF_Q <= c0 < COL_DIFF_V:
            x = _rope_group(x, cos32, sin32, DIFF_QK // 2, mod32)
            if c0 < COL_DIFF_K:
                x = x * diff_scale
        r0 = c0 - COL_SWA_Q
        attn_ref[r0:r0 + LANES, :] = x.T.astype(BF16)


def input_projection(h, modtab, w_in, ropetab):
    b, t, d = h.shape
    tm = TOK_TILE
    nt = t // tm
    kern = functools.partial(_inproj_kernel, swa_scale=HEAD_DIM ** -0.5, diff_scale=DIFF_QK ** -0.5)
    return pl.pallas_call(
        kern,
        out_shape=(jax.ShapeDtypeStruct((b, t, COL_SWA_Q), F32),
                   jax.ShapeDtypeStruct((b, nt, ATTN_COLS, tm), BF16)),
        grid=(b, nt),
        in_specs=[pl.BlockSpec((None, tm, d), lambda bi, i: (bi, i, 0)),
                  pl.BlockSpec((None, None, 6, d), lambda bi, i: (bi, jnp.minimum(i, 1), 0, 0)),
                  _const_spec((d, IN_COLS)),
                  pl.BlockSpec((tm, 4 * LANES), lambda bi, i: (i, 0))],
        out_specs=(pl.BlockSpec((None, tm, COL_SWA_Q), lambda bi, i: (bi, i, 0)),
                   pl.BlockSpec((None, None, ATTN_COLS, tm), lambda bi, i: (bi, i, 0, 0))),
        compiler_params=_cparams(("arbitrary", "arbitrary")),
        name="input_projection",
    )(h, modtab, w_in, ropetab)


def _neg_expm1(z):
    series = z * (1.0 + z * (1 / 2 + z * (1 / 6 + z * (1 / 24 + z * (1 / 120 + z * (1 / 720 + z * (
        1 / 5040 + z * (1 / 40320 + z * (1 / 362880 + z * (1 / 3628800))))))))))
    return jnp.where(z > -0.5, -series, 1.0 - jnp.exp(z))


def _lru_kernel(xg_ref, cw_ref, cb_ref, wg_ref, gb_ref, lam_ref, o_ref, a_s, b_s, *, chunk, ctx_chunks):
    t, c = o_ref.shape
    nchunk = t // chunk
    row = lax.broadcasted_iota(jnp.int32, (chunk, 1), 0)
    sp = jax.nn.softplus(-lam_ref[...])
    cw = cw_ref[...]
    cb = cb_ref[...]

    def coeffs(j, carry):
        r0 = pl.multiple_of(j * chunk, chunk)
        x = xg_ref[pl.ds(r0, chunk), 0:c]
        prev8 = xg_ref[pl.ds(pl.multiple_of(jnp.maximum(r0 - 8, 0), 8), 8), 0:c]
        next8 = xg_ref[pl.ds(pl.multiple_of(jnp.minimum(r0 + chunk, t - 8), 8), 8), 0:c]
        seg_start = jnp.logical_or(j == 0, j == ctx_chunks)
        seg_end = jnp.logical_or(j == ctx_chunks - 1, j == nchunk - 1)
        pm1 = jnp.where(seg_start, 0.0, prev8[7:8, :])
        np0 = jnp.where(seg_end, 0.0, next8[0:1, :])
        np1 = jnp.where(seg_end, 0.0, next8[1:2, :])
        xm1 = jnp.where(row == 0, pm1, pltpu.roll(x, 1, axis=0))
        xp1 = jnp.where(row == chunk - 1, np0, pltpu.roll(x, chunk - 1, axis=0))
        xp2 = jnp.where(row == chunk - 1, np1, jnp.where(row == chunk - 2, np0, pltpu.roll(x, chunk - 2, axis=0)))
        u = cw[0:1, :] * xm1 + cw[1:2, :] * x + cw[2:3, :] * xp1 + cw[3:4, :] * xp2 + cb
        g = jnp.dot(u.astype(BF16), wg_ref[...], preferred_element_type=F32) + gb_ref[...]
        for d in range(2):
            r = jax.nn.sigmoid(g[:, (2 * d) * c:(2 * d + 1) * c])
            i = jax.nn.sigmoid(g[:, (2 * d + 1) * c:(2 * d + 2) * c])
            log_a = (-LRU_C) * r * sp[d:d + 1, :]
            a_s[d, pl.ds(r0, chunk), :] = jnp.exp(log_a)
            b_s[d, pl.ds(r0, chunk), :] = jnp.sqrt(_neg_expm1(2.0 * log_a)) * (i * u)
        return carry

    lax.fori_loop(0, nchunk, coeffs, 0)

    def scan_segment(lo, n8, hf, hb):
        def body(k, hh):
            hf, hb = hh
            f0 = pl.multiple_of(lo + k * 8, 8)
            r0 = pl.multiple_of(lo + (n8 - 1 - k) * 8, 8)
            for s in range(8):
                hf = a_s[0, pl.ds(f0 + s, 1), :] * hf + b_s[0, pl.ds(f0 + s, 1), :]
                b_s[0, pl.ds(f0 + s, 1), :] = hf
                hb = a_s[1, pl.ds(r0 + 7 - s, 1), :] * hb + b_s[1, pl.ds(r0 + 7 - s, 1), :]
                b_s[1, pl.ds(r0 + 7 - s, 1), :] = hb
            return hf, hb
        return lax.fori_loop(0, n8, body, (hf, hb))

    zero = jnp.zeros((1, c), F32)
    ctx_rows = ctx_chunks * chunk
    hf, hb = scan_segment(0, ctx_rows // 8, zero, zero)
    scan_segment(ctx_rows, (t - ctx_rows) // 8, hf, hb)

    def finish(j, carry):
        r0 = pl.multiple_of(j * chunk, chunk)
        y = b_s[0, pl.ds(r0, chunk), :] + b_s[1, pl.ds(r0, chunk), :]
        gate = xg_ref[pl.ds(r0, chunk), c:2 * c]
        o_ref[pl.ds(r0, chunk), :] = (y * jax.nn.gelu(gate)).astype(o_ref.dtype)
        return carry

    lax.fori_loop(0, nchunk, finish, 0)


def rglru_mix(lru_xg, conv_w, conv_b, gate_dense, gate_b, lam, ctx_len):
    b, t, c2 = lru_xg.shape
    c = c2 // 2
    chunk = TOK_TILE
    assert t % chunk == 0 and ctx_len % chunk == 0
    kern = functools.partial(_lru_kernel, chunk=chunk, ctx_chunks=ctx_len // chunk)
    return pl.pallas_call(
        kern,
        out_shape=jax.ShapeDtypeStruct((b, t, c), BF16),
        grid=(b,),
        in_specs=[pl.BlockSpec((None, t, c2), lambda bi: (bi, 0, 0)),
                  _const_spec((CONV_W, c)), _const_spec((1, c)), _const_spec((c, 4 * c)), _const_spec((1, 4 * c)),
                  _const_spec((2, c))],
        out_specs=pl.BlockSpec((None, t, c), lambda bi: (bi, 0, 0)),
        scratch_shapes=[pltpu.VMEM((2, t, c), F32), pltpu.VMEM((2, t, c), F32)],
        compiler_params=_cparams(("arbitrary",)),
        name="rglru_mix",
    )(lru_xg, conv_w, conv_b, gate_dense, gate_b, lam)


def _dot_t0(a, b):
    return lax.dot_general(a, b, (((0,), (0,)), ((), ())), preferred_element_type=F32)


def _swa_kernel(q_ref, own_ref, prev_ref, next_ref, ctx_ref, sink_ref, o_ref, ot_s):
    i = pl.program_id(1)
    nt = pl.num_programs(1)
    blk = ATTN_BLOCK
    nq = SWA_GROUP * blk
    kp = lax.broadcasted_iota(jnp.int32, (blk, nq), 0)
    qp = lax.broadcasted_iota(jnp.int32, (blk, nq), 1) % blk

    def attend(hk, half, keys, vals, masks):
        qt = jnp.concatenate(
            [q_ref[(hk * SWA_GROUP + g) * HEAD_DIM:(hk * SWA_GROUP + g + 1) * HEAD_DIM, half * blk:(half + 1) * blk]
             for g in range(SWA_GROUP)], axis=1)
        sink = sink_ref[hk]
        s = []
        for kt, msk in zip(keys, masks):
            sj = _dot_t0(kt, qt)
            s.append(sj if msk is None else jnp.where(msk, sj, NEG_INF))
        m = sink
        for sj in s:
            m = jnp.maximum(m, jnp.max(sj, axis=0, keepdims=True))
        p = [jnp.exp(sj - m) for sj in s]
        l = jnp.exp(sink - m)
        for pj in p:
            l = l + jnp.sum(pj, axis=0, keepdims=True)
        pt = jnp.concatenate([pj.astype(BF16) for pj in p], axis=0)
        vt = jnp.concatenate(vals, axis=1)
        o = jnp.dot(vt, pt, preferred_element_type=F32) / l
        for g in range(SWA_GROUP):
            f0 = (hk * SWA_GROUP + g) * HEAD_DIM
            ot_s[f0:f0 + HEAD_DIM, half * blk:(half + 1) * blk] = o[:, g * blk:(g + 1) * blk]

    def kv(ref, hk, half):
        k = ref[hk * HEAD_DIM:(hk + 1) * HEAD_DIM, half * blk:(half + 1) * blk]
        v = ref[SWA_KV_HEADS * HEAD_DIM + hk * HEAD_DIM:SWA_KV_HEADS * HEAD_DIM + (hk + 1) * HEAD_DIM,
                half * blk:(half + 1) * blk]
        return k, v

    @pl.when(i == 0)
    def _():
        for hk in range(SWA_KV_HEADS):
            k0, v0 = kv(ctx_ref, hk, 0)
            k1, v1 = kv(ctx_ref, hk, 1)
            for half in range(2):
                attend(hk, half, [k0, k1], [v0, v1], [None, None])

    @pl.when(i > 0)
    def _():
        has_prev = i > 1
        has_next = i < nt - 1
        m_prev = kp >= qp
        m_next = kp <= qp
        for hk in range(SWA_KV_HEADS):
            c0k, c0v = kv(ctx_ref, hk, 0)
            c1k, c1v = kv(ctx_ref, hk, 1)
            o0k, o0v = kv(own_ref, hk, 0)
            o1k, o1v = kv(own_ref, hk, 1)
            pk, pv = kv(prev_ref, hk, 1)
            nk, nv = kv(next_ref, hk, 0)
            attend(hk, 0, [c0k, c1k, pk, o0k, o1k], [c0v, c1v, pv, o0v, o1v],
                   [None, None, jnp.logical_and(m_prev, has_prev), None, m_next])
            attend(hk, 1, [c0k, c1k, o0k, o1k, nk], [c0v, c1v, o0v, o1v, nv],
                   [None, None, m_prev, None, jnp.logical_and(m_next, has_next)])

    o_ref[...] = ot_s[...].T.astype(o_ref.dtype)


def windowed_attention(attn_t, sink_rows):
    b, nt, _, tm = attn_t.shape
    kvrows = 2 * SWA_KV_HEADS * HEAD_DIM
    kvblk = ROW_SWA_KV // kvrows
    assert ROW_SWA_KV % kvrows == 0 and tm == 2 * ATTN_BLOCK
    return pl.pallas_call(
        _swa_kernel,
        out_shape=jax.ShapeDtypeStruct((b, nt * tm, SWA_WIDTH), BF16),
        grid=(b, nt),
        in_specs=[pl.BlockSpec((None, None, SWA_WIDTH, tm), lambda bi, i: (bi, i, 0, 0)),
                  pl.BlockSpec((None, None, kvrows, tm), lambda bi, i: (bi, i, kvblk, 0)),
                  pl.BlockSpec((None, None, kvrows, tm), lambda bi, i: (bi, jnp.maximum(i - 1, 1), kvblk, 0)),
                  pl.BlockSpec((None, None, kvrows, tm), lambda bi, i: (bi, jnp.minimum(i + 1, nt - 1), kvblk, 0)),
                  pl.BlockSpec((None, None, kvrows, tm), lambda bi, i: (bi, 0, kvblk, 0)),
                  pl.BlockSpec((SWA_KV_HEADS, 1, SWA_GROUP * ATTN_BLOCK), lambda bi, i: (0, 0, 0))],
        out_specs=pl.BlockSpec((None, tm, SWA_WIDTH), lambda bi, i: (bi, i, 0)),
        scratch_shapes=[pltpu.VMEM((SWA_WIDTH, tm), F32)],
        compiler_params=_cparams(("arbitrary", "arbitrary")),
        name="windowed_attention",
    )(attn_t, attn_t, attn_t, attn_t, attn_t, sink_rows)


def _diff_kernel(q_ref, k_ref, v_ref, lam_ref, g_ref, o_ref, m_s, l_s, acc_s, ot_s, *, lam_init):
    i = pl.program_id(1)
    nkv = jnp.where(i == 0, 1, k_ref.shape[0])
    ncomp = 2 * DIFF_HEADS
    m_s[...] = jnp.full(m_s.shape, NEG_INF, F32)
    l_s[...] = jnp.zeros(l_s.shape, F32)
    acc_s[...] = jnp.zeros(acc_s.shape, F32)

    def body(j, carry):
        kt = k_ref[j]
        vt = v_ref[j]
        for c in range(ncomp):
            h = c // 2
            s = _dot_t0(kt[c * DIFF_QK:(c + 1) * DIFF_QK, :], q_ref[c * DIFF_QK:(c + 1) * DIFF_QK, :])
            m_old = m_s[c:c + 1, :]
            m_new = jnp.maximum(m_old, jnp.max(s, axis=0, keepdims=True))
            alpha = jnp.exp(m_old - m_new)
            p = jnp.exp(s - m_new)
            l_s[c:c + 1, :] = alpha * l_s[c:c + 1, :] + jnp.sum(p, axis=0, keepdims=True)
            m_s[c:c + 1, :] = m_new
            pv = jnp.dot(vt[h * HEAD_DIM:(h + 1) * HEAD_DIM, :], p.astype(BF16), preferred_element_type=F32)
            acc_s[c] = alpha * acc_s[c] + pv
        return carry

    lax.fori_loop(0, nkv, body, 0)

    lam_v = lam_ref[...]
    lam = (jnp.exp(jnp.sum(lam_v[0:1, :] * lam_v[1:2, :], axis=1, keepdims=True))
           - jnp.exp(jnp.sum(lam_v[2:3, :] * lam_v[3:4, :], axis=1, keepdims=True)) + lam_init)
    for h in range(DIFF_HEADS):
        o = acc_s[2 * h] / l_s[2 * h:2 * h + 1, :] - lam * (acc_s[2 * h + 1] / l_s[2 * h + 1:2 * h + 2, :])
        o = o * lax.rsqrt(jnp.mean(jnp.square(o), axis=0, keepdims=True) + LN_EPS)
        ot_s[h * HEAD_DIM:(h + 1) * HEAD_DIM, :] = (o * (1.0 - lam_init)) * g_ref[...]
    o_ref[...] = ot_s[...].T.astype(o_ref.dtype)


def differential_attention(attn_t, diff_lam, norm_g, lam_init):
    b, nt, _, tm = attn_t.shape
    kern = functools.partial(_diff_kernel, lam_init=lam_init)
    ncomp = 2 * DIFF_HEADS
    return pl.pallas_call(
        kern,
        out_shape=jax.ShapeDtypeStruct((b, nt * tm, DIFF_WIDTH), BF16),
        grid=(b, nt),
        in_specs=[pl.BlockSpec((None, None, DIFF_WIDTH, tm), lambda bi, i: (bi, i, ROW_DIFF_Q // DIFF_WIDTH, 0)),
                  pl.BlockSpec((None, nt, DIFF_WIDTH, tm), lambda bi, i: (bi, 0, ROW_DIFF_K // DIFF_WIDTH, 0)),
                  pl.BlockSpec((None, nt, DIFF_WIDTH, tm), lambda bi, i: (bi, 0, ROW_DIFF_V // DIFF_WIDTH, 0)),
                  pl.BlockSpec((4, DIFF_QK), lambda bi, i: (0, 0)),
                  pl.BlockSpec((HEAD_DIM, 1), lambda bi, i: (0, 0))],
        out_specs=pl.BlockSpec((None, tm, DIFF_WIDTH), lambda bi, i: (bi, i, 0)),
        scratch_shapes=[pltpu.VMEM((ncomp, tm), F32), pltpu.VMEM((ncomp, tm), F32),
                        pltpu.VMEM((ncomp, HEAD_DIM, tm), F32), pltpu.VMEM((DIFF_WIDTH, tm), F32)],
        compiler_params=_cparams(("arbitrary", "arbitrary")),
        name="differential_attention",
    )(attn_t, attn_t, attn_t, diff_lam, norm_g.reshape(HEAD_DIM, 1))


def _mod_rows(mod_ref, k, is_ctx):
    return jnp.where(is_ctx, mod_ref[0, k:k + 1, :], mod_ref[1, k:k + 1, :])


def _is_ctx_rows(tm, ctx_len):
    r0 = pl.program_id(1) * tm
    return (r0 + lax.broadcasted_iota(jnp.int32, (tm, 1), 0)) < ctx_len


def _deepnorm_ln(h, branch, gate, g, b, alpha):
    x = alpha * h + gate * branch
    mu = jnp.mean(x, axis=-1, keepdims=True)
    xc = x - mu
    var = jnp.mean(xc * xc, axis=-1, keepdims=True)
    return xc * lax.rsqrt(var + LN_EPS) * g + b


def _outproj_kernel(lru_ref, swa_ref, diff_ref, h_ref, mod_ref, w_ref, g_ref, b_ref, o_ref, *, alpha, ctx_len):
    tm = h_ref.shape[0]
    is_ctx = _is_ctx_rows(tm, ctx_len)
    m = jnp.dot(lru_ref[...], w_ref[0:LRU_WIDTH, :], preferred_element_type=F32)
    m = m + jnp.dot(swa_ref[...], w_ref[LRU_WIDTH:LRU_WIDTH + SWA_WIDTH, :], preferred_element_type=F32)
    m = m + jnp.dot(diff_ref[...], w_ref[LRU_WIDTH + SWA_WIDTH:MIX_WIDTH, :], preferred_element_type=F32)
    o_ref[...] = _deepnorm_ln(h_ref[...], m, _mod_rows(mod_ref, 2, is_ctx), g_ref[...], b_ref[...], alpha)


def _row_tile(t):
    for tm in range(640, 15, -16):
        if t % tm == 0:
            return tm
    raise ValueError(f"no row tile for T={t}")


def output_projection(lru, swa, diff, h, modtab, w_out, ln_g, ln_b, alpha, ctx_len):
    b, t, d = h.shape
    tm = _row_tile(t)
    kern = functools.partial(_outproj_kernel, alpha=alpha, ctx_len=ctx_len)
    row = lambda w: pl.BlockSpec((None, tm, w), lambda bi, i: (bi, i, 0))
    return pl.pallas_call(
        kern,
        out_shape=jax.ShapeDtypeStruct((b, t, d), F32),
        grid=(b, t // tm),
        in_specs=[row(LRU_WIDTH), row(SWA_WIDTH), row(DIFF_WIDTH), row(d),
                  pl.BlockSpec((None, 2, 6, d), lambda bi, i: (bi, 0, 0, 0)),
                  _const_spec((MIX_WIDTH, d)), _const_spec((1, d)), _const_spec((1, d))],
        out_specs=row(d),
        compiler_params=_cparams(("arbitrary", "arbitrary")),
        name="output_projection",
    )(lru, swa, diff, h, modtab, w_out, ln_g, ln_b)


FF_CHUNK = 256


def _swiglu_hidden(u, w1_ref, w3_ref, g_s, scale=None):
    ff = w1_ref.shape[-1]
    for c0 in range(0, ff, FF_CHUNK):
        h1 = jnp.dot(u, w1_ref[:, c0:c0 + FF_CHUNK], preferred_element_type=F32)
        h3 = jnp.dot(u, w3_ref[:, c0:c0 + FF_CHUNK], preferred_element_type=F32)
        g = (h1 * jax.nn.sigmoid(h1)) * h3
        if scale is not None:
            g = g * scale
        g_s[:, c0:c0 + FF_CHUNK] = g.astype(BF16)


def _ffn_kernel(h_ref, mod_ref, w1_ref, w3_ref, w2_ref, g_ref, b_ref, o_ref, g_s, *, alpha, ctx_len):
    tm = h_ref.shape[0]
    is_ctx = _is_ctx_rows(tm, ctx_len)
    h = h_ref[...]
    u = (h * (1.0 + _mod_rows(mod_ref, 4, is_ctx)) + _mod_rows(mod_ref, 3, is_ctx)).astype(BF16)
    _swiglu_hidden(u, w1_ref, w3_ref, g_s)
    f = jnp.dot(g_s[...], w2_ref[...], preferred_element_type=F32)
    o_ref[...] = _deepnorm_ln(h, f, _mod_rows(mod_ref, 5, is_ctx), g_ref[...], b_ref[...], alpha)


def dense_ffn(h, modtab, w1, w3, w2, ln_g, ln_b, alpha, ctx_len):
    b, t, d = h.shape
    ff = w1.shape[1]
    assert ff % FF_CHUNK == 0
    tm = _row_tile(t)
    kern = functools.partial(_ffn_kernel, alpha=alpha, ctx_len=ctx_len)
    row = pl.BlockSpec((None, tm, d), lambda bi, i: (bi, i, 0))
    return pl.pallas_call(
        kern,
        out_shape=jax.ShapeDtypeStruct((b, t, d), F32),
        grid=(b, t // tm),
        in_specs=[row, pl.BlockSpec((None, 2, 6, d), lambda bi, i: (bi, 0, 0, 0)),
                  _const_spec((d, ff)), _const_spec((d, ff)), _const_spec((ff, d)),
                  _const_spec((1, d)), _const_spec((1, d))],
        out_specs=row,
        scratch_shapes=[pltpu.VMEM((tm, ff), BF16)],
        compiler_params=_cparams(("arbitrary", "arbitrary")),
        name="dense_ffn",
    )(h, modtab, w1, w3, w2, ln_g, ln_b)


def _top2_combine(logits, n_experts):
    lane = lax.broadcasted_iota(jnp.int32, logits.shape, 1)
    m1 = jnp.max(logits, axis=-1, keepdims=True)
    i1 = jnp.min(jnp.where(logits == m1, lane, LANES), axis=-1, keepdims=True)
    rest = jnp.where(lane == i1, -jnp.inf, logits)
    m2 = jnp.max(rest, axis=-1, keepdims=True)
    i2 = jnp.min(jnp.where(rest == m2, lane, LANES), axis=-1, keepdims=True)
    e = jnp.exp(m2 - m1)
    g1 = 1.0 / (1.0 + e)
    g2 = e / (1.0 + e)
    return jnp.where(lane == i1, g1, 0.0) + jnp.where(lane == i2, g2, 0.0)


def _moe_kernel(h_ref, mod_ref, rw_ref, rb_ref, w1_ref, w3_ref, w2_ref, g_ref, b_ref, o_ref,
                u_s, comb_s, acc_s, g_s, *, alpha, ctx_len, n_experts):
    e = pl.program_id(2)
    tm = h_ref.shape[0]
    is_ctx = _is_ctx_rows(tm, ctx_len)

    @pl.when(e == 0)
    def _():
        h = h_ref[...]
        u = (h * (1.0 + _mod_rows(mod_ref, 4, is_ctx)) + _mod_rows(mod_ref, 3, is_ctx)).astype(BF16)
        u_s[...] = u
        logits = jnp.dot(u, rw_ref[...], preferred_element_type=F32) + rb_ref[...]
        comb_s[...] = _top2_combine(logits, n_experts)
        acc_s[...] = jnp.zeros(acc_s.shape, F32)

    lane = lax.broadcasted_iota(jnp.int32, comb_s.shape, 1)
    ce = jnp.sum(jnp.where(lane == e, comb_s[...], 0.0), axis=-1, keepdims=True)
    _swiglu_hidden(u_s[...], w1_ref, w3_ref, g_s)
    acc_s[...] += ce * jnp.dot(g_s[...], w2_ref[...], preferred_element_type=F32)

    @pl.when(e == n_experts - 1)
    def _():
        o_ref[...] = _deepnorm_ln(h_ref[...], acc_s[...], _mod_rows(mod_ref, 5, is_ctx), g_ref[...], b_ref[...], alpha)


def moe_ffn(h, modtab, router_w, router_b, w1, w3, w2, ln_g, ln_b, alpha, ctx_len):
    b, t, d = h.shape
    n_experts, _, ff = w1.shape
    tm = _row_tile(t)
    kern = functools.partial(_moe_kernel, alpha=alpha, ctx_len=ctx_len, n_experts=n_experts)
    row = pl.BlockSpec((None, tm, d), lambda bi, i, e: (bi, i, 0))
    cst = lambda shape: pl.BlockSpec(shape, lambda bi, i, e: (0,) * len(shape), pipeline_mode=pl.Buffered(1))
    return pl.pallas_call(
        kern,
        out_shape=jax.ShapeDtypeStruct((b, t, d), F32),
        grid=(b, t // tm, n_experts),
        in_specs=[row, pl.BlockSpec((None, 2, 6, d), lambda bi, i, e: (bi, 0, 0, 0)),
                  cst((d, LANES)), cst((1, LANES)),
                  pl.BlockSpec((None, d, ff), lambda bi, i, e: (e, 0, 0)),
                  pl.BlockSpec((None, d, ff), lambda bi, i, e: (e, 0, 0)),
                  pl.BlockSpec((None, ff, d), lambda bi, i, e: (e, 0, 0)),
                  cst((1, d)), cst((1, d))],
        out_specs=row,
        scratch_shapes=[pltpu.VMEM((tm, d), BF16), pltpu.VMEM((tm, LANES), F32), pltpu.VMEM((tm, d), F32),
                        pltpu.VMEM((tm, ff), BF16)],
        compiler_params=_cparams(("arbitrary", "arbitrary", "arbitrary")),
        name="moe_ffn",
    )(h, modtab, router_w, router_b, w1, w3, w2, ln_g, ln_b)


def _rope_table(rows, ctx_len):
    def table(rot_dim):
        n_freq = rot_dim // 4
        inv = ROPE_BASE ** (-jnp.arange(n_freq, dtype=F32) / n_freq)
        row = jnp.repeat(jnp.arange(rows, dtype=F32), GRID_W)
        col = jnp.tile(jnp.arange(GRID_W, dtype=F32), rows)
        ang = jnp.concatenate([row[:, None] * inv, col[:, None] * inv], -1)
        cos, sin = jnp.cos(ang), jnp.sin(ang)
        reps = LANES // rot_dim
        cos_t = jnp.tile(jnp.concatenate([cos, cos], -1), (1, reps))
        sin_t = jnp.tile(jnp.concatenate([-sin, sin], -1), (1, reps))
        ident = (jnp.ones((ctx_len, LANES), F32), jnp.zeros((ctx_len, LANES), F32))
        return jnp.concatenate([ident[0], cos_t], 0), jnp.concatenate([ident[1], sin_t], 0)
    c64, s64 = table(HEAD_DIM)
    c32, s32 = table(DIFF_QK)
    return jnp.concatenate([c64, s64, c32, s32], axis=-1)


def _gate_dense(gate_w):
    cols = []
    for d in range(2):
        for g in range(2):
            cols.append(jax.scipy.linalg.block_diag(*[gate_w[d, g, k] for k in range(LRU_BLOCKS)]))
    return jnp.concatenate(cols, axis=1)


def kernel(x, c, ctx, c_ctx, ada_w, ada_b, w_in, w_out, lru_conv_w, lru_conv_b, lru_gate_w, lru_gate_b, lru_lam,
           swa_sink, diff_lam, diff_norm_g, ln_g, ln_b, ffn_w1, ffn_w3, ffn_w2, moe_router_w, moe_router_b,
           moe_w1, moe_w3, moe_w2):
    b, s, d = x.shape
    ctx_len = ctx.shape[1]
    depth = ada_w.shape[0]
    n_experts = moe_router_w.shape[-1]
    alpha = (2.0 * depth) ** 0.25
    assert s % GRID_W == 0 and s % TOK_TILE == 0 and ctx_len == TOK_TILE

    h = jnp.concatenate([ctx, x], axis=1)
    ropetab = _rope_table(s // GRID_W, ctx_len)

    rows = 8 * ((b + 1 + 7) // 8)
    cvec = jnp.zeros((rows, d), F32).at[:b].set(c).at[b].set(c_ctx)
    mods = ada_modulation(cvec, ada_w, ada_b).reshape(depth, rows, 6, d)
    modtab = jnp.stack([jnp.broadcast_to(mods[:, b:b + 1], (depth, b, 6, d)), mods[:, :b]], axis=2)

    sink_rows = jnp.repeat(swa_sink.reshape(depth, SWA_KV_HEADS, 1, SWA_GROUP), ATTN_BLOCK, axis=-1)

    for layer in range(depth):
        lam_init = 0.8 - 0.6 * math.exp(-0.3 * layer)
        mt = modtab[layer]
        lru_xg, attn_t = input_projection(h, mt, w_in[layer].astype(BF16), ropetab)
        lru = rglru_mix(lru_xg, lru_conv_w[layer], lru_conv_b[layer].reshape(1, LRU_WIDTH),
                        _gate_dense(lru_gate_w[layer]).astype(BF16), lru_gate_b[layer].reshape(1, 4 * LRU_WIDTH),
                        lru_lam[layer], ctx_len)
        swa = windowed_attention(attn_t, sink_rows[layer])
        dif = differential_attention(attn_t, diff_lam[layer], diff_norm_g[layer], lam_init)
        h = output_projection(lru, swa, dif, h, mt, w_out[layer].astype(BF16), ln_g[layer, 0:1], ln_b[layer, 0:1],
                              alpha, ctx_len)
        j = layer // 2
        if layer % 2 == 0:
            h = dense_ffn(h, mt, ffn_w1[j].astype(BF16), ffn_w3[j].astype(BF16), ffn_w2[j].astype(BF16),
                          ln_g[layer, 1:2], ln_b[layer, 1:2], alpha, ctx_len)
        else:
            rw = jnp.zeros((d, LANES), F32).at[:, :n_experts].set(moe_router_w[j]).astype(BF16)
            rb = jnp.full((1, LANES), NEG_INF, F32).at[0, :n_experts].set(moe_router_b[j])
            h = moe_ffn(h, mt, rw, rb, moe_w1[j].astype(BF16), moe_w3[j].astype(BF16), moe_w2[j].astype(BF16),
                        ln_g[layer, 1:2], ln_b[layer, 1:2], alpha, ctx_len)
    return h[:, ctx_len:, :]
```

```python
import functools
import math

import jax
import jax.numpy as jnp
from jax import lax
from jax.experimental import pallas as pl
from jax.experimental.pallas import tpu as pltpu

F32 = jnp.float32
BF16 = jnp.bfloat16

GRID_W = 64
HEAD_DIM = 64
LRU_WIDTH = 256
LRU_BLOCKS = 4
LRU_BLOCK = LRU_WIDTH // LRU_BLOCKS
CONV_W = 4
LRU_C = 8.0
SWA_Q_HEADS = 8
SWA_KV_HEADS = 2
SWA_GROUP = SWA_Q_HEADS // SWA_KV_HEADS
SWA_WIDTH = SWA_Q_HEADS * HEAD_DIM
ATTN_BLOCK = 128
DIFF_HEADS = 4
DIFF_QK = HEAD_DIM // 2
DIFF_WIDTH = DIFF_HEADS * HEAD_DIM
MIX_WIDTH = LRU_WIDTH + SWA_WIDTH + DIFF_WIDTH
TOP_K = 2
ROPE_BASE = 10000.0
LN_EPS = 1e-5
NEG_INF = -1e30
COL_LRU_X, COL_LRU_G, COL_SWA_Q, COL_SWA_K, COL_SWA_V, COL_DIFF_Q, COL_DIFF_K, COL_DIFF_V, IN_COLS = (
    0, 256, 512, 1024, 1152, 1280, 1536, 1792, 2048)
ATTN_COLS = IN_COLS - COL_SWA_Q
ROW_SWA_Q, ROW_SWA_KV, ROW_DIFF_Q, ROW_DIFF_K, ROW_DIFF_V = 0, 512, 768, 1024, 1280

LANES = 128
TOK_TILE = 256
VMEM_LIMIT = 56 * 1024 * 1024


def _cparams(sem, vmem=VMEM_LIMIT):
    return pltpu.CompilerParams(dimension_semantics=sem, vmem_limit_bytes=vmem)


def _const_spec(shape):
    nd = len(shape)
    return pl.BlockSpec(shape, lambda *_: (0,) * nd, pipeline_mode=pl.Buffered(1))


def _ada_kernel(c_ref, w_ref, b_ref, o_ref):
    c = c_ref[...]
    s = (c * jax.nn.sigmoid(c)).astype(BF16)
    o_ref[...] = jnp.dot(s, w_ref[...].astype(BF16), preferred_element_type=F32) + b_ref[...]


def ada_modulation(cvec, ada_w, ada_b):
    depth, d, n = ada_w.shape
    r = cvec.shape[0]
    tn = 1536
    assert n % tn == 0
    return pl.pallas_call(
        _ada_kernel,
        out_shape=jax.ShapeDtypeStruct((depth, r, n), F32),
        grid=(depth, n // tn),
        in_specs=[pl.BlockSpec((r, d), lambda l, j: (0, 0)),
                  pl.BlockSpec((None, d, tn), lambda l, j: (l, 0, j)),
                  pl.BlockSpec((None, 1, tn), lambda l, j: (l, 0, j))],
        out_specs=pl.BlockSpec((None, r, tn), lambda l, j: (l, 0, j)),
        compiler_params=_cparams(("arbitrary", "arbitrary")),
        name="ada_modulation",
    )(cvec, ada_w, ada_b.reshape(depth, 1, n))


def _rope_group(x, cos, sin_signed, half, lane_mod):
    swapped = jnp.where(lane_mod < half, pltpu.roll(x, LANES - half, axis=1), pltpu.roll(x, half, axis=1))
    return x * cos + swapped * sin_signed


def _inproj_kernel(h_ref, mod_ref, w_ref, rope_ref, lru_ref, attn_ref, *, swa_scale, diff_scale):
    h = h_ref[...]
    u = (h * (1.0 + mod_ref[1:2, :]) + mod_ref[0:1, :]).astype(BF16)
    acc = jnp.dot(u, w_ref[...], preferred_element_type=F32)
    lru_ref[...] = acc[:, :COL_SWA_Q]
    lane = lax.broadcasted_iota(jnp.int32, (1, LANES), 1)
    cos64, sin64 = rope_ref[:, 0:LANES], rope_ref[:, LANES:2 * LANES]
    cos32, sin32 = rope_ref[:, 2 * LANES:3 * LANES], rope_ref[:, 3 * LANES:4 * LANES]
    mod64, mod32 = lane % HEAD_DIM, lane % DIFF_QK
    for c0 in range(COL_SWA_Q, IN_COLS, LANES):
        x = acc[:, c0:c0 + LANES]
        if c0 < COL_SWA_V:
            x = _rope_group(x, cos64, sin64, HEAD_DIM // 2, mod64)
            if c0 < COL_SWA_K:
                x = x * swa_scale
        elif COL_DIFF_Q <= c0 < COL_DIFF_V:
            x = _rope_group(x, cos32, sin32, DIFF_QK // 2, mod32)
            if c0 < COL_DIFF_K:
                x = x * diff_scale
        r0 = c0 - COL_SWA_Q
        attn_ref[r0:r0 + LANES, :] = x.T.astype(BF16)


def input_projection(h, modtab, w_in, ropetab):
    b, t, d = h.shape
    tm = TOK_TILE
    nt = t // tm
    kern = functools.partial(_inproj_kernel, swa_scale=HEAD_DIM ** -0.5, diff_scale=DIFF_QK ** -0.5 * math.log2(math.e))
    return pl.pallas_call(
        kern,
        out_shape=(jax.ShapeDtypeStruct((b, t, COL_SWA_Q), F32),
                   jax.ShapeDtypeStruct((b, nt, ATTN_COLS, tm), BF16)),
        grid=(b, nt),
        in_specs=[pl.BlockSpec((None, tm, d), lambda bi, i: (bi, i, 0)),
                  pl.BlockSpec((None, None, 6, d), lambda bi, i: (bi, jnp.minimum(i, 1), 0, 0)),
                  _const_spec((d, IN_COLS)),
                  pl.BlockSpec((tm, 4 * LANES), lambda bi, i: (i, 0))],
        out_specs=(pl.BlockSpec((None, tm, COL_SWA_Q), lambda bi, i: (bi, i, 0)),
                   pl.BlockSpec((None, None, ATTN_COLS, tm), lambda bi, i: (bi, i, 0, 0))),
        compiler_params=_cparams(("arbitrary", "arbitrary")),
        name="input_projection",
    )(h, modtab, w_in, ropetab)


def _neg_expm1(z):
    series = z * (1.0 + z * (1 / 2 + z * (1 / 6 + z * (1 / 24 + z * (1 / 120 + z * (1 / 720 + z * (1 / 5040)))))))
    return jnp.where(z > -0.125, -series, 1.0 - jnp.exp(z))


def _lru_kernel(xg_ref, cw_ref, cb_ref, wg_ref, gb_ref, lam_ref, o_ref, a_s, b_s, h_s, *, chunk, ctx_chunks):
    t, c = o_ref.shape
    nchunk = t // chunk
    row = lax.broadcasted_iota(jnp.int32, (chunk, 1), 0)
    sp = jax.nn.softplus(-lam_ref[...])
    cw = cw_ref[...]
    cb = cb_ref[...]

    def coeffs(j, carry):
        r0 = pl.multiple_of(j * chunk, chunk)
        x = xg_ref[pl.ds(r0, chunk), 0:c]
        prev8 = xg_ref[pl.ds(pl.multiple_of(jnp.maximum(r0 - 8, 0), 8), 8), 0:c]
        next8 = xg_ref[pl.ds(pl.multiple_of(jnp.minimum(r0 + chunk, t - 8), 8), 8), 0:c]
        seg_start = jnp.logical_or(j == 0, j == ctx_chunks)
        seg_end = jnp.logical_or(j == ctx_chunks - 1, j == nchunk - 1)
        pm1 = jnp.where(seg_start, 0.0, prev8[7:8, :])
        np0 = jnp.where(seg_end, 0.0, next8[0:1, :])
        np1 = jnp.where(seg_end, 0.0, next8[1:2, :])
        xm1 = jnp.where(row == 0, pm1, pltpu.roll(x, 1, axis=0))
        xp1 = jnp.where(row == chunk - 1, np0, pltpu.roll(x, chunk - 1, axis=0))
        xp2 = jnp.where(row == chunk - 1, np1, jnp.where(row == chunk - 2, np0, pltpu.roll(x, chunk - 2, axis=0)))
        u = cw[0:1, :] * xm1 + cw[1:2, :] * x + cw[2:3, :] * xp1 + cw[3:4, :] * xp2 + cb
        g = jnp.dot(u.astype(BF16), wg_ref[...], preferred_element_type=F32) + gb_ref[...]
        for d in range(2):
            r = jax.nn.sigmoid(g[:, (2 * d) * c:(2 * d + 1) * c])
            i = jax.nn.sigmoid(g[:, (2 * d + 1) * c:(2 * d + 2) * c])
            log_a = (-LRU_C) * r * sp[d:d + 1, :]
            a_s[d, pl.ds(r0, chunk), :] = jnp.exp(log_a)
            b_s[d, pl.ds(r0, chunk), :] = jnp.sqrt(_neg_expm1(2.0 * log_a)) * (i * u)
        return carry

    lax.fori_loop(0, nchunk, coeffs, 0)

    def scan_segment(lo, n8, hf, hb):
        def body(k, hh):
            hf, hb = hh
            f0 = pl.multiple_of(lo + k * 8, 8)
            r0 = pl.multiple_of(lo + (n8 - 1 - k) * 8, 8)
            af, bf = a_s[0, pl.ds(f0, 8), :], b_s[0, pl.ds(f0, 8), :]
            ab, bb = a_s[1, pl.ds(r0, 8), :], b_s[1, pl.ds(r0, 8), :]
            fw, bw = [], []
            for s in range(8):
                hf = af[s:s + 1, :] * hf + bf[s:s + 1, :]
                fw.append(hf)
                hb = ab[7 - s:8 - s, :] * hb + bb[7 - s:8 - s, :]
                bw.append(hb)
            h_s[0, pl.ds(f0, 8), :] = jnp.concatenate(fw, axis=0)
            h_s[1, pl.ds(r0, 8), :] = jnp.concatenate(bw[::-1], axis=0)
            return hf, hb
        return lax.fori_loop(0, n8, body, (hf, hb))

    zero = jnp.zeros((1, c), F32)
    ctx_rows = ctx_chunks * chunk
    hf, hb = scan_segment(0, ctx_rows // 8, zero, zero)
    scan_segment(ctx_rows, (t - ctx_rows) // 8, hf, hb)

    def finish(j, carry):
        r0 = pl.multiple_of(j * chunk, chunk)
        y = h_s[0, pl.ds(r0, chunk), :] + h_s[1, pl.ds(r0, chunk), :]
        gate = xg_ref[pl.ds(r0, chunk), c:2 * c]
        o_ref[pl.ds(r0, chunk), :] = (y * jax.nn.gelu(gate)).astype(o_ref.dtype)
        return carry

    lax.fori_loop(0, nchunk, finish, 0)


def rglru_mix(lru_xg, conv_w, conv_b, gate_dense, gate_b, lam, ctx_len):
    b, t, c2 = lru_xg.shape
    c = c2 // 2
    chunk = TOK_TILE
    assert t % chunk == 0 and ctx_len % chunk == 0
    kern = functools.partial(_lru_kernel, chunk=chunk, ctx_chunks=ctx_len // chunk)
    return pl.pallas_call(
        kern,
        out_shape=jax.ShapeDtypeStruct((b, t, c), BF16),
        grid=(b,),
        in_specs=[pl.BlockSpec((None, t, c2), lambda bi: (bi, 0, 0), pipeline_mode=pl.Buffered(1)),
                  _const_spec((CONV_W, c)), _const_spec((1, c)), _const_spec((c, 4 * c)), _const_spec((1, 4 * c)),
                  _const_spec((2, c))],
        out_specs=pl.BlockSpec((None, t, c), lambda bi: (bi, 0, 0)),
        scratch_shapes=[pltpu.VMEM((2, t, c), F32), pltpu.VMEM((2, t, c), F32), pltpu.VMEM((2, t, c), F32)],
        compiler_params=_cparams(("arbitrary",)),
        name="rglru_mix",
    )(lru_xg, conv_w, conv_b, gate_dense, gate_b, lam)


def _dot_t0(a, b):
    return lax.dot_general(a, b, (((0,), (0,)), ((), ())), preferred_element_type=F32)


SWA_MAX_KEY_BLOCKS = 5


def _swa_kernel(q_ref, own_ref, prev_ref, next_ref, ctx_ref, sink_ref, o_ref, s_s, p_s, l_s, ot_s):
    i = pl.program_id(1)
    nt = pl.num_programs(1)
    blk = ATTN_BLOCK
    nq = SWA_GROUP * blk
    kp = lax.broadcasted_iota(jnp.int32, (blk, nq), 0)
    qp = lax.broadcasted_iota(jnp.int32, (blk, nq), 1) % blk

    def kv(ref, hk, half):
        k = ref[hk * HEAD_DIM:(hk + 1) * HEAD_DIM, half * blk:(half + 1) * blk]
        v = ref[SWA_KV_HEADS * HEAD_DIM + hk * HEAD_DIM:SWA_KV_HEADS * HEAD_DIM + (hk + 1) * HEAD_DIM,
                half * blk:(half + 1) * blk]
        return k, v

    def run(jobs):
        for a, (hk, half, keys, _, _) in enumerate(jobs):
            qt = jnp.concatenate(
                [q_ref[(hk * SWA_GROUP + g) * HEAD_DIM:(hk * SWA_GROUP + g + 1) * HEAD_DIM,
                       half * blk:(half + 1) * blk] for g in range(SWA_GROUP)], axis=1)
            for j, kt in enumerate(keys):
                s_s[a, j * blk:(j + 1) * blk, :] = _dot_t0(kt, qt)
        for a, (hk, _, keys, _, masks) in enumerate(jobs):
            sink = sink_ref[hk]
            s = []
            for j, msk in enumerate(masks):
                sj = s_s[a, j * blk:(j + 1) * blk, :]
                s.append(sj if msk is None else jnp.where(msk, sj, NEG_INF))
            m = sink
            for sj in s:
                m = jnp.maximum(m, jnp.max(sj, axis=0, keepdims=True))
            l = jnp.exp(sink - m)
            for j, sj in enumerate(s):
                pj = jnp.exp(sj - m)
                l = l + jnp.sum(pj, axis=0, keepdims=True)
                p_s[a, j * blk:(j + 1) * blk, :] = pj.astype(BF16)
            l_s[a:a + 1, :] = l
        for a, (hk, half, keys, vals, _) in enumerate(jobs):
            nk = len(keys) * blk
            vt = jnp.concatenate(vals, axis=1)
            o = jnp.dot(vt, p_s[a, 0:nk, :], preferred_element_type=F32) / l_s[a:a + 1, :]
            for g in range(SWA_GROUP):
                f0 = (hk * SWA_GROUP + g) * HEAD_DIM
                ot_s[f0:f0 + HEAD_DIM, half * blk:(half + 1) * blk] = o[:, g * blk:(g + 1) * blk]

    @pl.when(i == 0)
    def _():
        jobs = []
        for hk in range(SWA_KV_HEADS):
            k0, v0 = kv(ctx_ref, hk, 0)
            k1, v1 = kv(ctx_ref, hk, 1)
            for half in range(2):
                jobs.append((hk, half, [k0, k1], [v0, v1], [None, None]))
        run(jobs)

    @pl.when(i > 0)
    def _():
        has_prev = i > 1
        has_next = i < nt - 1
        m_prev = kp >= qp
        m_next = kp <= qp
        jobs = []
        for hk in range(SWA_KV_HEADS):
            c0k, c0v = kv(ctx_ref, hk, 0)
            c1k, c1v = kv(ctx_ref, hk, 1)
            o0k, o0v = kv(own_ref, hk, 0)
            o1k, o1v = kv(own_ref, hk, 1)
            pk, pv = kv(prev_ref, hk, 1)
            nk, nv = kv(next_ref, hk, 0)
            jobs.append((hk, 0, [c0k, c1k, pk, o0k, o1k], [c0v, c1v, pv, o0v, o1v],
                         [None, None, jnp.logical_and(m_prev, has_prev), None, m_next]))
            jobs.append((hk, 1, [c0k, c1k, o0k, o1k, nk], [c0v, c1v, o0v, o1v, nv],
                         [None, None, m_prev, None, jnp.logical_and(m_next, has_next)]))
        run(jobs)

    o_ref[...] = ot_s[...].T.astype(o_ref.dtype)


def windowed_attention(attn_t, sink_rows):
    b, nt, _, tm = attn_t.shape
    kvrows = 2 * SWA_KV_HEADS * HEAD_DIM
    kvblk = ROW_SWA_KV // kvrows
    assert ROW_SWA_KV % kvrows == 0 and tm == 2 * ATTN_BLOCK
    return pl.pallas_call(
        _swa_kernel,
        out_shape=jax.ShapeDtypeStruct((b, nt * tm, SWA_WIDTH), BF16),
        grid=(b, nt),
        in_specs=[pl.BlockSpec((None, None, SWA_WIDTH, tm), lambda bi, i: (bi, i, 0, 0)),
                  pl.BlockSpec((None, None, kvrows, tm), lambda bi, i: (bi, i, kvblk, 0)),
                  pl.BlockSpec((None, None, kvrows, tm), lambda bi, i: (bi, jnp.maximum(i - 1, 1), kvblk, 0)),
                  pl.BlockSpec((None, None, kvrows, tm), lambda bi, i: (bi, jnp.minimum(i + 1, nt - 1), kvblk, 0)),
                  pl.BlockSpec((None, None, kvrows, tm), lambda bi, i: (bi, 0, kvblk, 0)),
                  pl.BlockSpec((SWA_KV_HEADS, 1, SWA_GROUP * ATTN_BLOCK), lambda bi, i: (0, 0, 0))],
        out_specs=pl.BlockSpec((None, tm, SWA_WIDTH), lambda bi, i: (bi, i, 0)),
        scratch_shapes=[pltpu.VMEM((2 * SWA_KV_HEADS, SWA_MAX_KEY_BLOCKS * ATTN_BLOCK, SWA_GROUP * ATTN_BLOCK), F32),
                        pltpu.VMEM((2 * SWA_KV_HEADS, SWA_MAX_KEY_BLOCKS * ATTN_BLOCK, SWA_GROUP * ATTN_BLOCK), BF16),
                        pltpu.VMEM((2 * SWA_KV_HEADS, SWA_GROUP * ATTN_BLOCK), F32),
                        pltpu.VMEM((SWA_WIDTH, tm), F32)],
        compiler_params=_cparams(("arbitrary", "arbitrary")),
        name="windowed_attention",
    )(attn_t, attn_t, attn_t, attn_t, attn_t, sink_rows)


DIFF_ACC_ROWS = HEAD_DIM + 16


def _diff_kernel(q_ref, k_ref, v_ref, lam_ref, g_ref, o_ref, m_s, a_s, acc_s, s_s, p_s, ot_s, *, lam_init):
    i = pl.program_id(1)
    nkv = jnp.where(i == 0, 1, k_ref.shape[0])
    ncomp = 2 * DIFF_HEADS
    m_s[...] = jnp.full(m_s.shape, NEG_INF, F32)
    acc_s[...] = jnp.zeros(acc_s.shape, F32)
    ones = jnp.ones((DIFF_ACC_ROWS - HEAD_DIM, k_ref.shape[-1]), BF16)

    def scores(j, slot):
        kt = k_ref[j]
        for c in range(ncomp):
            s_s[slot, c] = _dot_t0(kt[c * DIFF_QK:(c + 1) * DIFF_QK, :], q_ref[c * DIFF_QK:(c + 1) * DIFF_QK, :])

    def softmax(slot):
        for c in range(ncomp):
            s = s_s[slot, c]
            m_old = m_s[c:c + 1, :]
            m_new = jnp.maximum(m_old, jnp.max(s, axis=0, keepdims=True))
            m_s[c:c + 1, :] = m_new
            a_s[slot, c:c + 1, :] = jnp.exp2(m_old - m_new)
            p_s[slot, c] = jnp.exp2(s - m_new).astype(BF16)

    def values(j, slot):
        vt = v_ref[j]
        for c in range(ncomp):
            h = c // 2
            v1 = jnp.concatenate([vt[h * HEAD_DIM:(h + 1) * HEAD_DIM, :], ones], axis=0)
            acc_s[c] = a_s[slot, c:c + 1, :] * acc_s[c] + jnp.dot(v1, p_s[slot, c], preferred_element_type=F32)

    last = nkv - 1
    scores(0, 0)
    scores(jnp.minimum(1, last), 1)
    softmax(0)

    def pair(jj, carry):
        j = 2 * jj
        scores(j + 2, 0)
        softmax(1)
        values(j, 0)
        scores(jnp.minimum(j + 3, last), 1)
        softmax(0)
        values(j + 1, 1)
        return carry

    lax.fori_loop(0, last // 2, pair, 0)
    values(last, 0)

    lam_v = lam_ref[...]
    lam = (jnp.exp(jnp.sum(lam_v[0:1, :] * lam_v[1:2, :], axis=1, keepdims=True))
           - jnp.exp(jnp.sum(lam_v[2:3, :] * lam_v[3:4, :], axis=1, keepdims=True)) + lam_init)

    def head_out(c):
        return acc_s[c, 0:HEAD_DIM, :] / acc_s[c, HEAD_DIM:HEAD_DIM + 1, :]

    for h in range(DIFF_HEADS):
        o = head_out(2 * h) - lam * head_out(2 * h + 1)
        o = o * lax.rsqrt(jnp.mean(jnp.square(o), axis=0, keepdims=True) + LN_EPS)
        ot_s[h * HEAD_DIM:(h + 1) * HEAD_DIM, :] = (o * (1.0 - lam_init)) * g_ref[...]
    o_ref[...] = ot_s[...].T.astype(o_ref.dtype)


def differential_attention(attn_t, diff_lam, norm_g, lam_init):
    b, nt, _, tm = attn_t.shape
    assert nt % 2 == 1
    kern = functools.partial(_diff_kernel, lam_init=lam_init)
    ncomp = 2 * DIFF_HEADS
    return pl.pallas_call(
        kern,
        out_shape=jax.ShapeDtypeStruct((b, nt * tm, DIFF_WIDTH), BF16),
        grid=(b, nt),
        in_specs=[pl.BlockSpec((None, None, DIFF_WIDTH, tm), lambda bi, i: (bi, i, ROW_DIFF_Q // DIFF_WIDTH, 0)),
                  pl.BlockSpec((None, nt, DIFF_WIDTH, tm), lambda bi, i: (bi, 0, ROW_DIFF_K // DIFF_WIDTH, 0)),
                  pl.BlockSpec((None, nt, DIFF_WIDTH, tm), lambda bi, i: (bi, 0, ROW_DIFF_V // DIFF_WIDTH, 0)),
                  pl.BlockSpec((4, DIFF_QK), lambda bi, i: (0, 0)),
                  pl.BlockSpec((HEAD_DIM, 1), lambda bi, i: (0, 0))],
        out_specs=pl.BlockSpec((None, tm, DIFF_WIDTH), lambda bi, i: (bi, i, 0)),
        scratch_shapes=[pltpu.VMEM((ncomp, tm), F32), pltpu.VMEM((2, ncomp, tm), F32),
                        pltpu.VMEM((ncomp, DIFF_ACC_ROWS, tm), F32), pltpu.VMEM((2, ncomp, tm, tm), F32),
                        pltpu.VMEM((2, ncomp, tm, tm), BF16), pltpu.VMEM((DIFF_WIDTH, tm), F32)],
        compiler_params=_cparams(("arbitrary", "arbitrary")),
        name="differential_attention",
    )(attn_t, attn_t, attn_t, diff_lam, norm_g.reshape(HEAD_DIM, 1))


def _mod_rows(mod_ref, k, is_ctx):
    return jnp.where(is_ctx, mod_ref[0, k:k + 1, :], mod_ref[1, k:k + 1, :])


def _is_ctx_rows(tm, ctx_len):
    r0 = pl.program_id(1) * tm
    return (r0 + lax.broadcasted_iota(jnp.int32, (tm, 1), 0)) < ctx_len


def _deepnorm_ln(h, branch, gate, g, b, alpha):
    x = alpha * h + gate * branch
    mu = jnp.mean(x, axis=-1, keepdims=True)
    xc = x - mu
    var = jnp.mean(xc * xc, axis=-1, keepdims=True)
    return xc * lax.rsqrt(var + LN_EPS) * g + b


def _outproj_kernel(lru_ref, swa_ref, diff_ref, h_ref, mod_ref, w_ref, g_ref, b_ref, o_ref, *, alpha, ctx_len):
    tm = h_ref.shape[0]
    is_ctx = _is_ctx_rows(tm, ctx_len)
    m = jnp.dot(lru_ref[...], w_ref[0:LRU_WIDTH, :], preferred_element_type=F32)
    m = m + jnp.dot(swa_ref[...], w_ref[LRU_WIDTH:LRU_WIDTH + SWA_WIDTH, :], preferred_element_type=F32)
    m = m + jnp.dot(diff_ref[...], w_ref[LRU_WIDTH + SWA_WIDTH:MIX_WIDTH, :], preferred_element_type=F32)
    o_ref[...] = _deepnorm_ln(h_ref[...], m, _mod_rows(mod_ref, 2, is_ctx), g_ref[...], b_ref[...], alpha)


def _row_tile(t):
    for tm in range(640, 15, -16):
        if t % tm == 0:
            return tm
    raise ValueError(f"no row tile for T={t}")


def output_projection(lru, swa, diff, h, modtab, w_out, ln_g, ln_b, alpha, ctx_len):
    b, t, d = h.shape
    tm = _row_tile(t)
    kern = functools.partial(_outproj_kernel, alpha=alpha, ctx_len=ctx_len)
    row = lambda w: pl.BlockSpec((None, tm, w), lambda bi, i: (bi, i, 0))
    return pl.pallas_call(
        kern,
        out_shape=jax.ShapeDtypeStruct((b, t, d), F32),
        grid=(b, t // tm),
        in_specs=[row(LRU_WIDTH), row(SWA_WIDTH), row(DIFF_WIDTH), row(d),
                  pl.BlockSpec((None, 2, 6, d), lambda bi, i: (bi, 0, 0, 0)),
                  _const_spec((MIX_WIDTH, d)), _const_spec((1, d)), _const_spec((1, d))],
        out_specs=row(d),
        compiler_params=_cparams(("arbitrary", "arbitrary")),
        name="output_projection",
    )(lru, swa, diff, h, modtab, w_out, ln_g, ln_b)


FF_CHUNK = 256


def _swiglu_hidden(u, w1_ref, w3_ref, g_s, scale=None):
    ff = w1_ref.shape[-1]
    for c0 in range(0, ff, FF_CHUNK):
        h1 = jnp.dot(u, w1_ref[:, c0:c0 + FF_CHUNK], preferred_element_type=F32)
        h3 = jnp.dot(u, w3_ref[:, c0:c0 + FF_CHUNK], preferred_element_type=F32)
        g = (h1 * jax.nn.sigmoid(h1)) * h3
        if scale is not None:
            g = g * scale
        g_s[:, c0:c0 + FF_CHUNK] = g.astype(BF16)


def _ffn_kernel(h_ref, mod_ref, w1_ref, w3_ref, w2_ref, g_ref, b_ref, o_ref, g_s, *, alpha, ctx_len):
    tm = h_ref.shape[0]
    is_ctx = _is_ctx_rows(tm, ctx_len)
    h = h_ref[...]
    u = (h * (1.0 + _mod_rows(mod_ref, 4, is_ctx)) + _mod_rows(mod_ref, 3, is_ctx)).astype(BF16)
    _swiglu_hidden(u, w1_ref, w3_ref, g_s)
    f = jnp.dot(g_s[...], w2_ref[...], preferred_element_type=F32)
    o_ref[...] = _deepnorm_ln(h, f, _mod_rows(mod_ref, 5, is_ctx), g_ref[...], b_ref[...], alpha)


def dense_ffn(h, modtab, w1, w3, w2, ln_g, ln_b, alpha, ctx_len):
    b, t, d = h.shape
    ff = w1.shape[1]
    assert ff % FF_CHUNK == 0
    tm = _row_tile(t)
    kern = functools.partial(_ffn_kernel, alpha=alpha, ctx_len=ctx_len)
    row = pl.BlockSpec((None, tm, d), lambda bi, i: (bi, i, 0))
    return pl.pallas_call(
        kern,
        out_shape=jax.ShapeDtypeStruct((b, t, d), F32),
        grid=(b, t // tm),
        in_specs=[row, pl.BlockSpec((None, 2, 6, d), lambda bi, i: (bi, 0, 0, 0)),
                  _const_spec((d, ff)), _const_spec((d, ff)), _const_spec((ff, d)),
                  _const_spec((1, d)), _const_spec((1, d))],
        out_specs=row,
        scratch_shapes=[pltpu.VMEM((tm, ff), BF16)],
        compiler_params=_cparams(("arbitrary", "arbitrary")),
        name="dense_ffn",
    )(h, modtab, w1, w3, w2, ln_g, ln_b)


MOE_TILE = 512
ROUTE_E1, ROUTE_E2, ROUTE_R1, ROUTE_R2 = 0, 1, 2, 3
HI16 = 0xFFFF0000


def _pack_rows(x):
    half = x.shape[1] // 2
    lo = lax.bitcast_convert_type(x[:, :half].astype(BF16).astype(F32), jnp.uint32)
    hi = lax.bitcast_convert_type(x[:, half:].astype(BF16).astype(F32), jnp.uint32)
    return (lo >> 16) | (hi & jnp.uint32(HI16))


def _unpack_rows(w):
    lo = lax.bitcast_convert_type(w << 16, F32)
    hi = lax.bitcast_convert_type(w & jnp.uint32(HI16), F32)
    return lo, hi


def _route_kernel(h_ref, mod_ref, rw_ref, rb_ref, ri_ref, rg_ref, cnt_ref, cnt_s, *, ctx_len):
    tm = h_ref.shape[0]
    is_ctx = _is_ctx_rows(tm, ctx_len)

    @pl.when(jnp.logical_and(pl.program_id(0) == 0, pl.program_id(1) == 0))
    def _():
        cnt_s[...] = jnp.zeros(cnt_s.shape, F32)

    h = h_ref[...]
    u = (h * (1.0 + _mod_rows(mod_ref, 4, is_ctx)) + _mod_rows(mod_ref, 3, is_ctx)).astype(BF16)
    logits = jnp.dot(u, rw_ref[...], preferred_element_type=F32) + rb_ref[...]
    lane = lax.broadcasted_iota(jnp.int32, logits.shape, 1)
    m1 = jnp.max(logits, axis=-1, keepdims=True)
    i1 = jnp.min(jnp.where(logits == m1, lane, LANES), axis=-1, keepdims=True)
    rest = jnp.where(lane == i1, -jnp.inf, logits)
    m2 = jnp.max(rest, axis=-1, keepdims=True)
    i2 = jnp.min(jnp.where(rest == m2, lane, LANES), axis=-1, keepdims=True)
    e = jnp.exp(m2 - m1)
    g1 = 1.0 / (1.0 + e)
    g2 = e / (1.0 + e)
    sel1 = lane == i1
    sel2 = lane == i2
    picked = jnp.where(jnp.logical_or(sel1, sel2), 1.0, 0.0)
    before = lax.broadcasted_iota(jnp.int32, (tm, tm), 1) < lax.broadcasted_iota(jnp.int32, (tm, tm), 0)
    prefix = jnp.dot(jnp.where(before, 1.0, 0.0).astype(BF16), picked.astype(BF16), preferred_element_type=F32)
    rank = cnt_s[...] + prefix
    r1 = jnp.sum(jnp.where(sel1, rank, 0.0), axis=-1, keepdims=True).astype(jnp.int32)
    r2 = jnp.sum(jnp.where(sel2, rank, 0.0), axis=-1, keepdims=True).astype(jnp.int32)
    cnt_s[...] = cnt_s[...] + jnp.sum(picked, axis=0, keepdims=True)
    cnt_ref[...] = cnt_s[...]
    ri_ref[...] = jnp.where(lane == ROUTE_E1, i1, jnp.where(lane == ROUTE_E2, i2, jnp.where(
        lane == ROUTE_R1, r1, jnp.where(lane == ROUTE_R2, r2, 0))))
    rg_ref[...] = jnp.where(lane == 0, g1, jnp.where(lane == 1, g2, 0.0))


def moe_route(h, modtab, router_w, router_b, ctx_len):
    b, t, d = h.shape
    tm = _row_tile(t)
    kern = functools.partial(_route_kernel, ctx_len=ctx_len)
    row = lambda w: pl.BlockSpec((None, tm, w), lambda bi, i: (bi, i, 0))
    return pl.pallas_call(
        kern,
        out_shape=(jax.ShapeDtypeStruct((b, t, LANES), jnp.int32), jax.ShapeDtypeStruct((b, t, LANES), F32),
                   jax.ShapeDtypeStruct((1, LANES), F32)),
        grid=(b, t // tm),
        in_specs=[row(d), pl.BlockSpec((None, 2, 6, d), lambda bi, i: (bi, 0, 0, 0)),
                  _const_spec((d, LANES)), _const_spec((1, LANES))],
        out_specs=(row(LANES), row(LANES), pl.BlockSpec((1, LANES), lambda bi, i: (0, 0))),
        scratch_shapes=[pltpu.VMEM((1, LANES), F32)],
        compiler_params=_cparams(("arbitrary", "arbitrary")),
        name="moe_route",
    )(h, modtab, router_w, router_b)


def _row_dma_params(vmem=VMEM_LIMIT):
    return pltpu.CompilerParams(dimension_semantics=("arbitrary", "arbitrary"), vmem_limit_bytes=vmem,
                                disable_bounds_checks=True)


def _dispatch_kernel(s1_ref, s2_ref, h_ref, mod_ref, xs_in, xs_out, w_s, sem, *, ctx_len):
    del xs_in
    tm = h_ref.shape[0]
    is_ctx = _is_ctx_rows(tm, ctx_len)
    h = h_ref[...]
    w_s[...] = _pack_rows(h * (1.0 + _mod_rows(mod_ref, 4, is_ctx)) + _mod_rows(mod_ref, 3, is_ctx))

    def row_copy(r, slots, k):
        return pltpu.make_async_copy(w_s.at[pl.ds(r, 1), :], xs_out.at[pl.ds(slots[0, r], 1), :], sem.at[k])

    def issue(r, carry):
        row_copy(r, s1_ref, 0).start()
        row_copy(r, s2_ref, 1).start(priority=1)
        return carry

    lax.fori_loop(0, tm, issue, 0, unroll=8)
    for k in range(TOP_K):
        pltpu.make_async_copy(w_s, xs_out.at[pl.ds(0, tm), :], sem.at[k]).wait()


def moe_dispatch(h, modtab, slot1, slot2, n_slots, ctx_len):
    b, t, d = h.shape
    tm = _row_tile(t)
    nt = t // tm
    kern = functools.partial(_dispatch_kernel, ctx_len=ctx_len)
    slot_spec = pl.BlockSpec((None, 1, tm), lambda bi, i: (bi * nt + i, 0, 0), memory_space=pltpu.SMEM)
    return pl.pallas_call(
        kern,
        out_shape=jax.ShapeDtypeStruct((n_slots, d // 2), jnp.uint32),
        grid=(b, nt),
        in_specs=[slot_spec, slot_spec,
                  pl.BlockSpec((None, tm, d), lambda bi, i: (bi, i, 0)),
                  pl.BlockSpec((None, 2, 6, d), lambda bi, i: (bi, 0, 0, 0)),
                  pl.BlockSpec(memory_space=pl.ANY)],
        out_specs=pl.BlockSpec(memory_space=pl.ANY),
        scratch_shapes=[pltpu.VMEM((tm, d // 2), jnp.uint32), pltpu.SemaphoreType.DMA((TOP_K,))],
        input_output_aliases={4: 0},
        compiler_params=_row_dma_params(),
        name="moe_dispatch",
    )(slot1.reshape(b * nt, 1, tm), slot2.reshape(b * nt, 1, tm), h, modtab,
      jnp.zeros((n_slots, d // 2), jnp.uint32))


def _gffn_kernel(te_ref, na_ref, xs_ref, w1_ref, w3_ref, w2_ref, ys_ref, g_s):
    del te_ref
    i = pl.program_id(0)

    @pl.when(i < na_ref[0])
    def _():
        lo, hi = _unpack_rows(xs_ref[...])
        u = jnp.concatenate([lo.astype(BF16), hi.astype(BF16)], axis=1)
        _swiglu_hidden(u, w1_ref, w3_ref, g_s)
        ys_ref[...] = _pack_rows(jnp.dot(g_s[...], w2_ref[...], preferred_element_type=F32))

    @pl.when(i >= na_ref[0])
    def _():
        ys_ref[...] = jnp.zeros(ys_ref.shape, ys_ref.dtype)


def moe_grouped_ffn(xs, tile_expert, n_active, w1, w3, w2):
    n_slots, dh = xs.shape
    _, d, ff = w1.shape
    tm = MOE_TILE
    n_tiles = n_slots // tm
    return pl.pallas_call(
        _gffn_kernel,
        out_shape=jax.ShapeDtypeStruct((n_slots, dh), jnp.uint32),
        grid_spec=pltpu.PrefetchScalarGridSpec(
            num_scalar_prefetch=2,
            grid=(n_tiles,),
            in_specs=[pl.BlockSpec((tm, dh), lambda i, te, na: (i, 0)),
                      pl.BlockSpec((None, d, ff), lambda i, te, na: (te[i], 0, 0)),
                      pl.BlockSpec((None, d, ff), lambda i, te, na: (te[i], 0, 0)),
                      pl.BlockSpec((None, ff, d), lambda i, te, na: (te[i], 0, 0))],
            out_specs=pl.BlockSpec((tm, dh), lambda i, te, na: (i, 0)),
            scratch_shapes=[pltpu.VMEM((tm, ff), BF16)]),
        compiler_params=_cparams(("arbitrary",)),
        name="moe_grouped_ffn",
    )(tile_expert, n_active, xs, w1, w3, w2)


def _combine_kernel(s1_ref, s2_ref, ys_hbm, h_ref, mod_ref, rg_ref, g_ref, b_ref, o_ref, y_s, sem, *, alpha, ctx_len):
    tm = h_ref.shape[0]
    is_ctx = _is_ctx_rows(tm, ctx_len)

    def row_copy(r, slots, k):
        return pltpu.make_async_copy(ys_hbm.at[pl.ds(slots[0, r], 1), :], y_s.at[k, pl.ds(r, 1), :], sem.at[k])

    def issue(r, carry):
        row_copy(r, s1_ref, 0).start()
        row_copy(r, s2_ref, 1).start(priority=1)
        return carry

    lax.fori_loop(0, tm, issue, 0, unroll=8)
    for k in range(TOP_K):
        pltpu.make_async_copy(ys_hbm.at[pl.ds(0, tm), :], y_s.at[k], sem.at[k]).wait()

    lo1, hi1 = _unpack_rows(y_s[0])
    lo2, hi2 = _unpack_rows(y_s[1])
    g1 = rg_ref[:, 0:1]
    g2 = rg_ref[:, 1:2]
    f = jnp.concatenate([g1 * lo1 + g2 * lo2, g1 * hi1 + g2 * hi2], axis=1)
    o_ref[...] = _deepnorm_ln(h_ref[...], f, _mod_rows(mod_ref, 5, is_ctx), g_ref[...], b_ref[...], alpha)


def moe_combine(ys, slot1, slot2, gates, h, modtab, ln_g, ln_b, alpha, ctx_len):
    b, t, d = h.shape
    tm = _row_tile(t)
    nt = t // tm
    kern = functools.partial(_combine_kernel, alpha=alpha, ctx_len=ctx_len)
    slot_spec = pl.BlockSpec((None, 1, tm), lambda bi, i: (bi * nt + i, 0, 0), memory_space=pltpu.SMEM)
    row = lambda w: pl.BlockSpec((None, tm, w), lambda bi, i: (bi, i, 0))
    return pl.pallas_call(
        kern,
        out_shape=jax.ShapeDtypeStruct((b, t, d), F32),
        grid=(b, nt),
        in_specs=[slot_spec, slot_spec, pl.BlockSpec(memory_space=pl.ANY), row(d),
                  pl.BlockSpec((None, 2, 6, d), lambda bi, i: (bi, 0, 0, 0)), row(LANES),
                  _const_spec((1, d)), _const_spec((1, d))],
        out_specs=row(d),
        scratch_shapes=[pltpu.VMEM((TOP_K, tm, d // 2), jnp.uint32), pltpu.SemaphoreType.DMA((TOP_K,))],
        compiler_params=_row_dma_params(),
        name="moe_combine",
    )(slot1.reshape(b * nt, 1, tm), slot2.reshape(b * nt, 1, tm), ys, h, modtab, gates, ln_g, ln_b)


def moe_ffn(h, modtab, router_w, router_b, w1, w3, w2, ln_g, ln_b, alpha, ctx_len):
    b, t, d = h.shape
    n_experts = w1.shape[0]
    n_tiles = -(-(TOP_K * b * t) // MOE_TILE) + n_experts
    ri, gates, cnt = moe_route(h, modtab, router_w, router_b, ctx_len)
    counts = cnt[0, :n_experts].astype(jnp.int32)
    tiles_e = (counts + MOE_TILE - 1) // MOE_TILE
    ends = jnp.cumsum(tiles_e)
    offs = (ends - tiles_e) * MOE_TILE
    n_active = ends[-1:]
    tile_ids = jnp.arange(n_tiles, dtype=jnp.int32)
    tile_expert = jnp.searchsorted(ends, jnp.minimum(tile_ids, n_active - 1), side="right").astype(jnp.int32)
    slot1 = offs[ri[..., ROUTE_E1]] + ri[..., ROUTE_R1]
    slot2 = offs[ri[..., ROUTE_E2]] + ri[..., ROUTE_R2]
    xs = moe_dispatch(h, modtab, slot1, slot2, n_tiles * MOE_TILE, ctx_len)
    ys = moe_grouped_ffn(xs, tile_expert, n_active.astype(jnp.int32), w1, w3, w2)
    return moe_combine(ys, slot1, slot2, gates, h, modtab, ln_g, ln_b, alpha, ctx_len)


def _rope_table(rows, ctx_len):
    def table(rot_dim):
        n_freq = rot_dim // 4
        inv = ROPE_BASE ** (-jnp.arange(n_freq, dtype=F32) / n_freq)
        row = jnp.repeat(jnp.arange(rows, dtype=F32), GRID_W)
        col = jnp.tile(jnp.arange(GRID_W, dtype=F32), rows)
        ang = jnp.concatenate([row[:, None] * inv, col[:, None] * inv], -1)
        cos, sin = jnp.cos(ang), jnp.sin(ang)
        reps = LANES // rot_dim
        cos_t = jnp.tile(jnp.concatenate([cos, cos], -1), (1, reps))
        sin_t = jnp.tile(jnp.concatenate([-sin, sin], -1), (1, reps))
        ident = (jnp.ones((ctx_len, LANES), F32), jnp.zeros((ctx_len, LANES), F32))
        return jnp.concatenate([ident[0], cos_t], 0), jnp.concatenate([ident[1], sin_t], 0)
    c64, s64 = table(HEAD_DIM)
    c32, s32 = table(DIFF_QK)
    return jnp.concatenate([c64, s64, c32, s32], axis=-1)


def _gate_dense(gate_w):
    cols = []
    for d in range(2):
        for g in range(2):
            cols.append(jax.scipy.linalg.block_diag(*[gate_w[d, g, k] for k in range(LRU_BLOCKS)]))
    return jnp.concatenate(cols, axis=1)


def kernel(x, c, ctx, c_ctx, ada_w, ada_b, w_in, w_out, lru_conv_w, lru_conv_b, lru_gate_w, lru_gate_b, lru_lam,
           swa_sink, diff_lam, diff_norm_g, ln_g, ln_b, ffn_w1, ffn_w3, ffn_w2, moe_router_w, moe_router_b,
           moe_w1, moe_w3, moe_w2):
    b, s, d = x.shape
    ctx_len = ctx.shape[1]
    depth = ada_w.shape[0]
    n_experts = moe_router_w.shape[-1]
    alpha = (2.0 * depth) ** 0.25
    assert s % GRID_W == 0 and s % TOK_TILE == 0 and ctx_len == TOK_TILE

    h = jnp.concatenate([ctx, x], axis=1)
    ropetab = _rope_table(s // GRID_W, ctx_len)

    rows = 8 * ((b + 1 + 7) // 8)
    cvec = jnp.zeros((rows, d), F32).at[:b].set(c).at[b].set(c_ctx)
    mods = ada_modulation(cvec, ada_w, ada_b).reshape(depth, rows, 6, d)
    modtab = jnp.stack([jnp.broadcast_to(mods[:, b:b + 1], (depth, b, 6, d)), mods[:, :b]], axis=2)

    sink_rows = jnp.repeat(swa_sink.reshape(depth, SWA_KV_HEADS, 1, SWA_GROUP), ATTN_BLOCK, axis=-1)

    for layer in range(depth):
        lam_init = 0.8 - 0.6 * math.exp(-0.3 * layer)
        mt = modtab[layer]
        lru_xg, attn_t = input_projection(h, mt, w_in[layer].astype(BF16), ropetab)
        lru = rglru_mix(lru_xg, lru_conv_w[layer], lru_conv_b[layer].reshape(1, LRU_WIDTH),
                        _gate_dense(lru_gate_w[layer]).astype(BF16), lru_gate_b[layer].reshape(1, 4 * LRU_WIDTH),
                        lru_lam[layer], ctx_len)
        swa = windowed_attention(attn_t, sink_rows[layer])
        dif = differential_attention(attn_t, diff_lam[layer], diff_norm_g[layer], lam_init)
        h = output_projection(lru, swa, dif, h, mt, w_out[layer].astype(BF16), ln_g[layer, 0:1], ln_b[layer, 0:1],
                              alpha, ctx_len)
        j = layer // 2
        if layer % 2 == 0:
            h = dense_ffn(h, mt, ffn_w1[j].astype(BF16), ffn_w3[j].astype(BF16), ffn_w2[j].astype(BF16),
                          ln_g[layer, 1:2], ln_b[layer, 1:2], alpha, ctx_len)
        else:
            rw = jnp.zeros((d, LANES), F32).at[:, :n_experts].set(moe_router_w[j]).astype(BF16)
            rb = jnp.full((1, LANES), NEG_INF, F32).at[0, :n_experts].set(moe_router_b[j])
            h = moe_ffn(h, mt, rw, rb, moe_w1[j].astype(BF16), moe_w3[j].astype(BF16), moe_w2[j].astype(BF16),
                        ln_g[layer, 1:2], ln_b[layer, 1:2], alpha, ctx_len)
    return h[:, ctx_len:, :]
```

```python
import functools
import math

import jax
import jax.numpy as jnp
from jax import lax
from jax.experimental import pallas as pl
from jax.experimental.pallas import tpu as pltpu

F32 = jnp.float32
BF16 = jnp.bfloat16

GRID_W = 64
HEAD_DIM = 64
LRU_WIDTH = 256
LRU_BLOCKS = 4
LRU_BLOCK = LRU_WIDTH // LRU_BLOCKS
CONV_W = 4
LRU_C = 8.0
SWA_Q_HEADS = 8
SWA_KV_HEADS = 2
SWA_GROUP = SWA_Q_HEADS // SWA_KV_HEADS
SWA_WIDTH = SWA_Q_HEADS * HEAD_DIM
ATTN_BLOCK = 128
DIFF_HEADS = 4
DIFF_QK = HEAD_DIM // 2
DIFF_WIDTH = DIFF_HEADS * HEAD_DIM
MIX_WIDTH = LRU_WIDTH + SWA_WIDTH + DIFF_WIDTH
TOP_K = 2
ROPE_BASE = 10000.0
LN_EPS = 1e-5
NEG_INF = -1e30
COL_LRU_X, COL_LRU_G, COL_SWA_Q, COL_SWA_K, COL_SWA_V, COL_DIFF_Q, COL_DIFF_K, COL_DIFF_V, IN_COLS = (
    0, 256, 512, 1024, 1152, 1280, 1536, 1792, 2048)
ATTN_COLS = IN_COLS - COL_SWA_Q
ROW_SWA_Q, ROW_SWA_KV, ROW_DIFF_Q, ROW_DIFF_K, ROW_DIFF_V = 0, 512, 768, 1024, 1280

LANES = 128
TOK_TILE = 256
VMEM_LIMIT = 56 * 1024 * 1024


def _cparams(sem, vmem=VMEM_LIMIT):
    return pltpu.CompilerParams(dimension_semantics=sem, vmem_limit_bytes=vmem)


def _const_spec(shape):
    nd = len(shape)
    return pl.BlockSpec(shape, lambda *_: (0,) * nd, pipeline_mode=pl.Buffered(1))


def _ada_kernel(c_ref, w_ref, b_ref, o_ref):
    c = c_ref[...]
    s = (c * jax.nn.sigmoid(c)).astype(BF16)
    o_ref[...] = jnp.dot(s, w_ref[...].astype(BF16), preferred_element_type=F32) + b_ref[...]


def ada_modulation(cvec, ada_w, ada_b):
    depth, d, n = ada_w.shape
    r = cvec.shape[0]
    tn = 1536
    assert n % tn == 0
    return pl.pallas_call(
        _ada_kernel,
        out_shape=jax.ShapeDtypeStruct((depth, r, n), F32),
        grid=(depth, n // tn),
        in_specs=[pl.BlockSpec((r, d), lambda l, j: (0, 0)),
                  pl.BlockSpec((None, d, tn), lambda l, j: (l, 0, j)),
                  pl.BlockSpec((None, 1, tn), lambda l, j: (l, 0, j))],
        out_specs=pl.BlockSpec((None, r, tn), lambda l, j: (l, 0, j)),
        compiler_params=_cparams(("arbitrary", "arbitrary")),
        name="ada_modulation",
    )(cvec, ada_w, ada_b.reshape(depth, 1, n))


def _rope_group(x, cos, sin_signed, half, lane_mod):
    swapped = jnp.where(lane_mod < half, pltpu.roll(x, LANES - half, axis=1), pltpu.roll(x, half, axis=1))
    return x * cos + swapped * sin_signed


def _inproj_kernel(h_ref, mod_ref, w_ref, rope_ref, lru_ref, attn_ref, *, swa_scale, diff_scale):
    h = h_ref[...]
    u = (h * (1.0 + mod_ref[1:2, :]) + mod_ref[0:1, :]).astype(BF16)
    acc = jnp.dot(u, w_ref[...], preferred_element_type=F32)
    lru_ref[...] = acc[:, :COL_SWA_Q]
    lane = lax.broadcasted_iota(jnp.int32, (1, LANES), 1)
    cos64, sin64 = rope_ref[:, 0:LANES], rope_ref[:, LANES:2 * LANES]
    cos32, sin32 = rope_ref[:, 2 * LANES:3 * LANES], rope_ref[:, 3 * LANES:4 * LANES]
    mod64, mod32 = lane % HEAD_DIM, lane % DIFF_QK
    for c0 in range(COL_SWA_Q, IN_COLS, LANES):
        x = acc[:, c0:c0 + LANES]
        if c0 < COL_SWA_V:
            x = _rope_group(x, cos64, sin64, HEAD_DIM // 2, mod64)
            if c0 < COL_SWA_K:
                x = x * swa_scale
        elif COL_DIFF_Q <= c0 < COL_DIFF_V:
            x = _rope_group(x, cos32, sin32, DIFF_QK // 2, mod32)
            if c0 < COL_DIFF_K:
                x = x * diff_scale
        r0 = c0 - COL_SWA_Q
        attn_ref[r0:r0 + LANES, :] = x.T.astype(BF16)


def input_projection(h, modtab, w_in, ropetab):
    b, t, d = h.shape
    tm = TOK_TILE
    nt = t // tm
    kern = functools.partial(_inproj_kernel, swa_scale=HEAD_DIM ** -0.5, diff_scale=DIFF_QK ** -0.5 * math.log2(math.e))
    return pl.pallas_call(
        kern,
        out_shape=(jax.ShapeDtypeStruct((b, t, COL_SWA_Q), F32),
                   jax.ShapeDtypeStruct((b, nt, ATTN_COLS, tm), BF16)),
        grid=(b, nt),
        in_specs=[pl.BlockSpec((None, tm, d), lambda bi, i: (bi, i, 0)),
                  pl.BlockSpec((None, None, 6, d), lambda bi, i: (bi, jnp.minimum(i, 1), 0, 0)),
                  _const_spec((d, IN_COLS)),
                  pl.BlockSpec((tm, 4 * LANES), lambda bi, i: (i, 0))],
        out_specs=(pl.BlockSpec((None, tm, COL_SWA_Q), lambda bi, i: (bi, i, 0)),
                   pl.BlockSpec((None, None, ATTN_COLS, tm), lambda bi, i: (bi, i, 0, 0))),
        compiler_params=_cparams(("arbitrary", "arbitrary")),
        name="input_projection",
    )(h, modtab, w_in, ropetab)


def _neg_expm1(z):
    series = z * (1.0 + z * (1 / 2 + z * (1 / 6 + z * (1 / 24 + z * (1 / 120 + z * (1 / 720 + z * (1 / 5040)))))))
    return jnp.where(z > -0.125, -series, 1.0 - jnp.exp(z))


def _lru_kernel(xg_ref, cw_ref, cb_ref, wg_ref, gb_ref, lam_ref, o_ref, a_s, b_s, h_s, *, chunk, ctx_chunks):
    t, c = o_ref.shape
    nchunk = t // chunk
    row = lax.broadcasted_iota(jnp.int32, (chunk, 1), 0)
    sp = jax.nn.softplus(-lam_ref[...])
    cw = cw_ref[...]
    cb = cb_ref[...]

    def coeffs(j, carry):
        r0 = pl.multiple_of(j * chunk, chunk)
        x = xg_ref[pl.ds(r0, chunk), 0:c]
        prev8 = xg_ref[pl.ds(pl.multiple_of(jnp.maximum(r0 - 8, 0), 8), 8), 0:c]
        next8 = xg_ref[pl.ds(pl.multiple_of(jnp.minimum(r0 + chunk, t - 8), 8), 8), 0:c]
        seg_start = jnp.logical_or(j == 0, j == ctx_chunks)
        seg_end = jnp.logical_or(j == ctx_chunks - 1, j == nchunk - 1)
        pm1 = jnp.where(seg_start, 0.0, prev8[7:8, :])
        np0 = jnp.where(seg_end, 0.0, next8[0:1, :])
        np1 = jnp.where(seg_end, 0.0, next8[1:2, :])
        xm1 = jnp.where(row == 0, pm1, pltpu.roll(x, 1, axis=0))
        xp1 = jnp.where(row == chunk - 1, np0, pltpu.roll(x, chunk - 1, axis=0))
        xp2 = jnp.where(row == chunk - 1, np1, jnp.where(row == chunk - 2, np0, pltpu.roll(x, chunk - 2, axis=0)))
        u = cw[0:1, :] * xm1 + cw[1:2, :] * x + cw[2:3, :] * xp1 + cw[3:4, :] * xp2 + cb
        g = jnp.dot(u.astype(BF16), wg_ref[...], preferred_element_type=F32) + gb_ref[...]
        for d in range(2):
            r = jax.nn.sigmoid(g[:, (2 * d) * c:(2 * d + 1) * c])
            i = jax.nn.sigmoid(g[:, (2 * d + 1) * c:(2 * d + 2) * c])
            log_a = (-LRU_C) * r * sp[d:d + 1, :]
            a_s[d, pl.ds(r0, chunk), :] = jnp.exp(log_a)
            b_s[d, pl.ds(r0, chunk), :] = jnp.sqrt(_neg_expm1(2.0 * log_a)) * (i * u)
        return carry

    lax.fori_loop(0, nchunk, coeffs, 0)

    def scan_segment(lo, n8, hf, hb):
        def body(k, hh):
            hf, hb = hh
            f0 = pl.multiple_of(lo + k * 8, 8)
            r0 = pl.multiple_of(lo + (n8 - 1 - k) * 8, 8)
            af, bf = a_s[0, pl.ds(f0, 8), :], b_s[0, pl.ds(f0, 8), :]
            ab, bb = a_s[1, pl.ds(r0, 8), :], b_s[1, pl.ds(r0, 8), :]
            fw, bw = [], []
            for s in range(8):
                hf = af[s:s + 1, :] * hf + bf[s:s + 1, :]
                fw.append(hf)
                hb = ab[7 - s:8 - s, :] * hb + bb[7 - s:8 - s, :]
                bw.append(hb)
            h_s[0, pl.ds(f0, 8), :] = jnp.concatenate(fw, axis=0)
            h_s[1, pl.ds(r0, 8), :] = jnp.concatenate(bw[::-1], axis=0)
            return hf, hb
        return lax.fori_loop(0, n8, body, (hf, hb))

    zero = jnp.zeros((1, c), F32)
    ctx_rows = ctx_chunks * chunk
    hf, hb = scan_segment(0, ctx_rows // 8, zero, zero)
    scan_segment(ctx_rows, (t - ctx_rows) // 8, hf, hb)

    def finish(j, carry):
        r0 = pl.multiple_of(j * chunk, chunk)
        y = h_s[0, pl.ds(r0, chunk), :] + h_s[1, pl.ds(r0, chunk), :]
        gate = xg_ref[pl.ds(r0, chunk), c:2 * c]
        o_ref[pl.ds(r0, chunk), :] = (y * jax.nn.gelu(gate)).astype(o_ref.dtype)
        return carry

    lax.fori_loop(0, nchunk, finish, 0)


def rglru_mix(lru_xg, conv_w, conv_b, gate_dense, gate_b, lam, ctx_len):
    b, t, c2 = lru_xg.shape
    c = c2 // 2
    chunk = TOK_TILE
    assert t % chunk == 0 and ctx_len % chunk == 0
    kern = functools.partial(_lru_kernel, chunk=chunk, ctx_chunks=ctx_len // chunk)
    return pl.pallas_call(
        kern,
        out_shape=jax.ShapeDtypeStruct((b, t, c), BF16),
        grid=(b,),
        in_specs=[pl.BlockSpec((None, t, c2), lambda bi: (bi, 0, 0), pipeline_mode=pl.Buffered(1)),
                  _const_spec((CONV_W, c)), _const_spec((1, c)), _const_spec((c, 4 * c)), _const_spec((1, 4 * c)),
                  _const_spec((2, c))],
        out_specs=pl.BlockSpec((None, t, c), lambda bi: (bi, 0, 0)),
        scratch_shapes=[pltpu.VMEM((2, t, c), F32), pltpu.VMEM((2, t, c), F32), pltpu.VMEM((2, t, c), F32)],
        compiler_params=_cparams(("arbitrary",)),
        name="rglru_mix",
    )(lru_xg, conv_w, conv_b, gate_dense, gate_b, lam)


def _dot_t0(a, b):
    return lax.dot_general(a, b, (((0,), (0,)), ((), ())), preferred_element_type=F32)


SWA_MAX_KEY_BLOCKS = 5


def _swa_kernel(q_ref, own_ref, prev_ref, next_ref, ctx_ref, sink_ref, o_ref, s_s, p_s, l_s, ot_s):
    i = pl.program_id(1)
    nt = pl.num_programs(1)
    blk = ATTN_BLOCK
    nq = SWA_GROUP * blk
    kp = lax.broadcasted_iota(jnp.int32, (blk, nq), 0)
    qp = lax.broadcasted_iota(jnp.int32, (blk, nq), 1) % blk

    def kv(ref, hk, half):
        k = ref[hk * HEAD_DIM:(hk + 1) * HEAD_DIM, half * blk:(half + 1) * blk]
        v = ref[SWA_KV_HEADS * HEAD_DIM + hk * HEAD_DIM:SWA_KV_HEADS * HEAD_DIM + (hk + 1) * HEAD_DIM,
                half * blk:(half + 1) * blk]
        return k, v

    def run(jobs):
        for a, (hk, half, keys, _, _) in enumerate(jobs):
            qt = jnp.concatenate(
                [q_ref[(hk * SWA_GROUP + g) * HEAD_DIM:(hk * SWA_GROUP + g + 1) * HEAD_DIM,
                       half * blk:(half + 1) * blk] for g in range(SWA_GROUP)], axis=1)
            for j, kt in enumerate(keys):
                s_s[a, j * blk:(j + 1) * blk, :] = _dot_t0(kt, qt)
        for a, (hk, _, keys, _, masks) in enumerate(jobs):
            sink = sink_ref[hk]
            s = []
            for j, msk in enumerate(masks):
                sj = s_s[a, j * blk:(j + 1) * blk, :]
                s.append(sj if msk is None else jnp.where(msk, sj, NEG_INF))
            m = sink
            for sj in s:
                m = jnp.maximum(m, jnp.max(sj, axis=0, keepdims=True))
            l = jnp.exp(sink - m)
            for j, sj in enumerate(s):
                pj = jnp.exp(sj - m)
                l = l + jnp.sum(pj, axis=0, keepdims=True)
                p_s[a, j * blk:(j + 1) * blk, :] = pj.astype(BF16)
            l_s[a:a + 1, :] = l
        for a, (hk, half, keys, vals, _) in enumerate(jobs):
            nk = len(keys) * blk
            vt = jnp.concatenate(vals, axis=1)
            o = jnp.dot(vt, p_s[a, 0:nk, :], preferred_element_type=F32) / l_s[a:a + 1, :]
            for g in range(SWA_GROUP):
                f0 = (hk * SWA_GROUP + g) * HEAD_DIM
                ot_s[f0:f0 + HEAD_DIM, half * blk:(half + 1) * blk] = o[:, g * blk:(g + 1) * blk]

    @pl.when(i == 0)
    def _():
        jobs = []
        for hk in range(SWA_KV_HEADS):
            k0, v0 = kv(ctx_ref, hk, 0)
            k1, v1 = kv(ctx_ref, hk, 1)
            for half in range(2):
                jobs.append((hk, half, [k0, k1], [v0, v1], [None, None]))
        run(jobs)

    @pl.when(i > 0)
    def _():
        has_prev = i > 1
        has_next = i < nt - 1
        m_prev = kp >= qp
        m_next = kp <= qp
        jobs = []
        for hk in range(SWA_KV_HEADS):
            c0k, c0v = kv(ctx_ref, hk, 0)
            c1k, c1v = kv(ctx_ref, hk, 1)
            o0k, o0v = kv(own_ref, hk, 0)
            o1k, o1v = kv(own_ref, hk, 1)
            pk, pv = kv(prev_ref, hk, 1)
            nk, nv = kv(next_ref, hk, 0)
            jobs.append((hk, 0, [c0k, c1k, pk, o0k, o1k], [c0v, c1v, pv, o0v, o1v],
                         [None, None, jnp.logical_and(m_prev, has_prev), None, m_next]))
            jobs.append((hk, 1, [c0k, c1k, o0k, o1k, nk], [c0v, c1v, o0v, o1v, nv],
                         [None, None, m_prev, None, jnp.logical_and(m_next, has_next)]))
        run(jobs)

    o_ref[...] = ot_s[...].T.astype(o_ref.dtype)


def windowed_attention(attn_t, sink_rows):
    b, nt, _, tm = attn_t.shape
    kvrows = 2 * SWA_KV_HEADS * HEAD_DIM
    kvblk = ROW_SWA_KV // kvrows
    assert ROW_SWA_KV % kvrows == 0 and tm == 2 * ATTN_BLOCK
    return pl.pallas_call(
        _swa_kernel,
        out_shape=jax.ShapeDtypeStruct((b, nt * tm, SWA_WIDTH), BF16),
        grid=(b, nt),
        in_specs=[pl.BlockSpec((None, None, SWA_WIDTH, tm), lambda bi, i: (bi, i, 0, 0)),
                  pl.BlockSpec((None, None, kvrows, tm), lambda bi, i: (bi, i, kvblk, 0)),
                  pl.BlockSpec((None, None, kvrows, tm), lambda bi, i: (bi, jnp.maximum(i - 1, 1), kvblk, 0)),
                  pl.BlockSpec((None, None, kvrows, tm), lambda bi, i: (bi, jnp.minimum(i + 1, nt - 1), kvblk, 0)),
                  pl.BlockSpec((None, None, kvrows, tm), lambda bi, i: (bi, 0, kvblk, 0)),
                  pl.BlockSpec((SWA_KV_HEADS, 1, SWA_GROUP * ATTN_BLOCK), lambda bi, i: (0, 0, 0))],
        out_specs=pl.BlockSpec((None, tm, SWA_WIDTH), lambda bi, i: (bi, i, 0)),
        scratch_shapes=[pltpu.VMEM((2 * SWA_KV_HEADS, SWA_MAX_KEY_BLOCKS * ATTN_BLOCK, SWA_GROUP * ATTN_BLOCK), F32),
                        pltpu.VMEM((2 * SWA_KV_HEADS, SWA_MAX_KEY_BLOCKS * ATTN_BLOCK, SWA_GROUP * ATTN_BLOCK), BF16),
                        pltpu.VMEM((2 * SWA_KV_HEADS, SWA_GROUP * ATTN_BLOCK), F32),
                        pltpu.VMEM((SWA_WIDTH, tm), F32)],
        compiler_params=_cparams(("arbitrary", "arbitrary")),
        name="windowed_attention",
    )(attn_t, attn_t, attn_t, attn_t, attn_t, sink_rows)


DIFF_ACC_ROWS = HEAD_DIM + 16
DIFF_Q_TILES = 2


DIFF_MIN_SUM_LOG2 = -100.0


def _diff_kernel(q_ref, k_ref, v_ref, lam_ref, g_ref, o_ref, m_s, a_s, acc_s, s_s, p_s, ot_s, *, lam_init):
    nkv = k_ref.shape[0]
    last = nkv - 1
    ncomp = 2 * DIFF_HEADS
    ones = jnp.ones((DIFF_ACC_ROWS - HEAD_DIM, k_ref.shape[-1]), BF16)

    def q_comp(c):
        return jnp.concatenate([q_ref[t, c * DIFF_QK:(c + 1) * DIFF_QK, :] for t in range(q_ref.shape[0])], axis=1)

    def add_values(j, slot, rescale):
        vt = v_ref[j]
        for c in range(ncomp):
            h = c // 2
            v1 = jnp.concatenate([vt[h * HEAD_DIM:(h + 1) * HEAD_DIM, :], ones], axis=0)
            old = a_s[slot, c:c + 1, :] * acc_s[c] if rescale else acc_s[c]
            acc_s[c] = old + jnp.dot(v1, p_s[slot, c], preferred_element_type=F32)

    def key_norms(j, mx):
        kf = k_ref[j].astype(F32)
        sq = kf * kf
        n2 = jnp.concatenate([jnp.sum(sq[c * DIFF_QK:(c + 1) * DIFF_QK, :], axis=0, keepdims=True)
                              for c in range(ncomp)], axis=0)
        return jnp.maximum(mx, n2)

    kmax2 = jnp.max(lax.fori_loop(0, nkv, key_norms, jnp.zeros((ncomp, k_ref.shape[-1]), F32)),
                    axis=1, keepdims=True)
    for c in range(ncomp):
        qf = q_comp(c).astype(F32)
        m_s[c:c + 1, :] = jnp.sqrt(jnp.sum(qf * qf, axis=0, keepdims=True) * kmax2[c:c + 1, :])
    acc_s[...] = jnp.zeros(acc_s.shape, F32)

    def bounded_probs(j, slot):
        kt = k_ref[j]
        for c in range(ncomp):
            s = _dot_t0(kt[c * DIFF_QK:(c + 1) * DIFF_QK, :], q_comp(c))
            p_s[slot, c] = jnp.exp2(s - m_s[c:c + 1, :]).astype(BF16)

    bounded_probs(0, 0)

    def bounded_pair(jj, carry):
        j = 2 * jj
        bounded_probs(j + 1, 1)
        add_values(j, 0, False)
        bounded_probs(j + 2, 0)
        add_values(j + 1, 1, False)
        return carry

    lax.fori_loop(0, last // 2, bounded_pair, 0)
    add_values(last, 0, False)

    min_sum = acc_s[0, HEAD_DIM:HEAD_DIM + 1, :]
    for c in range(1, ncomp):
        min_sum = jnp.minimum(min_sum, acc_s[c, HEAD_DIM:HEAD_DIM + 1, :])
    lost_range = jnp.min(min_sum) < 2.0 ** DIFF_MIN_SUM_LOG2

    def scores(j, slot):
        kt = k_ref[j]
        for c in range(ncomp):
            s_s[slot, c] = _dot_t0(kt[c * DIFF_QK:(c + 1) * DIFF_QK, :], q_comp(c))

    def softmax(slot):
        for c in range(ncomp):
            s = s_s[slot, c]
            m_old = m_s[c:c + 1, :]
            m_new = jnp.maximum(m_old, jnp.max(s, axis=0, keepdims=True))
            m_s[c:c + 1, :] = m_new
            a_s[slot, c:c + 1, :] = jnp.exp2(m_old - m_new)
            p_s[slot, c] = jnp.exp2(s - m_new).astype(BF16)

    @pl.when(lost_range)
    def _():
        m_s[...] = jnp.full(m_s.shape, NEG_INF, F32)
        acc_s[...] = jnp.zeros(acc_s.shape, F32)
        scores(0, 0)
        scores(min(1, last), 1)
        softmax(0)

        def pair(jj, carry):
            j = 2 * jj
            scores(j + 2, 0)
            softmax(1)
            add_values(j, 0, True)
            scores(jnp.minimum(j + 3, last), 1)
            softmax(0)
            add_values(j + 1, 1, True)
            return carry

        lax.fori_loop(0, last // 2, pair, 0)
        add_values(last, 0, True)

    lam_v = lam_ref[...]
    lam = (jnp.exp(jnp.sum(lam_v[0:1, :] * lam_v[1:2, :], axis=1, keepdims=True))
           - jnp.exp(jnp.sum(lam_v[2:3, :] * lam_v[3:4, :], axis=1, keepdims=True)) + lam_init)

    def head_out(c):
        return acc_s[c, 0:HEAD_DIM, :] / acc_s[c, HEAD_DIM:HEAD_DIM + 1, :]

    for h in range(DIFF_HEADS):
        o = head_out(2 * h) - lam * head_out(2 * h + 1)
        o = o * lax.rsqrt(jnp.mean(jnp.square(o), axis=0, keepdims=True) + LN_EPS)
        ot_s[h * HEAD_DIM:(h + 1) * HEAD_DIM, :] = (o * (1.0 - lam_init)) * g_ref[...]
    o_ref[...] = ot_s[...].T.astype(o_ref.dtype)


def _diff_call(q_tiles, q_feat_blk, kv, nk, nq, diff_lam, norm_g, lam_init):
    b, nqt, _, tm = q_tiles.shape
    assert nk % 2 == 1 and nqt % nq == 0
    tq = nq * tm
    ncomp = 2 * DIFF_HEADS
    kern = functools.partial(_diff_kernel, lam_init=lam_init)
    return pl.pallas_call(
        kern,
        out_shape=jax.ShapeDtypeStruct((b, nqt * tm, DIFF_WIDTH), BF16),
        grid=(b, nqt // nq),
        in_specs=[pl.BlockSpec((None, nq, DIFF_WIDTH, tm), lambda bi, i: (bi, i, q_feat_blk, 0)),
                  pl.BlockSpec((None, nk, DIFF_WIDTH, tm), lambda bi, i: (bi, 0, ROW_DIFF_K // DIFF_WIDTH, 0)),
                  pl.BlockSpec((None, nk, DIFF_WIDTH, tm), lambda bi, i: (bi, 0, ROW_DIFF_V // DIFF_WIDTH, 0)),
                  pl.BlockSpec((4, DIFF_QK), lambda bi, i: (0, 0)),
                  pl.BlockSpec((HEAD_DIM, 1), lambda bi, i: (0, 0))],
        out_specs=pl.BlockSpec((None, tq, DIFF_WIDTH), lambda bi, i: (bi, i, 0)),
        scratch_shapes=[pltpu.VMEM((ncomp, tq), F32), pltpu.VMEM((2, ncomp, tq), F32),
                        pltpu.VMEM((ncomp, DIFF_ACC_ROWS, tq), F32), pltpu.VMEM((2, ncomp, tm, tq), F32),
                        pltpu.VMEM((2, ncomp, tm, tq), BF16), pltpu.VMEM((DIFF_WIDTH, tq), F32)],
        compiler_params=_cparams(("arbitrary", "arbitrary")),
        name="differential_attention",
    )(q_tiles, kv, kv, diff_lam, norm_g.reshape(HEAD_DIM, 1))


def differential_attention(attn_t, diff_lam, norm_g, lam_init, need_ctx):
    b, nt, _, tm = attn_t.shape
    q_lat = attn_t[:, 1:, ROW_DIFF_Q:ROW_DIFF_Q + DIFF_WIDTH, :]
    lat = _diff_call(q_lat, 0, attn_t, nt, DIFF_Q_TILES, diff_lam, norm_g, lam_init)
    if need_ctx:
        ctx = _diff_call(attn_t[:, 0:1], ROW_DIFF_Q // DIFF_WIDTH, attn_t, 1, 1, diff_lam, norm_g, lam_init)
    else:
        ctx = jnp.zeros((b, tm, DIFF_WIDTH), BF16)
    return jnp.concatenate([ctx, lat], axis=1)


def _mod_rows(mod_ref, k, is_ctx):
    return jnp.where(is_ctx, mod_ref[0, k:k + 1, :], mod_ref[1, k:k + 1, :])


def _is_ctx_rows(tm, ctx_len):
    r0 = pl.program_id(1) * tm
    return (r0 + lax.broadcasted_iota(jnp.int32, (tm, 1), 0)) < ctx_len


def _deepnorm_ln(h, branch, gate, g, b, alpha):
    x = alpha * h + gate * branch
    mu = jnp.mean(x, axis=-1, keepdims=True)
    xc = x - mu
    var = jnp.mean(xc * xc, axis=-1, keepdims=True)
    return xc * lax.rsqrt(var + LN_EPS) * g + b


def _outproj_kernel(lru_ref, swa_ref, diff_ref, h_ref, mod_ref, w_ref, g_ref, b_ref, o_ref, *, alpha, ctx_len):
    tm = h_ref.shape[0]
    is_ctx = _is_ctx_rows(tm, ctx_len)
    m = jnp.dot(lru_ref[...], w_ref[0:LRU_WIDTH, :], preferred_element_type=F32)
    m = m + jnp.dot(swa_ref[...], w_ref[LRU_WIDTH:LRU_WIDTH + SWA_WIDTH, :], preferred_element_type=F32)
    m = m + jnp.dot(diff_ref[...], w_ref[LRU_WIDTH + SWA_WIDTH:MIX_WIDTH, :], preferred_element_type=F32)
    o_ref[...] = _deepnorm_ln(h_ref[...], m, _mod_rows(mod_ref, 2, is_ctx), g_ref[...], b_ref[...], alpha)


def _row_tile(t):
    for tm in range(640, 15, -16):
        if t % tm == 0:
            return tm
    raise ValueError(f"no row tile for T={t}")


def output_projection(lru, swa, diff, h, modtab, w_out, ln_g, ln_b, alpha, ctx_len):
    b, t, d = h.shape
    tm = _row_tile(t)
    kern = functools.partial(_outproj_kernel, alpha=alpha, ctx_len=ctx_len)
    row = lambda w: pl.BlockSpec((None, tm, w), lambda bi, i: (bi, i, 0))
    return pl.pallas_call(
        kern,
        out_shape=jax.ShapeDtypeStruct((b, t, d), F32),
        grid=(b, t // tm),
        in_specs=[row(LRU_WIDTH), row(SWA_WIDTH), row(DIFF_WIDTH), row(d),
                  pl.BlockSpec((None, 2, 6, d), lambda bi, i: (bi, 0, 0, 0)),
                  _const_spec((MIX_WIDTH, d)), _const_spec((1, d)), _const_spec((1, d))],
        out_specs=row(d),
        compiler_params=_cparams(("arbitrary", "arbitrary")),
        name="output_projection",
    )(lru, swa, diff, h, modtab, w_out, ln_g, ln_b)


FF_CHUNK = 256


def _swiglu_hidden(u, w1_ref, w3_ref, g_s, scale=None):
    ff = w1_ref.shape[-1]
    for c0 in range(0, ff, FF_CHUNK):
        h1 = jnp.dot(u, w1_ref[:, c0:c0 + FF_CHUNK], preferred_element_type=F32)
        h3 = jnp.dot(u, w3_ref[:, c0:c0 + FF_CHUNK], preferred_element_type=F32)
        g = (h1 * jax.nn.sigmoid(h1)) * h3
        if scale is not None:
            g = g * scale
        g_s[:, c0:c0 + FF_CHUNK] = g.astype(BF16)


def _ffn_kernel(h_ref, mod_ref, w1_ref, w3_ref, w2_ref, g_ref, b_ref, o_ref, g_s, *, alpha, ctx_len):
    tm = h_ref.shape[0]
    is_ctx = _is_ctx_rows(tm, ctx_len)
    h = h_ref[...]
    u = (h * (1.0 + _mod_rows(mod_ref, 4, is_ctx)) + _mod_rows(mod_ref, 3, is_ctx)).astype(BF16)
    _swiglu_hidden(u, w1_ref, w3_ref, g_s)
    f = jnp.dot(g_s[...], w2_ref[...], preferred_element_type=F32)
    o_ref[...] = _deepnorm_ln(h, f, _mod_rows(mod_ref, 5, is_ctx), g_ref[...], b_ref[...], alpha)


def dense_ffn(h, modtab, w1, w3, w2, ln_g, ln_b, alpha, ctx_len):
    b, t, d = h.shape
    ff = w1.shape[1]
    assert ff % FF_CHUNK == 0
    tm = _row_tile(t)
    kern = functools.partial(_ffn_kernel, alpha=alpha, ctx_len=ctx_len)
    row = pl.BlockSpec((None, tm, d), lambda bi, i: (bi, i, 0))
    return pl.pallas_call(
        kern,
        out_shape=jax.ShapeDtypeStruct((b, t, d), F32),
        grid=(b, t // tm),
        in_specs=[row, pl.BlockSpec((None, 2, 6, d), lambda bi, i: (bi, 0, 0, 0)),
                  _const_spec((d, ff)), _const_spec((d, ff)), _const_spec((ff, d)),
                  _const_spec((1, d)), _const_spec((1, d))],
        out_specs=row,
        scratch_shapes=[pltpu.VMEM((tm, ff), BF16)],
        compiler_params=_cparams(("arbitrary", "arbitrary")),
        name="dense_ffn",
    )(h, modtab, w1, w3, w2, ln_g, ln_b)


MOE_TILE = 512
ROUTE_E1, ROUTE_E2, ROUTE_R1, ROUTE_R2 = 0, 1, 2, 3
HI16 = 0xFFFF0000


def _pack_rows(x):
    half = x.shape[1] // 2
    lo = lax.bitcast_convert_type(x[:, :half].astype(BF16).astype(F32), jnp.uint32)
    hi = lax.bitcast_convert_type(x[:, half:].astype(BF16).astype(F32), jnp.uint32)
    return (lo >> 16) | (hi & jnp.uint32(HI16))


def _unpack_rows(w):
    lo = lax.bitcast_convert_type(w << 16, F32)
    hi = lax.bitcast_convert_type(w & jnp.uint32(HI16), F32)
    return lo, hi


def _route_kernel(h_ref, mod_ref, rw_ref, rb_ref, ri_ref, rg_ref, cnt_ref, cnt_s, *, ctx_len):
    tm = h_ref.shape[0]
    is_ctx = _is_ctx_rows(tm, ctx_len)

    @pl.when(jnp.logical_and(pl.program_id(0) == 0, pl.program_id(1) == 0))
    def _():
        cnt_s[...] = jnp.zeros(cnt_s.shape, F32)

    h = h_ref[...]
    u = (h * (1.0 + _mod_rows(mod_ref, 4, is_ctx)) + _mod_rows(mod_ref, 3, is_ctx)).astype(BF16)
    logits = jnp.dot(u, rw_ref[...], preferred_element_type=F32) + rb_ref[...]
    lane = lax.broadcasted_iota(jnp.int32, logits.shape, 1)
    m1 = jnp.max(logits, axis=-1, keepdims=True)
    i1 = jnp.min(jnp.where(logits == m1, lane, LANES), axis=-1, keepdims=True)
    rest = jnp.where(lane == i1, -jnp.inf, logits)
    m2 = jnp.max(rest, axis=-1, keepdims=True)
    i2 = jnp.min(jnp.where(rest == m2, lane, LANES), axis=-1, keepdims=True)
    e = jnp.exp(m2 - m1)
    g1 = 1.0 / (1.0 + e)
    g2 = e / (1.0 + e)
    sel1 = lane == i1
    sel2 = lane == i2
    picked = jnp.where(jnp.logical_or(sel1, sel2), 1.0, 0.0)
    before = lax.broadcasted_iota(jnp.int32, (tm, tm), 1) < lax.broadcasted_iota(jnp.int32, (tm, tm), 0)
    prefix = jnp.dot(jnp.where(before, 1.0, 0.0).astype(BF16), picked.astype(BF16), preferred_element_type=F32)
    rank = cnt_s[...] + prefix
    r1 = jnp.sum(jnp.where(sel1, rank, 0.0), axis=-1, keepdims=True)
    r2 = jnp.sum(jnp.where(sel2, rank, 0.0), axis=-1, keepdims=True)
    cnt_s[...] = cnt_s[...] + jnp.sum(picked, axis=0, keepdims=True)
    cnt_ref[...] = cnt_s[...]
    rg_ref[...] = jnp.where(lane == 0, g1, jnp.where(lane == 1, g2, 0.0))
    r1_hi = jnp.floor(r1 * (1.0 / 256.0))
    r2_hi = jnp.floor(r2 * (1.0 / 256.0))
    cols = (i1.astype(F32), i2.astype(F32), r1_hi, r1 - 256.0 * r1_hi, r2_hi, r2 - 256.0 * r2_hi)
    table = jnp.zeros(logits.shape, F32)
    for k, col in enumerate(cols):
        table = jnp.where(lane == k, col, table)
    pick = (lax.broadcasted_iota(jnp.int32, (8, LANES), 0) == lax.broadcasted_iota(jnp.int32, (8, LANES), 1))
    rows = lax.dot_general(jnp.where(pick, 1.0, 0.0).astype(BF16), table.astype(BF16),
                           (((1,), (1,)), ((), ())), preferred_element_type=F32)
    out = jnp.concatenate([rows[0:2], 256.0 * rows[2:3] + rows[3:4], 256.0 * rows[4:5] + rows[5:6],
                           jnp.zeros((4, tm), F32)], axis=0)
    ri_ref[...] = out.astype(jnp.int32)


def moe_route(h, modtab, router_w, router_b, ctx_len):
    b, t, d = h.shape
    tm = _row_tile(t)
    nt = t // tm
    kern = functools.partial(_route_kernel, ctx_len=ctx_len)
    row = lambda w: pl.BlockSpec((None, tm, w), lambda bi, i: (bi, i, 0))
    return pl.pallas_call(
        kern,
        out_shape=(jax.ShapeDtypeStruct((b * nt, 8, tm), jnp.int32), jax.ShapeDtypeStruct((b, t, LANES), F32),
                   jax.ShapeDtypeStruct((1, LANES), F32)),
        grid=(b, nt),
        in_specs=[row(d), pl.BlockSpec((None, 2, 6, d), lambda bi, i: (bi, 0, 0, 0)),
                  _const_spec((d, LANES)), _const_spec((1, LANES))],
        out_specs=(pl.BlockSpec((None, 8, tm), lambda bi, i: (bi * nt + i, 0, 0)), row(LANES),
                   pl.BlockSpec((1, LANES), lambda bi, i: (0, 0))),
        scratch_shapes=[pltpu.VMEM((1, LANES), F32)],
        compiler_params=_cparams(("arbitrary", "arbitrary")),
        name="moe_route",
    )(h, modtab, router_w, router_b)


def _row_dma_params(vmem=VMEM_LIMIT):
    return pltpu.CompilerParams(dimension_semantics=("arbitrary", "arbitrary"), vmem_limit_bytes=vmem,
                                disable_bounds_checks=True)


def _dispatch_kernel(s1_ref, s2_ref, h_ref, mod_ref, xs_in, xs_out, w_s, sem, *, ctx_len):
    del xs_in
    tm = h_ref.shape[0]
    g = pl.program_id(0) * pl.num_programs(1) + pl.program_id(1)
    last = pl.num_programs(0) * pl.num_programs(1) - 1
    cur = g % 2

    def drain(buf):
        for k in range(TOP_K):
            pltpu.make_async_copy(w_s.at[buf], xs_out.at[pl.ds(0, tm), :], sem.at[buf, k]).wait()

    @pl.when(g >= 2)
    def _():
        drain(cur)

    is_ctx = _is_ctx_rows(tm, ctx_len)
    h = h_ref[...]
    w_s[cur] = _pack_rows(h * (1.0 + _mod_rows(mod_ref, 4, is_ctx)) + _mod_rows(mod_ref, 3, is_ctx))

    def row_copy(r, slots, k):
        return pltpu.make_async_copy(w_s.at[cur, pl.ds(r, 1), :], xs_out.at[pl.ds(slots[0, r], 1), :],
                                     sem.at[cur, k])

    def issue(r, carry):
        row_copy(r, s1_ref, 0).start()
        row_copy(r, s2_ref, 1).start(priority=1)
        return carry

    lax.fori_loop(0, tm, issue, 0, unroll=8)

    @pl.when(g == last)
    def _():
        @pl.when(g >= 1)
        def _():
            drain(1 - cur)
        drain(cur)


def moe_dispatch(h, modtab, slot1, slot2, n_slots, ctx_len):
    b, t, d = h.shape
    tm = _row_tile(t)
    nt = t // tm
    kern = functools.partial(_dispatch_kernel, ctx_len=ctx_len)
    slot_spec = pl.BlockSpec((None, 1, tm), lambda bi, i: (bi * nt + i, 0, 0), memory_space=pltpu.SMEM)
    return pl.pallas_call(
        kern,
        out_shape=jax.ShapeDtypeStruct((n_slots, d // 2), jnp.uint32),
        grid=(b, nt),
        in_specs=[slot_spec, slot_spec,
                  pl.BlockSpec((None, tm, d), lambda bi, i: (bi, i, 0)),
                  pl.BlockSpec((None, 2, 6, d), lambda bi, i: (bi, 0, 0, 0)),
                  pl.BlockSpec(memory_space=pl.ANY)],
        out_specs=pl.BlockSpec(memory_space=pl.ANY),
        scratch_shapes=[pltpu.VMEM((2, tm, d // 2), jnp.uint32), pltpu.SemaphoreType.DMA((2, TOP_K))],
        input_output_aliases={4: 0},
        compiler_params=_row_dma_params(),
        name="moe_dispatch",
    )(slot1.reshape(b * nt, 1, tm), slot2.reshape(b * nt, 1, tm), h, modtab,
      jnp.zeros((n_slots, d // 2), jnp.uint32))


def _gffn_kernel(te_ref, na_ref, xs_ref, w1_ref, w3_ref, w2_ref, ys_ref, g_s):
    del te_ref
    i = pl.program_id(0)

    @pl.when(i < na_ref[0])
    def _():
        lo, hi = _unpack_rows(xs_ref[...])
        u = jnp.concatenate([lo.astype(BF16), hi.astype(BF16)], axis=1)
        _swiglu_hidden(u, w1_ref, w3_ref, g_s)
        ys_ref[...] = _pack_rows(jnp.dot(g_s[...], w2_ref[...], preferred_element_type=F32))

    @pl.when(i >= na_ref[0])
    def _():
        ys_ref[...] = jnp.zeros(ys_ref.shape, ys_ref.dtype)


def moe_grouped_ffn(xs, tile_expert, n_active, w1, w3, w2):
    n_slots, dh = xs.shape
    _, d, ff = w1.shape
    tm = MOE_TILE
    n_tiles = n_slots // tm
    return pl.pallas_call(
        _gffn_kernel,
        out_shape=jax.ShapeDtypeStruct((n_slots, dh), jnp.uint32),
        grid_spec=pltpu.PrefetchScalarGridSpec(
            num_scalar_prefetch=2,
            grid=(n_tiles,),
            in_specs=[pl.BlockSpec((tm, dh), lambda i, te, na: (i, 0)),
                      pl.BlockSpec((None, d, ff), lambda i, te, na: (te[i], 0, 0)),
                      pl.BlockSpec((None, d, ff), lambda i, te, na: (te[i], 0, 0)),
                      pl.BlockSpec((None, ff, d), lambda i, te, na: (te[i], 0, 0))],
            out_specs=pl.BlockSpec((tm, dh), lambda i, te, na: (i, 0)),
            scratch_shapes=[pltpu.VMEM((tm, ff), BF16)]),
        compiler_params=_cparams(("arbitrary",)),
        name="moe_grouped_ffn",
    )(tile_expert, n_active, xs, w1, w3, w2)


def _combine_kernel(s1_ref, s2_ref, n1_ref, n2_ref, ys_hbm, h_ref, mod_ref, rg_ref, g_ref, b_ref, o_ref, y_s, sem,
                    *, alpha, ctx_len, tile0):
    tm = h_ref.shape[0]
    g = pl.program_id(0) * pl.num_programs(1) + pl.program_id(1)
    last = pl.num_programs(0) * pl.num_programs(1) - 1
    cur = g % 2

    def gather(slots1, slots2, buf):
        def row_copy(r, slots, k):
            return pltpu.make_async_copy(ys_hbm.at[pl.ds(slots[0, r], 1), :], y_s.at[buf, k, pl.ds(r, 1), :],
                                         sem.at[buf, k])

        def issue(r, carry):
            row_copy(r, slots1, 0).start()
            row_copy(r, slots2, 1).start(priority=1)
            return carry

        lax.fori_loop(0, tm, issue, 0, unroll=8)

    @pl.when(g == 0)
    def _():
        gather(s1_ref, s2_ref, cur)

    @pl.when(g < last)
    def _():
        gather(n1_ref, n2_ref, 1 - cur)

    for k in range(TOP_K):
        pltpu.make_async_copy(ys_hbm.at[pl.ds(0, tm), :], y_s.at[cur, k], sem.at[cur, k]).wait()

    is_ctx = ((pl.program_id(1) + tile0) * tm + lax.broadcasted_iota(jnp.int32, (tm, 1), 0)) < ctx_len
    lo1, hi1 = _unpack_rows(y_s[cur, 0])
    lo2, hi2 = _unpack_rows(y_s[cur, 1])
    g1 = rg_ref[:, 0:1]
    g2 = rg_ref[:, 1:2]
    f = jnp.concatenate([g1 * lo1 + g2 * lo2, g1 * hi1 + g2 * hi2], axis=1)
    o_ref[...] = _deepnorm_ln(h_ref[...], f, _mod_rows(mod_ref, 5, is_ctx), g_ref[...], b_ref[...], alpha)


def moe_combine(ys, slot1, slot2, gates, h, modtab, ln_g, ln_b, alpha, ctx_len, tm, row0):
    b, t, d = h.shape
    assert row0 % tm == 0 and (t - row0) % tm == 0
    tile0 = row0 // tm
    n = (t - row0) // tm
    steps = b * n
    kern = functools.partial(_combine_kernel, alpha=alpha, ctx_len=ctx_len, tile0=tile0)
    cur_spec = pl.BlockSpec((None, 1, tm), lambda bi, i: (bi * n + i, 0, 0), memory_space=pltpu.SMEM)
    nxt_spec = pl.BlockSpec((None, 1, tm), lambda bi, i: (jnp.minimum(bi * n + i + 1, steps - 1), 0, 0),
                            memory_space=pltpu.SMEM)
    row = lambda w: pl.BlockSpec((None, tm, w), lambda bi, i: (bi, i + tile0, 0))
    s1 = slot1[:, row0:].reshape(steps, 1, tm)
    s2 = slot2[:, row0:].reshape(steps, 1, tm)
    return pl.pallas_call(
        kern,
        out_shape=jax.ShapeDtypeStruct((b, t - row0, d), F32),
        grid=(b, n),
        in_specs=[cur_spec, cur_spec, nxt_spec, nxt_spec, pl.BlockSpec(memory_space=pl.ANY), row(d),
                  pl.BlockSpec((None, 2, 6, d), lambda bi, i: (bi, 0, 0, 0)), row(LANES),
                  _const_spec((1, d)), _const_spec((1, d))],
        out_specs=pl.BlockSpec((None, tm, d), lambda bi, i: (bi, i, 0)),
        scratch_shapes=[pltpu.VMEM((2, TOP_K, tm, d // 2), jnp.uint32), pltpu.SemaphoreType.DMA((2, TOP_K))],
        compiler_params=_row_dma_params(),
        name="moe_combine",
    )(s1, s2, s1, s2, ys, h, modtab, gates, ln_g, ln_b)


def moe_ffn(h, modtab, router_w, router_b, w1, w3, w2, ln_g, ln_b, alpha, ctx_len, latent_only):
    b, t, d = h.shape
    n_experts = w1.shape[0]
    n_tiles = -(-(TOP_K * b * t) // MOE_TILE) + n_experts
    ri, gates, cnt = moe_route(h, modtab, router_w, router_b, ctx_len)
    counts = cnt[0, :n_experts].astype(jnp.int32)
    tiles_e = (counts + MOE_TILE - 1) // MOE_TILE
    ends = jnp.cumsum(tiles_e)
    offs = (ends - tiles_e) * MOE_TILE
    n_active = ends[-1:]
    tile_ids = jnp.minimum(jnp.arange(n_tiles, dtype=jnp.int32), n_active - 1)
    tile_expert = jnp.sum((ends[None, :] <= tile_ids[:, None]).astype(jnp.int32), axis=1)

    def slots(e, r):
        off = jnp.zeros_like(r)
        for k in range(n_experts):
            off = jnp.where(e == k, offs[k], off)
        return (off + r).reshape(b, t)

    slot1 = slots(ri[:, ROUTE_E1, :], ri[:, ROUTE_R1, :])
    slot2 = slots(ri[:, ROUTE_E2, :], ri[:, ROUTE_R2, :])
    xs = moe_dispatch(h, modtab, slot1, slot2, n_tiles * MOE_TILE, ctx_len)
    ys = moe_grouped_ffn(xs, tile_expert, n_active.astype(jnp.int32), w1, w3, w2)
    tm, row0 = (TOK_TILE, ctx_len) if latent_only else (_row_tile(t), 0)
    return moe_combine(ys, slot1, slot2, gates, h, modtab, ln_g, ln_b, alpha, ctx_len, tm, row0)


def _rope_table(rows, ctx_len):
    def table(rot_dim):
        n_freq = rot_dim // 4
        inv = ROPE_BASE ** (-jnp.arange(n_freq, dtype=F32) / n_freq)
        row = jnp.repeat(jnp.arange(rows, dtype=F32), GRID_W)
        col = jnp.tile(jnp.arange(GRID_W, dtype=F32), rows)
        ang = jnp.concatenate([row[:, None] * inv, col[:, None] * inv], -1)
        cos, sin = jnp.cos(ang), jnp.sin(ang)
        reps = LANES // rot_dim
        cos_t = jnp.tile(jnp.concatenate([cos, cos], -1), (1, reps))
        sin_t = jnp.tile(jnp.concatenate([-sin, sin], -1), (1, reps))
        ident = (jnp.ones((ctx_len, LANES), F32), jnp.zeros((ctx_len, LANES), F32))
        return jnp.concatenate([ident[0], cos_t], 0), jnp.concatenate([ident[1], sin_t], 0)
    c64, s64 = table(HEAD_DIM)
    c32, s32 = table(DIFF_QK)
    return jnp.concatenate([c64, s64, c32, s32], axis=-1)


def _gate_dense(gate_w):
    cols = []
    for d in range(2):
        for g in range(2):
            cols.append(jax.scipy.linalg.block_diag(*[gate_w[d, g, k] for k in range(LRU_BLOCKS)]))
    return jnp.concatenate(cols, axis=1)


def kernel(x, c, ctx, c_ctx, ada_w, ada_b, w_in, w_out, lru_conv_w, lru_conv_b, lru_gate_w, lru_gate_b, lru_lam,
           swa_sink, diff_lam, diff_norm_g, ln_g, ln_b, ffn_w1, ffn_w3, ffn_w2, moe_router_w, moe_router_b,
           moe_w1, moe_w3, moe_w2):
    b, s, d = x.shape
    ctx_len = ctx.shape[1]
    depth = ada_w.shape[0]
    n_experts = moe_router_w.shape[-1]
    alpha = (2.0 * depth) ** 0.25
    assert s % GRID_W == 0 and s % TOK_TILE == 0 and ctx_len == TOK_TILE

    h = jnp.concatenate([ctx, x], axis=1)
    ropetab = _rope_table(s // GRID_W, ctx_len)

    rows = 8 * ((b + 1 + 7) // 8)
    cvec = jnp.zeros((rows, d), F32).at[:b].set(c).at[b].set(c_ctx)
    mods = ada_modulation(cvec, ada_w, ada_b).reshape(depth, rows, 6, d)
    modtab = jnp.stack([jnp.broadcast_to(mods[:, b:b + 1], (depth, b, 6, d)), mods[:, :b]], axis=2)

    sink_rows = jnp.repeat(swa_sink.reshape(depth, SWA_KV_HEADS, 1, SWA_GROUP), ATTN_BLOCK, axis=-1)

    for layer in range(depth):
        lam_init = 0.8 - 0.6 * math.exp(-0.3 * layer)
        mt = modtab[layer]
        lru_xg, attn_t = input_projection(h, mt, w_in[layer].astype(BF16), ropetab)
        lru = rglru_mix(lru_xg, lru_conv_w[layer], lru_conv_b[layer].reshape(1, LRU_WIDTH),
                        _gate_dense(lru_gate_w[layer]).astype(BF16), lru_gate_b[layer].reshape(1, 4 * LRU_WIDTH),
                        lru_lam[layer], ctx_len)
        swa = windowed_attention(attn_t, sink_rows[layer])
        dif = differential_attention(attn_t, diff_lam[layer], diff_norm_g[layer], lam_init, layer < depth - 1)
        h = output_projection(lru, swa, dif, h, mt, w_out[layer].astype(BF16), ln_g[layer, 0:1], ln_b[layer, 0:1],
                              alpha, ctx_len)
        j = layer // 2
        if layer % 2 == 0:
            h = dense_ffn(h, mt, ffn_w1[j].astype(BF16), ffn_w3[j].astype(BF16), ffn_w2[j].astype(BF16),
                          ln_g[layer, 1:2], ln_b[layer, 1:2], alpha, ctx_len)
        else:
            rw = jnp.zeros((d, LANES), F32).at[:, :n_experts].set(moe_router_w[j]).astype(BF16)
            rb = jnp.full((1, LANES), NEG_INF, F32).at[0, :n_experts].set(moe_router_b[j])
            h = moe_ffn(h, mt, rw, rb, moe_w1[j].astype(BF16), moe_w3[j].astype(BF16), moe_w2[j].astype(BF16),
                        ln_g[layer, 1:2], ln_b[layer, 1:2], alpha, ctx_len, latent_only=layer == depth - 1)
    return h if depth % 2 == 0 else h[:, ctx_len:, :]
```

```python
import functools
import math

import jax
import jax.numpy as jnp
from jax import lax
from jax.experimental import pallas as pl
from jax.experimental.pallas import tpu as pltpu

F32 = jnp.float32
BF16 = jnp.bfloat16

GRID_W = 64
HEAD_DIM = 64
LRU_WIDTH = 256
LRU_BLOCKS = 4
LRU_BLOCK = LRU_WIDTH // LRU_BLOCKS
CONV_W = 4
LRU_C = 8.0
SWA_Q_HEADS = 8
SWA_KV_HEADS = 2
SWA_GROUP = SWA_Q_HEADS // SWA_KV_HEADS
SWA_WIDTH = SWA_Q_HEADS * HEAD_DIM
ATTN_BLOCK = 128
DIFF_HEADS = 4
DIFF_QK = HEAD_DIM // 2
DIFF_WIDTH = DIFF_HEADS * HEAD_DIM
MIX_WIDTH = LRU_WIDTH + SWA_WIDTH + DIFF_WIDTH
TOP_K = 2
ROPE_BASE = 10000.0
LN_EPS = 1e-5
NEG_INF = -1e30
ATTN_MIN_SUM_LOG2 = -100.0
COL_LRU_X, COL_LRU_G, COL_SWA_Q, COL_SWA_K, COL_SWA_V, COL_DIFF_Q, COL_DIFF_K, COL_DIFF_V, IN_COLS = (
    0, 256, 512, 1024, 1152, 1280, 1536, 1792, 2048)
ATTN_COLS = IN_COLS - COL_SWA_Q
ROW_SWA_Q, ROW_SWA_KV, ROW_DIFF_Q, ROW_DIFF_K, ROW_DIFF_V = 0, 512, 768, 1024, 1280

LANES = 128
TOK_TILE = 256
VMEM_LIMIT = 56 * 1024 * 1024


def _cparams(sem, vmem=VMEM_LIMIT):
    return pltpu.CompilerParams(dimension_semantics=sem, vmem_limit_bytes=vmem)


def _const_spec(shape):
    nd = len(shape)
    return pl.BlockSpec(shape, lambda *_: (0,) * nd, pipeline_mode=pl.Buffered(1))


def _layer_spec(shape, layer):
    nd = len(shape)
    return pl.BlockSpec((None,) + tuple(shape), lambda *_: (layer,) + (0,) * nd, pipeline_mode=pl.Buffered(1))


def _ada_kernel(c_ref, w_ref, b_ref, o_ref):
    c = c_ref[...]
    s = (c * jax.nn.sigmoid(c)).astype(BF16)
    o_ref[...] = jnp.dot(s, w_ref[...].astype(BF16), preferred_element_type=F32) + b_ref[...]


def ada_modulation(cvec, ada_w, ada_b):
    depth, d, n = ada_w.shape
    r = cvec.shape[0]
    tn = 1536
    assert n % tn == 0
    return pl.pallas_call(
        _ada_kernel,
        out_shape=jax.ShapeDtypeStruct((depth, r, n), F32),
        grid=(depth, n // tn),
        in_specs=[pl.BlockSpec((r, d), lambda l, j: (0, 0)),
                  pl.BlockSpec((None, d, tn), lambda l, j: (l, 0, j)),
                  pl.BlockSpec((None, 1, tn), lambda l, j: (l, 0, j))],
        out_specs=pl.BlockSpec((None, r, tn), lambda l, j: (l, 0, j)),
        compiler_params=_cparams(("arbitrary", "arbitrary")),
        name="ada_modulation",
    )(cvec, ada_w, ada_b.reshape(depth, 1, n))


def _rope_group(x, cos, sin_signed, half, lane_mod):
    swapped = jnp.where(lane_mod < half, pltpu.roll(x, LANES - half, axis=1), pltpu.roll(x, half, axis=1))
    return x * cos + swapped * sin_signed


def _inproj_kernel(h_ref, mod_ref, w_ref, rope_ref, lru_ref, attn_ref, *, swa_scale, diff_scale):
    h = h_ref[...]
    u = (h * (1.0 + mod_ref[1:2, :]) + mod_ref[0:1, :]).astype(BF16)
    acc = jnp.dot(u, w_ref[...], preferred_element_type=F32)
    lru_ref[...] = acc[:, :COL_SWA_Q]
    lane = lax.broadcasted_iota(jnp.int32, (1, LANES), 1)
    cos64, sin64 = rope_ref[:, 0:LANES], rope_ref[:, LANES:2 * LANES]
    cos32, sin32 = rope_ref[:, 2 * LANES:3 * LANES], rope_ref[:, 3 * LANES:4 * LANES]
    mod64, mod32 = lane % HEAD_DIM, lane % DIFF_QK
    for c0 in range(COL_SWA_Q, IN_COLS, LANES):
        x = acc[:, c0:c0 + LANES]
        if c0 < COL_SWA_V:
            x = _rope_group(x, cos64, sin64, HEAD_DIM // 2, mod64)
            if c0 < COL_SWA_K:
                x = x * swa_scale
        elif COL_DIFF_Q <= c0 < COL_DIFF_V:
            x = _rope_group(x, cos32, sin32, DIFF_QK // 2, mod32)
            if c0 < COL_DIFF_K:
                x = x * diff_scale
        r0 = c0 - COL_SWA_Q
        attn_ref[r0:r0 + LANES, :] = x.astype(BF16).T


def input_projection(h, modtab, w_in, layer, ropetab):
    b, t, d = h.shape
    tm = TOK_TILE
    nt = t // tm
    log2e = math.log2(math.e)
    kern = functools.partial(_inproj_kernel, swa_scale=HEAD_DIM ** -0.5 * log2e, diff_scale=DIFF_QK ** -0.5 * log2e)
    return pl.pallas_call(
        kern,
        out_shape=(jax.ShapeDtypeStruct((b, t, COL_SWA_Q), F32),
                   jax.ShapeDtypeStruct((b, nt, ATTN_COLS, tm), BF16)),
        grid=(b, nt),
        in_specs=[pl.BlockSpec((None, tm, d), lambda bi, i: (bi, i, 0)),
                  pl.BlockSpec((None, None, 6, d), lambda bi, i: (bi, jnp.minimum(i, 1), 0, 0)),
                  _layer_spec((d, IN_COLS), layer),
                  pl.BlockSpec((tm, 4 * LANES), lambda bi, i: (i, 0))],
        out_specs=(pl.BlockSpec((None, tm, COL_SWA_Q), lambda bi, i: (bi, i, 0)),
                   pl.BlockSpec((None, None, ATTN_COLS, tm), lambda bi, i: (bi, i, 0, 0))),
        compiler_params=_cparams(("arbitrary", "arbitrary")),
        name="input_projection",
    )(h, modtab, w_in, ropetab)


def _neg_expm1(z):
    series = z * (1.0 + z * (1 / 2 + z * (1 / 6 + z * (1 / 24 + z * (1 / 120 + z * (1 / 720 + z * (1 / 5040)))))))
    return jnp.where(z > -0.125, -series, 1.0 - jnp.exp(z))


def _lru_kernel(xg_ref, cw_ref, cb_ref, wg_ref, gb_ref, lam_ref, o_ref, a_s, b_s, h_s, *, chunk, ctx_chunks):
    t, c = o_ref.shape
    nchunk = t // chunk
    row = lax.broadcasted_iota(jnp.int32, (chunk, 1), 0)
    sp = jax.nn.softplus(-lam_ref[...])
    cw = cw_ref[...]
    cb = cb_ref[...]

    def coeffs(j, carry):
        r0 = pl.multiple_of(j * chunk, chunk)
        x = xg_ref[pl.ds(r0, chunk), 0:c]
        prev8 = xg_ref[pl.ds(pl.multiple_of(jnp.maximum(r0 - 8, 0), 8), 8), 0:c]
        next8 = xg_ref[pl.ds(pl.multiple_of(jnp.minimum(r0 + chunk, t - 8), 8), 8), 0:c]
        seg_start = jnp.logical_or(j == 0, j == ctx_chunks)
        seg_end = jnp.logical_or(j == ctx_chunks - 1, j == nchunk - 1)
        pm1 = jnp.where(seg_start, 0.0, prev8[7:8, :])
        np0 = jnp.where(seg_end, 0.0, next8[0:1, :])
        np1 = jnp.where(seg_end, 0.0, next8[1:2, :])
        xm1 = jnp.where(row == 0, pm1, pltpu.roll(x, 1, axis=0))
        xp1 = jnp.where(row == chunk - 1, np0, pltpu.roll(x, chunk - 1, axis=0))
        xp2 = jnp.where(row == chunk - 1, np1, jnp.where(row == chunk - 2, np0, pltpu.roll(x, chunk - 2, axis=0)))
        u = cw[0:1, :] * xm1 + cw[1:2, :] * x + cw[2:3, :] * xp1 + cw[3:4, :] * xp2 + cb
        g = jnp.dot(u.astype(BF16), wg_ref[...], preferred_element_type=F32) + gb_ref[...]
        for d in range(2):
            r = jax.nn.sigmoid(g[:, (2 * d) * c:(2 * d + 1) * c])
            i = jax.nn.sigmoid(g[:, (2 * d + 1) * c:(2 * d + 2) * c])
            log_a = (-LRU_C) * r * sp[d:d + 1, :]
            a_s[d, pl.ds(r0, chunk), :] = jnp.exp(log_a)
            b_s[d, pl.ds(r0, chunk), :] = jnp.sqrt(_neg_expm1(2.0 * log_a)) * (i * u)
        return carry

    lax.fori_loop(0, nchunk, coeffs, 0)

    def scan_segment(lo, n8, hf, hb):
        def body(k, hh):
            hf, hb = hh
            f0 = pl.multiple_of(lo + k * 8, 8)
            r0 = pl.multiple_of(lo + (n8 - 1 - k) * 8, 8)
            af, bf = a_s[0, pl.ds(f0, 8), :], b_s[0, pl.ds(f0, 8), :]
            ab, bb = a_s[1, pl.ds(r0, 8), :], b_s[1, pl.ds(r0, 8), :]
            fw, bw = [], []
            for s in range(8):
                hf = af[s:s + 1, :] * hf + bf[s:s + 1, :]
                fw.append(hf)
                hb = ab[7 - s:8 - s, :] * hb + bb[7 - s:8 - s, :]
                bw.append(hb)
            h_s[0, pl.ds(f0, 8), :] = jnp.concatenate(fw, axis=0)
            h_s[1, pl.ds(r0, 8), :] = jnp.concatenate(bw[::-1], axis=0)
            return hf, hb
        return lax.fori_loop(0, n8, body, (hf, hb))

    zero = jnp.zeros((1, c), F32)
    ctx_rows = ctx_chunks * chunk
    hf, hb = scan_segment(0, ctx_rows // 8, zero, zero)
    scan_segment(ctx_rows, (t - ctx_rows) // 8, hf, hb)

    def finish(j, carry):
        r0 = pl.multiple_of(j * chunk, chunk)
        y = h_s[0, pl.ds(r0, chunk), :] + h_s[1, pl.ds(r0, chunk), :]
        gate = xg_ref[pl.ds(r0, chunk), c:2 * c]
        o_ref[pl.ds(r0, chunk), :] = (y * jax.nn.gelu(gate)).astype(o_ref.dtype)
        return carry

    lax.fori_loop(0, nchunk, finish, 0)


def rglru_mix(lru_xg, conv_w, conv_b, gate_dense, gate_b, lam, ctx_len):
    b, t, c2 = lru_xg.shape
    c = c2 // 2
    chunk = TOK_TILE
    assert t % chunk == 0 and ctx_len % chunk == 0
    kern = functools.partial(_lru_kernel, chunk=chunk, ctx_chunks=ctx_len // chunk)
    return pl.pallas_call(
        kern,
        out_shape=jax.ShapeDtypeStruct((b, t, c), BF16),
        grid=(b,),
        in_specs=[pl.BlockSpec((None, t, c2), lambda bi: (bi, 0, 0), pipeline_mode=pl.Buffered(1)),
                  _const_spec((CONV_W, c)), _const_spec((1, c)), _const_spec((c, 4 * c)), _const_spec((1, 4 * c)),
                  _const_spec((2, c))],
        out_specs=pl.BlockSpec((None, t, c), lambda bi: (bi, 0, 0)),
        scratch_shapes=[pltpu.VMEM((2, t, c), F32), pltpu.VMEM((2, t, c), F32), pltpu.VMEM((2, t, c), F32)],
        compiler_params=_cparams(("arbitrary",)),
        name="rglru_mix",
    )(lru_xg, conv_w, conv_b, gate_dense, gate_b, lam)


def _dot_t0(a, b):
    return lax.dot_general(a, b, (((0,), (0,)), ((), ())), preferred_element_type=F32)


SWA_MAX_KEY_BLOCKS = 5


def _swa_kernel(q_ref, own_ref, prev_ref, next_ref, ctx_ref, sink_ref, o_ref, s_s, p_s, l_s, ot_s):
    i = pl.program_id(1)
    nt = pl.num_programs(1)
    blk = ATTN_BLOCK
    nq = SWA_GROUP * blk
    kp = lax.broadcasted_iota(jnp.int32, (blk, nq), 0)
    qp = lax.broadcasted_iota(jnp.int32, (blk, nq), 1) % blk

    def kv(ref, hk, half):
        k = ref[hk * HEAD_DIM:(hk + 1) * HEAD_DIM, half * blk:(half + 1) * blk]
        v = ref[SWA_KV_HEADS * HEAD_DIM + hk * HEAD_DIM:SWA_KV_HEADS * HEAD_DIM + (hk + 1) * HEAD_DIM,
                half * blk:(half + 1) * blk]
        return k, v

    def q_tile(hk, half):
        return jnp.concatenate(
            [q_ref[(hk * SWA_GROUP + g) * HEAD_DIM:(hk * SWA_GROUP + g + 1) * HEAD_DIM,
                   half * blk:(half + 1) * blk] for g in range(SWA_GROUP)], axis=1)

    def put(hk, half, o):
        for g in range(SWA_GROUP):
            f0 = (hk * SWA_GROUP + g) * HEAD_DIM
            ot_s[f0:f0 + HEAD_DIM, half * blk:(half + 1) * blk] = o[:, g * blk:(g + 1) * blk]

    def run_bounded(jobs):
        ones = jnp.ones((16, SWA_MAX_KEY_BLOCKS * blk), BF16)
        for a, (hk, half, keys, _, masks) in enumerate(jobs):
            qt = q_tile(hk, half)
            qf = qt.astype(F32)
            kmax2 = jnp.zeros((1, 1), F32)
            for kt in keys:
                kf = kt.astype(F32)
                kmax2 = jnp.maximum(kmax2, jnp.max(jnp.sum(kf * kf, axis=0, keepdims=True), axis=1, keepdims=True))
            ref = jnp.maximum(jnp.sqrt(jnp.sum(qf * qf, axis=0, keepdims=True) * kmax2), sink_ref[hk])
            l_s[a:a + 1, :] = ref
            for j, (kt, msk) in enumerate(zip(keys, masks)):
                sj = _dot_t0(kt, qt)
                if msk is not None:
                    sj = jnp.where(msk, sj, NEG_INF)
                p_s[a, j * blk:(j + 1) * blk, :] = jnp.exp2(sj - ref).astype(BF16)
        worst = None
        for a, (hk, half, keys, vals, _) in enumerate(jobs):
            nk = len(keys) * blk
            v1 = jnp.concatenate([jnp.concatenate(vals, axis=1), ones[:, 0:nk]], axis=0)
            oe = jnp.dot(v1, p_s[a, 0:nk, :], preferred_element_type=F32)
            l = oe[HEAD_DIM:HEAD_DIM + 1, :] + jnp.exp2(sink_ref[hk] - l_s[a:a + 1, :])
            put(hk, half, oe[0:HEAD_DIM, :] / l)
            worst = l if worst is None else jnp.minimum(worst, l)
        return jnp.min(worst) < 2.0 ** ATTN_MIN_SUM_LOG2

    def run_exact(jobs):
        for a, (hk, half, keys, _, _) in enumerate(jobs):
            qt = q_tile(hk, half)
            for j, kt in enumerate(keys):
                s_s[a, j * blk:(j + 1) * blk, :] = _dot_t0(kt, qt)
        for a, (hk, _, keys, _, masks) in enumerate(jobs):
            sink = sink_ref[hk]
            s = []
            for j, msk in enumerate(masks):
                sj = s_s[a, j * blk:(j + 1) * blk, :]
                s.append(sj if msk is None else jnp.where(msk, sj, NEG_INF))
            m = sink
            for sj in s:
                m = jnp.maximum(m, jnp.max(sj, axis=0, keepdims=True))
            l = jnp.exp2(sink - m)
            for j, sj in enumerate(s):
                pj = jnp.exp2(sj - m)
                l = l + jnp.sum(pj, axis=0, keepdims=True)
                p_s[a, j * blk:(j + 1) * blk, :] = pj.astype(BF16)
            l_s[a:a + 1, :] = l
        for a, (hk, half, keys, vals, _) in enumerate(jobs):
            nk = len(keys) * blk
            vt = jnp.concatenate(vals, axis=1)
            put(hk, half, jnp.dot(vt, p_s[a, 0:nk, :], preferred_element_type=F32) / l_s[a:a + 1, :])

    def run(jobs):
        lost_range = run_bounded(jobs)

        @pl.when(lost_range)
        def _():
            run_exact(jobs)

    @pl.when(i == 0)
    def _():
        jobs = []
        for hk in range(SWA_KV_HEADS):
            k0, v0 = kv(ctx_ref, hk, 0)
            k1, v1 = kv(ctx_ref, hk, 1)
            for half in range(2):
                jobs.append((hk, half, [k0, k1], [v0, v1], [None, None]))
        run(jobs)

    @pl.when(i > 0)
    def _():
        has_prev = i > 1
        has_next = i < nt - 1
        m_prev = kp >= qp
        m_next = kp <= qp
        jobs = []
        for hk in range(SWA_KV_HEADS):
            c0k, c0v = kv(ctx_ref, hk, 0)
            c1k, c1v = kv(ctx_ref, hk, 1)
            o0k, o0v = kv(own_ref, hk, 0)
            o1k, o1v = kv(own_ref, hk, 1)
            pk, pv = kv(prev_ref, hk, 1)
            nk, nv = kv(next_ref, hk, 0)
            jobs.append((hk, 0, [c0k, c1k, pk, o0k, o1k], [c0v, c1v, pv, o0v, o1v],
                         [None, None, jnp.logical_and(m_prev, has_prev), None, m_next]))
            jobs.append((hk, 1, [c0k, c1k, o0k, o1k, nk], [c0v, c1v, o0v, o1v, nv],
                         [None, None, m_prev, None, jnp.logical_and(m_next, has_next)]))
        run(jobs)

    o_ref[...] = ot_s[...].T.astype(o_ref.dtype)


def windowed_attention(attn_t, sink_rows):
    b, nt, _, tm = attn_t.shape
    kvrows = 2 * SWA_KV_HEADS * HEAD_DIM
    kvblk = ROW_SWA_KV // kvrows
    assert ROW_SWA_KV % kvrows == 0 and tm == 2 * ATTN_BLOCK
    return pl.pallas_call(
        _swa_kernel,
        out_shape=jax.ShapeDtypeStruct((b, nt * tm, SWA_WIDTH), BF16),
        grid=(b, nt),
        in_specs=[pl.BlockSpec((None, None, SWA_WIDTH, tm), lambda bi, i: (bi, i, 0, 0)),
                  pl.BlockSpec((None, None, kvrows, tm), lambda bi, i: (bi, i, kvblk, 0)),
                  pl.BlockSpec((None, None, kvrows, tm), lambda bi, i: (bi, jnp.maximum(i - 1, 1), kvblk, 0)),
                  pl.BlockSpec((None, None, kvrows, tm), lambda bi, i: (bi, jnp.minimum(i + 1, nt - 1), kvblk, 0)),
                  pl.BlockSpec((None, None, kvrows, tm), lambda bi, i: (bi, 0, kvblk, 0)),
                  pl.BlockSpec((SWA_KV_HEADS, 1, SWA_GROUP * ATTN_BLOCK), lambda bi, i: (0, 0, 0))],
        out_specs=pl.BlockSpec((None, tm, SWA_WIDTH), lambda bi, i: (bi, i, 0)),
        scratch_shapes=[pltpu.VMEM((2 * SWA_KV_HEADS, SWA_MAX_KEY_BLOCKS * ATTN_BLOCK, SWA_GROUP * ATTN_BLOCK), F32),
                        pltpu.VMEM((2 * SWA_KV_HEADS, SWA_MAX_KEY_BLOCKS * ATTN_BLOCK, SWA_GROUP * ATTN_BLOCK), BF16),
                        pltpu.VMEM((2 * SWA_KV_HEADS, SWA_GROUP * ATTN_BLOCK), F32),
                        pltpu.VMEM((SWA_WIDTH, tm), F32)],
        compiler_params=_cparams(("arbitrary", "arbitrary")),
        name="windowed_attention",
    )(attn_t, attn_t, attn_t, attn_t, attn_t, sink_rows)


DIFF_ACC_ROWS = HEAD_DIM + 16
DIFF_Q_TILES = 4


def _diff_kernel(q_ref, k_ref, v_ref, lam_ref, g_ref, o_ref, m_s, a_s, acc_s, s_s, p_s, ot_s, *, lam_init):
    nkv = k_ref.shape[0]
    last = nkv - 1
    ncomp = 2 * DIFF_HEADS
    ones = jnp.ones((DIFF_ACC_ROWS - HEAD_DIM, k_ref.shape[-1]), BF16)

    def q_comp(c):
        return jnp.concatenate([q_ref[t, c * DIFF_QK:(c + 1) * DIFF_QK, :] for t in range(q_ref.shape[0])], axis=1)

    def add_values(j, slot, rescale):
        vt = v_ref[j]
        for c in range(ncomp):
            h = c // 2
            v1 = jnp.concatenate([vt[h * HEAD_DIM:(h + 1) * HEAD_DIM, :], ones], axis=0)
            old = a_s[slot, c:c + 1, :] * acc_s[c] if rescale else acc_s[c]
            acc_s[c] = old + jnp.dot(v1, p_s[slot, c], preferred_element_type=F32)

    def key_norms(j, mx):
        kf = k_ref[j].astype(F32)
        sq = kf * kf
        n2 = jnp.concatenate([jnp.sum(sq[c * DIFF_QK:(c + 1) * DIFF_QK, :], axis=0, keepdims=True)
                              for c in range(ncomp)], axis=0)
        return jnp.maximum(mx, n2)

    kmax2 = jnp.max(lax.fori_loop(0, nkv, key_norms, jnp.zeros((ncomp, k_ref.shape[-1]), F32)),
                    axis=1, keepdims=True)
    for c in range(ncomp):
        qf = q_comp(c).astype(F32)
        m_s[c:c + 1, :] = jnp.sqrt(jnp.sum(qf * qf, axis=0, keepdims=True) * kmax2[c:c + 1, :])
    acc_s[...] = jnp.zeros(acc_s.shape, F32)

    def bounded_probs(j, slot):
        kt = k_ref[j]
        for c in range(ncomp):
            s = _dot_t0(kt[c * DIFF_QK:(c + 1) * DIFF_QK, :], q_comp(c))
            p_s[slot, c] = jnp.exp2(s - m_s[c:c + 1, :]).astype(BF16)

    bounded_probs(0, 0)

    def bounded_pair(jj, carry):
        j = 2 * jj
        bounded_probs(j + 1, 1)
        add_values(j, 0, False)
        bounded_probs(j + 2, 0)
        add_values(j + 1, 1, False)
        return carry

    lax.fori_loop(0, last // 2, bounded_pair, 0)
    add_values(last, 0, False)

    min_sum = acc_s[0, HEAD_DIM:HEAD_DIM + 1, :]
    for c in range(1, ncomp):
        min_sum = jnp.minimum(min_sum, acc_s[c, HEAD_DIM:HEAD_DIM + 1, :])
    lost_range = jnp.min(min_sum) < 2.0 ** ATTN_MIN_SUM_LOG2

    def scores(j, slot):
        kt = k_ref[j]
        for c in range(ncomp):
            s_s[slot, c] = _dot_t0(kt[c * DIFF_QK:(c + 1) * DIFF_QK, :], q_comp(c))

    def softmax(slot):
        for c in range(ncomp):
            s = s_s[slot, c]
            m_old = m_s[c:c + 1, :]
            m_new = jnp.maximum(m_old, jnp.max(s, axis=0, keepdims=True))
            m_s[c:c + 1, :] = m_new
            a_s[slot, c:c + 1, :] = jnp.exp2(m_old - m_new)
            p_s[slot, c] = jnp.exp2(s - m_new).astype(BF16)

    @pl.when(lost_range)
    def _():
        m_s[...] = jnp.full(m_s.shape, NEG_INF, F32)
        acc_s[...] = jnp.zeros(acc_s.shape, F32)
        scores(0, 0)
        scores(min(1, last), 1)
        softmax(0)

        def pair(jj, carry):
            j = 2 * jj
            scores(j + 2, 0)
            softmax(1)
            add_values(j, 0, True)
            scores(jnp.minimum(j + 3, last), 1)
            softmax(0)
            add_values(j + 1, 1, True)
            return carry

        lax.fori_loop(0, last // 2, pair, 0)
        add_values(last, 0, True)

    lam_v = lam_ref[...]
    lam = (jnp.exp(jnp.sum(lam_v[0:1, :] * lam_v[1:2, :], axis=1, keepdims=True))
           - jnp.exp(jnp.sum(lam_v[2:3, :] * lam_v[3:4, :], axis=1, keepdims=True)) + lam_init)

    def head_out(c):
        return acc_s[c, 0:HEAD_DIM, :] / acc_s[c, HEAD_DIM:HEAD_DIM + 1, :]

    for h in range(DIFF_HEADS):
        o = head_out(2 * h) - lam * head_out(2 * h + 1)
        o = o * lax.rsqrt(jnp.mean(jnp.square(o), axis=0, keepdims=True) + LN_EPS)
        ot_s[h * HEAD_DIM:(h + 1) * HEAD_DIM, :] = (o * (1.0 - lam_init)) * g_ref[...]
    o_ref[...] = ot_s[...].T.astype(o_ref.dtype)


def _diff_call(q_tiles, q_feat_blk, kv, nk, nq, diff_lam, norm_g, lam_init):
    b, nqt, _, tm = q_tiles.shape
    assert nk % 2 == 1 and nqt % nq == 0
    tq = nq * tm
    ncomp = 2 * DIFF_HEADS
    kern = functools.partial(_diff_kernel, lam_init=lam_init)
    return pl.pallas_call(
        kern,
        out_shape=jax.ShapeDtypeStruct((b, nqt * tm, DIFF_WIDTH), BF16),
        grid=(b, nqt // nq),
        in_specs=[pl.BlockSpec((None, nq, DIFF_WIDTH, tm), lambda bi, i: (bi, i, q_feat_blk, 0)),
                  pl.BlockSpec((None, nk, DIFF_WIDTH, tm), lambda bi, i: (bi, 0, ROW_DIFF_K // DIFF_WIDTH, 0)),
                  pl.BlockSpec((None, nk, DIFF_WIDTH, tm), lambda bi, i: (bi, 0, ROW_DIFF_V // DIFF_WIDTH, 0)),
                  pl.BlockSpec((4, DIFF_QK), lambda bi, i: (0, 0)),
                  pl.BlockSpec((HEAD_DIM, 1), lambda bi, i: (0, 0))],
        out_specs=pl.BlockSpec((None, tq, DIFF_WIDTH), lambda bi, i: (bi, i, 0)),
        scratch_shapes=[pltpu.VMEM((ncomp, tq), F32), pltpu.VMEM((2, ncomp, tq), F32),
                        pltpu.VMEM((ncomp, DIFF_ACC_ROWS, tq), F32), pltpu.VMEM((2, ncomp, tm, tq), F32),
                        pltpu.VMEM((2, ncomp, tm, tq), BF16), pltpu.VMEM((DIFF_WIDTH, tq), F32)],
        compiler_params=_cparams(("arbitrary", "arbitrary")),
        name="differential_attention",
    )(q_tiles, kv, kv, diff_lam, norm_g.reshape(HEAD_DIM, 1))


def differential_attention(attn_t, diff_lam, norm_g, lam_init, need_ctx):
    b, nt, _, tm = attn_t.shape
    q_lat = attn_t[:, 1:, ROW_DIFF_Q:ROW_DIFF_Q + DIFF_WIDTH, :]
    lat = _diff_call(q_lat, 0, attn_t, nt, math.gcd(DIFF_Q_TILES, nt - 1), diff_lam, norm_g, lam_init)
    if need_ctx:
        ctx = _diff_call(attn_t[:, 0:1], ROW_DIFF_Q // DIFF_WIDTH, attn_t, 1, 1, diff_lam, norm_g, lam_init)
    else:
        ctx = jnp.zeros((b, tm, DIFF_WIDTH), BF16)
    return jnp.concatenate([ctx, lat], axis=1)


def _mod_rows(mod_ref, k, is_ctx):
    return jnp.where(is_ctx, mod_ref[0, k:k + 1, :], mod_ref[1, k:k + 1, :])


def _is_ctx_rows(tm, ctx_len):
    r0 = pl.program_id(1) * tm
    return (r0 + lax.broadcasted_iota(jnp.int32, (tm, 1), 0)) < ctx_len


def _deepnorm_ln(h, branch, gate, g, b, alpha):
    x = alpha * h + gate * branch
    mu = jnp.mean(x, axis=-1, keepdims=True)
    xc = x - mu
    var = jnp.mean(xc * xc, axis=-1, keepdims=True)
    return xc * lax.rsqrt(var + LN_EPS) * g + b


def _outproj_kernel(lru_ref, swa_ref, diff_ref, h_ref, mod_ref, w_ref, g_ref, b_ref, o_ref, *, alpha, ctx_len):
    tm = h_ref.shape[0]
    is_ctx = _is_ctx_rows(tm, ctx_len)
    m = jnp.dot(lru_ref[...], w_ref[0:LRU_WIDTH, :], preferred_element_type=F32)
    m = m + jnp.dot(swa_ref[...], w_ref[LRU_WIDTH:LRU_WIDTH + SWA_WIDTH, :], preferred_element_type=F32)
    m = m + jnp.dot(diff_ref[...], w_ref[LRU_WIDTH + SWA_WIDTH:MIX_WIDTH, :], preferred_element_type=F32)
    o_ref[...] = _deepnorm_ln(h_ref[...], m, _mod_rows(mod_ref, 2, is_ctx), g_ref[...], b_ref[...], alpha)


def _row_tile(t):
    for tm in range(640, 15, -16):
        if t % tm == 0:
            return tm
    raise ValueError(f"no row tile for T={t}")


def output_projection(lru, swa, diff, h, modtab, w_out, layer, ln_g, ln_b, alpha, ctx_len):
    b, t, d = h.shape
    tm = _row_tile(t)
    kern = functools.partial(_outproj_kernel, alpha=alpha, ctx_len=ctx_len)
    row = lambda w: pl.BlockSpec((None, tm, w), lambda bi, i: (bi, i, 0))
    return pl.pallas_call(
        kern,
        out_shape=jax.ShapeDtypeStruct((b, t, d), F32),
        grid=(b, t // tm),
        in_specs=[row(LRU_WIDTH), row(SWA_WIDTH), row(DIFF_WIDTH), row(d),
                  pl.BlockSpec((None, 2, 6, d), lambda bi, i: (bi, 0, 0, 0)),
                  _layer_spec((MIX_WIDTH, d), layer), _const_spec((1, d)), _const_spec((1, d))],
        out_specs=row(d),
        compiler_params=_cparams(("arbitrary", "arbitrary")),
        name="output_projection",
    )(lru, swa, diff, h, modtab, w_out, ln_g, ln_b)


FF_CHUNK = 256


def _swiglu_hidden(u, w1_ref, w3_ref, g_s, scale=None):
    ff = w1_ref.shape[-1]
    for c0 in range(0, ff, FF_CHUNK):
        h1 = jnp.dot(u, w1_ref[:, c0:c0 + FF_CHUNK], preferred_element_type=F32)
        h3 = jnp.dot(u, w3_ref[:, c0:c0 + FF_CHUNK], preferred_element_type=F32)
        g = (h1 * jax.nn.sigmoid(h1)) * h3
        if scale is not None:
            g = g * scale
        g_s[:, c0:c0 + FF_CHUNK] = g.astype(BF16)


def _ffn_kernel(h_ref, mod_ref, w1_ref, w3_ref, w2_ref, g_ref, b_ref, o_ref, g_s, *, alpha, ctx_len):
    tm = h_ref.shape[0]
    is_ctx = _is_ctx_rows(tm, ctx_len)
    h = h_ref[...]
    u = (h * (1.0 + _mod_rows(mod_ref, 4, is_ctx)) + _mod_rows(mod_ref, 3, is_ctx)).astype(BF16)
    _swiglu_hidden(u, w1_ref, w3_ref, g_s)
    f = jnp.dot(g_s[...], w2_ref[...], preferred_element_type=F32)
    o_ref[...] = _deepnorm_ln(h, f, _mod_rows(mod_ref, 5, is_ctx), g_ref[...], b_ref[...], alpha)


def dense_ffn(h, modtab, w1, w3, w2, j, ln_g, ln_b, alpha, ctx_len):
    b, t, d = h.shape
    ff = w1.shape[-1]
    assert ff % FF_CHUNK == 0
    tm = _row_tile(t)
    kern = functools.partial(_ffn_kernel, alpha=alpha, ctx_len=ctx_len)
    row = pl.BlockSpec((None, tm, d), lambda bi, i: (bi, i, 0))
    return pl.pallas_call(
        kern,
        out_shape=jax.ShapeDtypeStruct((b, t, d), F32),
        grid=(b, t // tm),
        in_specs=[row, pl.BlockSpec((None, 2, 6, d), lambda bi, i: (bi, 0, 0, 0)),
                  _layer_spec((d, ff), j), _layer_spec((d, ff), j), _layer_spec((ff, d), j),
                  _const_spec((1, d)), _const_spec((1, d))],
        out_specs=row,
        scratch_shapes=[pltpu.VMEM((tm, ff), BF16)],
        compiler_params=_cparams(("arbitrary", "arbitrary")),
        name="dense_ffn",
    )(h, modtab, w1, w3, w2, ln_g, ln_b)


MOE_TILE = 512
ROUTE_E1, ROUTE_E2, ROUTE_R1, ROUTE_R2 = 0, 1, 2, 3


def _route_kernel(h_ref, mod_ref, rw_ref, rb_ref, ri_ref, rg_ref, cnt_ref, cnt_s, *, ctx_len):
    tm = h_ref.shape[0]
    is_ctx = _is_ctx_rows(tm, ctx_len)

    @pl.when(jnp.logical_and(pl.program_id(0) == 0, pl.program_id(1) == 0))
    def _():
        cnt_s[...] = jnp.zeros(cnt_s.shape, F32)

    h = h_ref[...]
    u = (h * (1.0 + _mod_rows(mod_ref, 4, is_ctx)) + _mod_rows(mod_ref, 3, is_ctx)).astype(BF16)
    logits = jnp.dot(u, rw_ref[...], preferred_element_type=F32) + rb_ref[...]
    lane = lax.broadcasted_iota(jnp.int32, logits.shape, 1)
    m1 = jnp.max(logits, axis=-1, keepdims=True)
    i1 = jnp.min(jnp.where(logits == m1, lane, LANES), axis=-1, keepdims=True)
    rest = jnp.where(lane == i1, -jnp.inf, logits)
    m2 = jnp.max(rest, axis=-1, keepdims=True)
    i2 = jnp.min(jnp.where(rest == m2, lane, LANES), axis=-1, keepdims=True)
    e = jnp.exp(m2 - m1)
    g1 = 1.0 / (1.0 + e)
    g2 = e / (1.0 + e)
    sel1 = lane == i1
    sel2 = lane == i2
    picked = jnp.where(jnp.logical_or(sel1, sel2), 1.0, 0.0)
    before = lax.broadcasted_iota(jnp.int32, (tm, tm), 1) < lax.broadcasted_iota(jnp.int32, (tm, tm), 0)
    prefix = jnp.dot(jnp.where(before, 1.0, 0.0).astype(BF16), picked.astype(BF16), preferred_element_type=F32)
    rank = cnt_s[...] + prefix
    r1 = jnp.sum(jnp.where(sel1, rank, 0.0), axis=-1, keepdims=True)
    r2 = jnp.sum(jnp.where(sel2, rank, 0.0), axis=-1, keepdims=True)
    cnt_s[...] = cnt_s[...] + jnp.sum(picked, axis=0, keepdims=True)
    cnt_ref[...] = cnt_s[...]
    rg_ref[...] = jnp.where(lane == 0, g1, jnp.where(lane == 1, g2, 0.0))
    r1_hi = jnp.floor(r1 * (1.0 / 256.0))
    r2_hi = jnp.floor(r2 * (1.0 / 256.0))
    cols = (i1.astype(F32), i2.astype(F32), r1_hi, r1 - 256.0 * r1_hi, r2_hi, r2 - 256.0 * r2_hi)
    table = jnp.zeros(logits.shape, F32)
    for k, col in enumerate(cols):
        table = jnp.where(lane == k, col, table)
    pick = (lax.broadcasted_iota(jnp.int32, (8, LANES), 0) == lax.broadcasted_iota(jnp.int32, (8, LANES), 1))
    rows = lax.dot_general(jnp.where(pick, 1.0, 0.0).astype(BF16), table.astype(BF16),
                           (((1,), (1,)), ((), ())), preferred_element_type=F32)
    out = jnp.concatenate([rows[0:2], 256.0 * rows[2:3] + rows[3:4], 256.0 * rows[4:5] + rows[5:6],
                           jnp.zeros((4, tm), F32)], axis=0)
    ri_ref[...] = out.astype(jnp.int32)


def moe_route(h, modtab, router_w, router_b, ctx_len):
    b, t, d = h.shape
    tm = _row_tile(t)
    nt = t // tm
    kern = functools.partial(_route_kernel, ctx_len=ctx_len)
    row = lambda w: pl.BlockSpec((None, tm, w), lambda bi, i: (bi, i, 0))
    return pl.pallas_call(
        kern,
        out_shape=(jax.ShapeDtypeStruct((b * nt, 8, tm), jnp.int32), jax.ShapeDtypeStruct((b, t, LANES), F32),
                   jax.ShapeDtypeStruct((1, LANES), F32)),
        grid=(b, nt),
        in_specs=[row(d), pl.BlockSpec((None, 2, 6, d), lambda bi, i: (bi, 0, 0, 0)),
                  _const_spec((d, LANES)), _const_spec((1, LANES))],
        out_specs=(pl.BlockSpec((None, 8, tm), lambda bi, i: (bi * nt + i, 0, 0)), row(LANES),
                   pl.BlockSpec((1, LANES), lambda bi, i: (0, 0))),
        scratch_shapes=[pltpu.VMEM((1, LANES), F32)],
        compiler_params=_cparams(("arbitrary", "arbitrary")),
        name="moe_route",
    )(h, modtab, router_w, router_b)


def _row_dma_params(vmem=VMEM_LIMIT):
    return pltpu.CompilerParams(dimension_semantics=("arbitrary", "arbitrary"), vmem_limit_bytes=vmem,
                                disable_bounds_checks=True)


def _dispatch_kernel(s1_ref, s2_ref, ends_ref, h_ref, mod_ref, xs_out, w_s, z_s, sem, zsem, *, ctx_len):
    tm = h_ref.shape[0]
    g = pl.program_id(0) * pl.num_programs(1) + pl.program_id(1)
    last = pl.num_programs(0) * pl.num_programs(1) - 1
    cur = g % 2

    @pl.when(g == 0)
    def _():
        z_s[...] = jnp.zeros(z_s.shape, F32)
        n_experts = ends_ref.shape[0]

        def zero_tile(tile):
            row0 = pl.multiple_of(tile * MOE_TILE, MOE_TILE)
            return pltpu.make_async_copy(z_s, xs_out.at[pl.ds(row0, MOE_TILE), :], zsem)

        for e in range(n_experts):
            zero_tile(jnp.maximum(ends_ref[e] - 1, 0)).start()
        for e in range(n_experts):
            zero_tile(0).wait()

        def zero_unused(tile, carry):
            zero_tile(tile).start()
            zero_tile(tile).wait()
            return carry

        lax.fori_loop(ends_ref[n_experts - 1], xs_out.shape[0] // MOE_TILE, zero_unused, 0)

    def drain(buf):
        for k in range(TOP_K):
            pltpu.make_async_copy(w_s.at[buf], xs_out.at[pl.ds(0, tm), :], sem.at[buf, k]).wait()

    @pl.when(g >= 2)
    def _():
        drain(cur)

    is_ctx = _is_ctx_rows(tm, ctx_len)
    h = h_ref[...]
    w_s[cur] = h * (1.0 + _mod_rows(mod_ref, 4, is_ctx)) + _mod_rows(mod_ref, 3, is_ctx)

    def row_copy(r, slots, k):
        return pltpu.make_async_copy(w_s.at[cur, pl.ds(r, 1), :], xs_out.at[pl.ds(slots[0, r], 1), :],
                                     sem.at[cur, k])

    def issue(r, carry):
        row_copy(r, s1_ref, 0).start()
        row_copy(r, s2_ref, 1).start(priority=1)
        return carry

    lax.fori_loop(0, tm, issue, 0, unroll=8)

    @pl.when(g == last)
    def _():
        @pl.when(g >= 1)
        def _():
            drain(1 - cur)
        drain(cur)


def moe_dispatch(h, modtab, slot1, slot2, ends, n_slots, ctx_len):
    b, t, d = h.shape
    tm = _row_tile(t)
    nt = t // tm
    kern = functools.partial(_dispatch_kernel, ctx_len=ctx_len)
    slot_spec = pl.BlockSpec((None, 1, tm), lambda bi, i: (bi * nt + i, 0, 0), memory_space=pltpu.SMEM)
    return pl.pallas_call(
        kern,
        out_shape=jax.ShapeDtypeStruct((n_slots, d), F32),
        grid=(b, nt),
        in_specs=[slot_spec, slot_spec, pl.BlockSpec(memory_space=pltpu.SMEM),
                  pl.BlockSpec((None, tm, d), lambda bi, i: (bi, i, 0)),
                  pl.BlockSpec((None, 2, 6, d), lambda bi, i: (bi, 0, 0, 0))],
        out_specs=pl.BlockSpec(memory_space=pl.ANY),
        scratch_shapes=[pltpu.VMEM((2, tm, d), F32), pltpu.VMEM((MOE_TILE, d), F32),
                        pltpu.SemaphoreType.DMA((2, TOP_K)), pltpu.SemaphoreType.DMA(())],
        compiler_params=_row_dma_params(),
        name="moe_dispatch",
    )(slot1.reshape(b * nt, 1, tm), slot2.reshape(b * nt, 1, tm), ends, h, modtab)


def _gffn_kernel(te_ref, na_ref, xs_ref, w1_ref, w3_ref, w2_ref, ys_ref, g_s):
    del te_ref
    i = pl.program_id(0)

    @pl.when(i < na_ref[0])
    def _():
        _swiglu_hidden(xs_ref[...].astype(BF16), w1_ref, w3_ref, g_s)
        ys_ref[...] = jnp.dot(g_s[...], w2_ref[...], preferred_element_type=F32)

    @pl.when(i >= na_ref[0])
    def _():
        ys_ref[...] = jnp.zeros(ys_ref.shape, ys_ref.dtype)


def moe_grouped_ffn(xs, tile_expert, n_active, w1, w3, w2, j):
    n_slots, dh = xs.shape
    _, _, d, ff = w1.shape
    tm = MOE_TILE
    n_tiles = n_slots // tm
    return pl.pallas_call(
        _gffn_kernel,
        out_shape=jax.ShapeDtypeStruct((n_slots, dh), F32),
        grid_spec=pltpu.PrefetchScalarGridSpec(
            num_scalar_prefetch=2,
            grid=(n_tiles,),
            in_specs=[pl.BlockSpec((tm, dh), lambda i, te, na: (jnp.minimum(i, na[0] - 1), 0)),
                      pl.BlockSpec((None, None, d, ff), lambda i, te, na: (j, te[i], 0, 0)),
                      pl.BlockSpec((None, None, d, ff), lambda i, te, na: (j, te[i], 0, 0)),
                      pl.BlockSpec((None, None, ff, d), lambda i, te, na: (j, te[i], 0, 0))],
            out_specs=pl.BlockSpec((tm, dh), lambda i, te, na: (i, 0)),
            scratch_shapes=[pltpu.VMEM((tm, ff), BF16)]),
        compiler_params=_cparams(("arbitrary",)),
        name="moe_grouped_ffn",
    )(tile_expert, n_active, xs, w1, w3, w2)


def _combine_kernel(s1_ref, s2_ref, n1_ref, n2_ref, ys_hbm, h_ref, mod_ref, rg_ref, g_ref, b_ref, o_ref, y_s, sem,
                    *, alpha, ctx_len, tile0):
    tm = h_ref.shape[0]
    g = pl.program_id(0) * pl.num_programs(1) + pl.program_id(1)
    last = pl.num_programs(0) * pl.num_programs(1) - 1
    cur = g % 2

    def gather(slots1, slots2, buf):
        def row_copy(r, slots, k):
            return pltpu.make_async_copy(ys_hbm.at[pl.ds(slots[0, r], 1), :], y_s.at[buf, k, pl.ds(r, 1), :],
                                         sem.at[buf, k])

        def issue(r, carry):
            row_copy(r, slots1, 0).start()
            row_copy(r, slots2, 1).start(priority=1)
            return carry

        lax.fori_loop(0, tm, issue, 0, unroll=8)

    @pl.when(g == 0)
    def _():
        gather(s1_ref, s2_ref, cur)

    @pl.when(g < last)
    def _():
        gather(n1_ref, n2_ref, 1 - cur)

    for k in range(TOP_K):
        pltpu.make_async_copy(ys_hbm.at[pl.ds(0, tm), :], y_s.at[cur, k], sem.at[cur, k]).wait()

    is_ctx = ((pl.program_id(1) + tile0) * tm + lax.broadcasted_iota(jnp.int32, (tm, 1), 0)) < ctx_len
    f = rg_ref[:, 0:1] * y_s[cur, 0] + rg_ref[:, 1:2] * y_s[cur, 1]
    o_ref[...] = _deepnorm_ln(h_ref[...], f, _mod_rows(mod_ref, 5, is_ctx), g_ref[...], b_ref[...], alpha)


def moe_combine(ys, slot1, slot2, gates, h, modtab, ln_g, ln_b, alpha, ctx_len, tm, row0):
    b, t, d = h.shape
    assert row0 % tm == 0 and (t - row0) % tm == 0
    tile0 = row0 // tm
    n = (t - row0) // tm
    steps = b * n
    kern = functools.partial(_combine_kernel, alpha=alpha, ctx_len=ctx_len, tile0=tile0)
    cur_spec = pl.BlockSpec((None, 1, tm), lambda bi, i: (bi * n + i, 0, 0), memory_space=pltpu.SMEM)
    nxt_spec = pl.BlockSpec((None, 1, tm), lambda bi, i: (jnp.minimum(bi * n + i + 1, steps - 1), 0, 0),
                            memory_space=pltpu.SMEM)
    row = lambda w: pl.BlockSpec((None, tm, w), lambda bi, i: (bi, i + tile0, 0))
    s1 = slot1[:, row0:].reshape(steps, 1, tm)
    s2 = slot2[:, row0:].reshape(steps, 1, tm)
    return pl.pallas_call(
        kern,
        out_shape=jax.ShapeDtypeStruct((b, t - row0, d), F32),
        grid=(b, n),
        in_specs=[cur_spec, cur_spec, nxt_spec, nxt_spec, pl.BlockSpec(memory_space=pl.ANY), row(d),
                  pl.BlockSpec((None, 2, 6, d), lambda bi, i: (bi, 0, 0, 0)), row(LANES),
                  _const_spec((1, d)), _const_spec((1, d))],
        out_specs=pl.BlockSpec((None, tm, d), lambda bi, i: (bi, i, 0)),
        scratch_shapes=[pltpu.VMEM((2, TOP_K, tm, d), F32), pltpu.SemaphoreType.DMA((2, TOP_K))],
        compiler_params=_row_dma_params(),
        name="moe_combine",
    )(s1, s2, s1, s2, ys, h, modtab, gates, ln_g, ln_b)


def moe_ffn(h, modtab, router_w, router_b, w1, w3, w2, j, ln_g, ln_b, alpha, ctx_len, latent_only):
    b, t, d = h.shape
    n_experts = w1.shape[1]
    n_tiles = -(-(TOP_K * b * t) // MOE_TILE) + n_experts
    ri, gates, cnt = moe_route(h, modtab, router_w, router_b, ctx_len)
    counts = cnt[0, :n_experts].astype(jnp.int32)
    tiles_e = (counts + MOE_TILE - 1) // MOE_TILE
    ends = jnp.cumsum(tiles_e)
    offs = (ends - tiles_e) * MOE_TILE
    n_active = ends[-1:]
    tile_ids = jnp.minimum(jnp.arange(n_tiles, dtype=jnp.int32), n_active - 1)
    tile_expert = jnp.sum((ends[None, :] <= tile_ids[:, None]).astype(jnp.int32), axis=1)

    def slots(e, r):
        off = jnp.zeros_like(r)
        for k in range(n_experts):
            off = jnp.where(e == k, offs[k], off)
        return (off + r).reshape(b, t)

    slot1 = slots(ri[:, ROUTE_E1, :], ri[:, ROUTE_R1, :])
    slot2 = slots(ri[:, ROUTE_E2, :], ri[:, ROUTE_R2, :])
    xs = moe_dispatch(h, modtab, slot1, slot2, ends.astype(jnp.int32), n_tiles * MOE_TILE, ctx_len)
    ys = moe_grouped_ffn(xs, tile_expert, n_active.astype(jnp.int32), w1, w3, w2, j)
    tm, row0 = (TOK_TILE, ctx_len) if latent_only else (_row_tile(t), 0)
    return moe_combine(ys, slot1, slot2, gates, h, modtab, ln_g, ln_b, alpha, ctx_len, tm, row0)


def _rope_table(rows, ctx_len):
    def table(rot_dim):
        n_freq = rot_dim // 4
        inv = ROPE_BASE ** (-jnp.arange(n_freq, dtype=F32) / n_freq)
        row = jnp.repeat(jnp.arange(rows, dtype=F32), GRID_W)
        col = jnp.tile(jnp.arange(GRID_W, dtype=F32), rows)
        ang = jnp.concatenate([row[:, None] * inv, col[:, None] * inv], -1)
        cos, sin = jnp.cos(ang), jnp.sin(ang)
        reps = LANES // rot_dim
        cos_t = jnp.tile(jnp.concatenate([cos, cos], -1), (1, reps))
        sin_t = jnp.tile(jnp.concatenate([-sin, sin], -1), (1, reps))
        ident = (jnp.ones((ctx_len, LANES), F32), jnp.zeros((ctx_len, LANES), F32))
        return jnp.concatenate([ident[0], cos_t], 0), jnp.concatenate([ident[1], sin_t], 0)
    c64, s64 = table(HEAD_DIM)
    c32, s32 = table(DIFF_QK)
    return jnp.concatenate([c64, s64, c32, s32], axis=-1)


def _gate_dense(gate_w):
    cols = []
    for d in range(2):
        for g in range(2):
            cols.append(jax.scipy.linalg.block_diag(*[gate_w[d, g, k] for k in range(LRU_BLOCKS)]))
    return jnp.concatenate(cols, axis=1)


def kernel(x, c, ctx, c_ctx, ada_w, ada_b, w_in, w_out, lru_conv_w, lru_conv_b, lru_gate_w, lru_gate_b, lru_lam,
           swa_sink, diff_lam, diff_norm_g, ln_g, ln_b, ffn_w1, ffn_w3, ffn_w2, moe_router_w, moe_router_b,
           moe_w1, moe_w3, moe_w2):
    b, s, d = x.shape
    ctx_len = ctx.shape[1]
    depth = ada_w.shape[0]
    n_experts = moe_router_w.shape[-1]
    alpha = (2.0 * depth) ** 0.25
    assert s % GRID_W == 0 and s % TOK_TILE == 0 and ctx_len == TOK_TILE

    h = jnp.concatenate([ctx, x], axis=1)
    ropetab = _rope_table(s // GRID_W, ctx_len)

    rows = 8 * ((b + 1 + 7) // 8)
    cvec = jnp.zeros((rows, d), F32).at[:b].set(c).at[b].set(c_ctx)
    mods = ada_modulation(cvec, ada_w, ada_b).reshape(depth, rows, 6, d)
    modtab = jnp.stack([jnp.broadcast_to(mods[:, b:b + 1], (depth, b, 6, d)), mods[:, :b]], axis=2)

    sink_rows = jnp.repeat((swa_sink * math.log2(math.e)).reshape(depth, SWA_KV_HEADS, 1, SWA_GROUP), ATTN_BLOCK,
                           axis=-1)

    w_in, w_out, ffn_w1, ffn_w3, ffn_w2, moe_w1, moe_w3, moe_w2 = (
        w.astype(BF16) for w in (w_in, w_out, ffn_w1, ffn_w3, ffn_w2, moe_w1, moe_w3, moe_w2))

    for layer in range(depth):
        lam_init = 0.8 - 0.6 * math.exp(-0.3 * layer)
        mt = modtab[layer]
        lru_xg, attn_t = input_projection(h, mt, w_in, layer, ropetab)
        lru = rglru_mix(lru_xg, lru_conv_w[layer], lru_conv_b[layer].reshape(1, LRU_WIDTH),
                        _gate_dense(lru_gate_w[layer]).astype(BF16), lru_gate_b[layer].reshape(1, 4 * LRU_WIDTH),
                        lru_lam[layer], ctx_len)
        swa = windowed_attention(attn_t, sink_rows[layer])
        dif = differential_attention(attn_t, diff_lam[layer], diff_norm_g[layer], lam_init, layer < depth - 1)
        h = output_projection(lru, swa, dif, h, mt, w_out, layer, ln_g[layer, 0:1], ln_b[layer, 0:1], alpha, ctx_len)
        j = layer // 2
        if layer % 2 == 0:
            h = dense_ffn(h, mt, ffn_w1, ffn_w3, ffn_w2, j, ln_g[layer, 1:2], ln_b[layer, 1:2], alpha, ctx_len)
        else:
            rw = jnp.zeros((d, LANES), F32).at[:, :n_experts].set(moe_router_w[j]).astype(BF16)
            rb = jnp.full((1, LANES), NEG_INF, F32).at[0, :n_experts].set(moe_router_b[j])
            h = moe_ffn(h, mt, rw, rb, moe_w1, moe_w3, moe_w2, j, ln_g[layer, 1:2], ln_b[layer, 1:2], alpha, ctx_len,
                        latent_only=layer == depth - 1)
    return h if depth % 2 == 0 else h[:, ctx_len:, :]
```

```python
import functools
import math

import jax
import jax.numpy as jnp
from jax import lax
from jax.experimental import pallas as pl
from jax.experimental.pallas import tpu as pltpu

F32 = jnp.float32
BF16 = jnp.bfloat16

GRID_W = 64
HEAD_DIM = 64
LRU_WIDTH = 256
LRU_BLOCKS = 4
LRU_BLOCK = LRU_WIDTH // LRU_BLOCKS
CONV_W = 4
LRU_C = 8.0
SWA_Q_HEADS = 8
SWA_KV_HEADS = 2
SWA_GROUP = SWA_Q_HEADS // SWA_KV_HEADS
SWA_WIDTH = SWA_Q_HEADS * HEAD_DIM
ATTN_BLOCK = 128
DIFF_HEADS = 4
DIFF_QK = HEAD_DIM // 2
DIFF_WIDTH = DIFF_HEADS * HEAD_DIM
MIX_WIDTH = LRU_WIDTH + SWA_WIDTH + DIFF_WIDTH
TOP_K = 2
ROPE_BASE = 10000.0
LN_EPS = 1e-5
NEG_INF = -1e30
ATTN_MIN_SUM_LOG2 = -100.0
COL_LRU_X, COL_LRU_G, COL_SWA_Q, COL_SWA_K, COL_SWA_V, COL_DIFF_Q, COL_DIFF_K, COL_DIFF_V, IN_COLS = (
    0, 256, 512, 1024, 1152, 1280, 1536, 1792, 2048)
ATTN_COLS = IN_COLS - COL_SWA_Q
ROW_SWA_Q, ROW_SWA_KV, ROW_DIFF_Q, ROW_DIFF_K, ROW_DIFF_V = 0, 512, 768, 1024, 1280

LANES = 128
TOK_TILE = 256
VMEM_LIMIT = 56 * 1024 * 1024


def _cparams(sem, vmem=VMEM_LIMIT):
    return pltpu.CompilerParams(dimension_semantics=sem, vmem_limit_bytes=vmem)


def _const_spec(shape):
    nd = len(shape)
    return pl.BlockSpec(shape, lambda *_: (0,) * nd, pipeline_mode=pl.Buffered(1))


def _layer_spec(shape, layer):
    nd = len(shape)
    return pl.BlockSpec((None,) + tuple(shape), lambda *_: (layer,) + (0,) * nd, pipeline_mode=pl.Buffered(1))


def _ada_kernel(c_ref, w_ref, b_ref, o_ref):
    c = c_ref[...]
    s = (c * jax.nn.sigmoid(c)).astype(BF16)
    o_ref[...] = jnp.dot(s, w_ref[...].astype(BF16), preferred_element_type=F32) + b_ref[...]


def ada_modulation(cvec, ada_w, ada_b):
    depth, d, n = ada_w.shape
    r = cvec.shape[0]
    tn = 1536
    assert n % tn == 0
    return pl.pallas_call(
        _ada_kernel,
        out_shape=jax.ShapeDtypeStruct((depth, r, n), F32),
        grid=(depth, n // tn),
        in_specs=[pl.BlockSpec((r, d), lambda l, j: (0, 0)),
                  pl.BlockSpec((None, d, tn), lambda l, j: (l, 0, j)),
                  pl.BlockSpec((None, 1, tn), lambda l, j: (l, 0, j))],
        out_specs=pl.BlockSpec((None, r, tn), lambda l, j: (l, 0, j)),
        compiler_params=_cparams(("arbitrary", "arbitrary")),
        name="ada_modulation",
    )(cvec, ada_w, ada_b.reshape(depth, 1, n))


def _rope_group(x, cos, sin_signed, half, lane_mod):
    swapped = jnp.where(lane_mod < half, pltpu.roll(x, LANES - half, axis=1), pltpu.roll(x, half, axis=1))
    return x * cos + swapped * sin_signed


INPROJ_GROUP = 512


def _inproj_kernel(h_ref, mod_ref, w_ref, rope_ref, lru_ref, attn_ref, *, swa_scale, diff_scale):
    h = h_ref[...]
    u = (h * (1.0 + mod_ref[1:2, :]) + mod_ref[0:1, :]).astype(BF16)
    lane = lax.broadcasted_iota(jnp.int32, (1, LANES), 1)
    cos64, sin64 = rope_ref[:, 0:LANES], rope_ref[:, LANES:2 * LANES]
    cos32, sin32 = rope_ref[:, 2 * LANES:3 * LANES], rope_ref[:, 3 * LANES:4 * LANES]
    mod64, mod32 = lane % HEAD_DIM, lane % DIFF_QK
    for g0 in list(range(COL_SWA_Q, IN_COLS, INPROJ_GROUP)) + list(range(0, COL_SWA_Q, INPROJ_GROUP)):
        acc = jnp.dot(u, w_ref[:, g0:g0 + INPROJ_GROUP], preferred_element_type=F32)
        if g0 < COL_SWA_Q:
            lru_ref[:, g0:g0 + INPROJ_GROUP] = acc
            continue
        for c0 in range(g0, g0 + INPROJ_GROUP, LANES):
            x = acc[:, c0 - g0:c0 - g0 + LANES]
            if c0 < COL_SWA_V:
                x = _rope_group(x, cos64, sin64, HEAD_DIM // 2, mod64)
                if c0 < COL_SWA_K:
                    x = x * swa_scale
            elif COL_DIFF_Q <= c0 < COL_DIFF_V:
                x = _rope_group(x, cos32, sin32, DIFF_QK // 2, mod32)
                if c0 < COL_DIFF_K:
                    x = x * diff_scale
            r0 = c0 - COL_SWA_Q
            attn_ref[r0:r0 + LANES, :] = x.astype(BF16).T


def input_projection(h, modtab, w_in, layer, ropetab):
    b, t, d = h.shape
    tm = TOK_TILE
    nt = t // tm
    log2e = math.log2(math.e)
    kern = functools.partial(_inproj_kernel, swa_scale=HEAD_DIM ** -0.5 * log2e, diff_scale=DIFF_QK ** -0.5 * log2e)
    return pl.pallas_call(
        kern,
        out_shape=(jax.ShapeDtypeStruct((b, t, COL_SWA_Q), F32),
                   jax.ShapeDtypeStruct((b, nt, ATTN_COLS, tm), BF16)),
        grid=(b, nt),
        in_specs=[pl.BlockSpec((None, tm, d), lambda bi, i: (bi, i, 0)),
                  pl.BlockSpec((None, None, 6, d), lambda bi, i: (bi, jnp.minimum(i, 1), 0, 0)),
                  _layer_spec((d, IN_COLS), layer),
                  pl.BlockSpec((tm, 4 * LANES), lambda bi, i: (i, 0))],
        out_specs=(pl.BlockSpec((None, tm, COL_SWA_Q), lambda bi, i: (bi, i, 0)),
                   pl.BlockSpec((None, None, ATTN_COLS, tm), lambda bi, i: (bi, i, 0, 0))),
        compiler_params=_cparams(("arbitrary", "arbitrary")),
        name="input_projection",
    )(h, modtab, w_in, ropetab)


def _neg_expm1(z):
    series = z * (1.0 + z * (1 / 2 + z * (1 / 6 + z * (1 / 24 + z * (1 / 120 + z * (1 / 720))))))
    return jnp.where(z > -1.0 / 32, -series, 1.0 - jnp.exp(z))


def _lru_kernel(xg_ref, cw_ref, cb_ref, wg_ref, gb_ref, lam_ref, o_ref, a_s, b_s, h_s, *, chunk, ctx_chunks):
    t, c = o_ref.shape
    nchunk = t // chunk
    row = lax.broadcasted_iota(jnp.int32, (chunk, 1), 0)
    sp = jax.nn.softplus(-lam_ref[...])
    cw = cw_ref[...]
    cb = cb_ref[...]

    def coeffs(j, carry):
        r0 = pl.multiple_of(j * chunk, chunk)
        x = xg_ref[pl.ds(r0, chunk), 0:c]
        prev8 = xg_ref[pl.ds(pl.multiple_of(jnp.maximum(r0 - 8, 0), 8), 8), 0:c]
        next8 = xg_ref[pl.ds(pl.multiple_of(jnp.minimum(r0 + chunk, t - 8), 8), 8), 0:c]
        seg_start = jnp.logical_or(j == 0, j == ctx_chunks)
        seg_end = jnp.logical_or(j == ctx_chunks - 1, j == nchunk - 1)
        pm1 = jnp.where(seg_start, 0.0, prev8[7:8, :])
        np0 = jnp.where(seg_end, 0.0, next8[0:1, :])
        np1 = jnp.where(seg_end, 0.0, next8[1:2, :])
        xm1 = jnp.where(row == 0, pm1, pltpu.roll(x, 1, axis=0))
        xp1 = jnp.where(row == chunk - 1, np0, pltpu.roll(x, chunk - 1, axis=0))
        xp2 = jnp.where(row == chunk - 1, np1, jnp.where(row == chunk - 2, np0, pltpu.roll(x, chunk - 2, axis=0)))
        u = cw[0:1, :] * xm1 + cw[1:2, :] * x + cw[2:3, :] * xp1 + cw[3:4, :] * xp2 + cb
        g = jnp.dot(u.astype(BF16), wg_ref[...], preferred_element_type=F32) + gb_ref[...]
        for d in range(2):
            r = jax.nn.sigmoid(g[:, (2 * d) * c:(2 * d + 1) * c])
            i = jax.nn.sigmoid(g[:, (2 * d + 1) * c:(2 * d + 2) * c])
            log_a = (-LRU_C) * r * sp[d:d + 1, :]
            a_s[d, pl.ds(r0, chunk), :] = jnp.exp(log_a)
            b_s[d, pl.ds(r0, chunk), :] = jnp.sqrt(_neg_expm1(2.0 * log_a)) * (i * u)
        return carry

    lax.fori_loop(0, nchunk, coeffs, 0)

    sub = lax.broadcasted_iota(jnp.int32, (8, 1), 0)

    def tile_scan(a, b, carry, reverse):
        for s in (1, 2, 4):
            shift, seen = (8 - s, sub < 8 - s) if reverse else (s, sub >= s)
            a_prev = jnp.where(seen, pltpu.roll(a, shift, axis=0), 1.0)
            b_prev = jnp.where(seen, pltpu.roll(b, shift, axis=0), 0.0)
            b = a * b_prev + b
            a = a * a_prev
        h = a * carry + b
        return h, (h[0:1, :] if reverse else h[7:8, :])

    def scan_segment(lo, n8, hf, hb):
        def body(k, hh):
            hf, hb = hh
            f0 = pl.multiple_of(lo + k * 8, 8)
            r0 = pl.multiple_of(lo + (n8 - 1 - k) * 8, 8)
            fw, hf = tile_scan(a_s[0, pl.ds(f0, 8), :], b_s[0, pl.ds(f0, 8), :], hf, False)
            bw, hb = tile_scan(a_s[1, pl.ds(r0, 8), :], b_s[1, pl.ds(r0, 8), :], hb, True)
            h_s[0, pl.ds(f0, 8), :] = fw
            h_s[1, pl.ds(r0, 8), :] = bw
            return hf, hb
        return lax.fori_loop(0, n8, body, (hf, hb))

    zero = jnp.zeros((1, c), F32)
    ctx_rows = ctx_chunks * chunk
    hf, hb = scan_segment(0, ctx_rows // 8, zero, zero)
    scan_segment(ctx_rows, (t - ctx_rows) // 8, hf, hb)

    def finish(j, carry):
        r0 = pl.multiple_of(j * chunk, chunk)
        y = h_s[0, pl.ds(r0, chunk), :] + h_s[1, pl.ds(r0, chunk), :]
        gate = xg_ref[pl.ds(r0, chunk), c:2 * c]
        o_ref[pl.ds(r0, chunk), :] = (y * jax.nn.gelu(gate)).astype(o_ref.dtype)
        return carry

    lax.fori_loop(0, nchunk, finish, 0)


def rglru_mix(lru_xg, conv_w, conv_b, gate_dense, gate_b, lam, ctx_len):
    b, t, c2 = lru_xg.shape
    c = c2 // 2
    chunk = TOK_TILE
    assert t % chunk == 0 and ctx_len % chunk == 0
    kern = functools.partial(_lru_kernel, chunk=chunk, ctx_chunks=ctx_len // chunk)
    return pl.pallas_call(
        kern,
        out_shape=jax.ShapeDtypeStruct((b, t, c), BF16),
        grid=(b,),
        in_specs=[pl.BlockSpec((None, t, c2), lambda bi: (bi, 0, 0), pipeline_mode=pl.Buffered(1)),
                  _const_spec((CONV_W, c)), _const_spec((1, c)), _const_spec((c, 4 * c)), _const_spec((1, 4 * c)),
                  _const_spec((2, c))],
        out_specs=pl.BlockSpec((None, t, c), lambda bi: (bi, 0, 0)),
        scratch_shapes=[pltpu.VMEM((2, t, c), F32), pltpu.VMEM((2, t, c), F32), pltpu.VMEM((2, t, c), F32)],
        compiler_params=_cparams(("arbitrary",)),
        name="rglru_mix",
    )(lru_xg, conv_w, conv_b, gate_dense, gate_b, lam)


def _dot_t0(a, b):
    return lax.dot_general(a, b, (((0,), (0,)), ((), ())), preferred_element_type=F32)


SWA_MAX_KEY_BLOCKS = 5


def _swa_kernel(q_ref, own_ref, prev_ref, next_ref, ctx_ref, sink_ref, o_ref, s_s, p_s, l_s, ot_s):
    i = pl.program_id(1)
    nt = pl.num_programs(1)
    blk = ATTN_BLOCK
    nq = SWA_GROUP * blk
    kp = lax.broadcasted_iota(jnp.int32, (blk, nq), 0)
    qp = lax.broadcasted_iota(jnp.int32, (blk, nq), 1) % blk

    def kv(ref, hk, half):
        k = ref[hk * HEAD_DIM:(hk + 1) * HEAD_DIM, half * blk:(half + 1) * blk]
        v = ref[SWA_KV_HEADS * HEAD_DIM + hk * HEAD_DIM:SWA_KV_HEADS * HEAD_DIM + (hk + 1) * HEAD_DIM,
                half * blk:(half + 1) * blk]
        return k, v

    def q_tile(hk, half):
        return jnp.concatenate(
            [q_ref[(hk * SWA_GROUP + g) * HEAD_DIM:(hk * SWA_GROUP + g + 1) * HEAD_DIM,
                   half * blk:(half + 1) * blk] for g in range(SWA_GROUP)], axis=1)

    def put(hk, half, o):
        for g in range(SWA_GROUP):
            f0 = (hk * SWA_GROUP + g) * HEAD_DIM
            ot_s[f0:f0 + HEAD_DIM, half * blk:(half + 1) * blk] = o[:, g * blk:(g + 1) * blk]

    def run_bounded(jobs):
        ones = jnp.ones((16, SWA_MAX_KEY_BLOCKS * blk), BF16)
        for a, (hk, half, keys, _, masks) in enumerate(jobs):
            qt = q_tile(hk, half)
            qf = qt.astype(F32)
            kmax2 = jnp.zeros((1, 1), F32)
            for kt in keys:
                kf = kt.astype(F32)
                kmax2 = jnp.maximum(kmax2, jnp.max(jnp.sum(kf * kf, axis=0, keepdims=True), axis=1, keepdims=True))
            ref = jnp.maximum(jnp.sqrt(jnp.sum(qf * qf, axis=0, keepdims=True) * kmax2), sink_ref[hk])
            l_s[a:a + 1, :] = ref
            for j, (kt, msk) in enumerate(zip(keys, masks)):
                sj = _dot_t0(kt, qt)
                if msk is not None:
                    sj = jnp.where(msk, sj, NEG_INF)
                p_s[a, j * blk:(j + 1) * blk, :] = jnp.exp2(sj - ref).astype(BF16)
        worst = None
        for a, (hk, half, keys, vals, _) in enumerate(jobs):
            nk = len(keys) * blk
            v1 = jnp.concatenate([jnp.concatenate(vals, axis=1), ones[:, 0:nk]], axis=0)
            oe = jnp.dot(v1, p_s[a, 0:nk, :], preferred_element_type=F32)
            l = oe[HEAD_DIM:HEAD_DIM + 1, :] + jnp.exp2(sink_ref[hk] - l_s[a:a + 1, :])
            put(hk, half, oe[0:HEAD_DIM, :] / l)
            worst = l if worst is None else jnp.minimum(worst, l)
        return jnp.min(worst) < 2.0 ** ATTN_MIN_SUM_LOG2

    def run_exact(jobs):
        for a, (hk, half, keys, _, _) in enumerate(jobs):
            qt = q_tile(hk, half)
            for j, kt in enumerate(keys):
                s_s[a, j * blk:(j + 1) * blk, :] = _dot_t0(kt, qt)
        for a, (hk, _, keys, _, masks) in enumerate(jobs):
            sink = sink_ref[hk]
            s = []
            for j, msk in enumerate(masks):
                sj = s_s[a, j * blk:(j + 1) * blk, :]
                s.append(sj if msk is None else jnp.where(msk, sj, NEG_INF))
            m = sink
            for sj in s:
                m = jnp.maximum(m, jnp.max(sj, axis=0, keepdims=True))
            l = jnp.exp2(sink - m)
            for j, sj in enumerate(s):
                pj = jnp.exp2(sj - m)
                l = l + jnp.sum(pj, axis=0, keepdims=True)
                p_s[a, j * blk:(j + 1) * blk, :] = pj.astype(BF16)
            l_s[a:a + 1, :] = l
        for a, (hk, half, keys, vals, _) in enumerate(jobs):
            nk = len(keys) * blk
            vt = jnp.concatenate(vals, axis=1)
            put(hk, half, jnp.dot(vt, p_s[a, 0:nk, :], preferred_element_type=F32) / l_s[a:a + 1, :])

    def run(jobs):
        lost_range = run_bounded(jobs)

        @pl.when(lost_range)
        def _():
            run_exact(jobs)

    @pl.when(i == 0)
    def _():
        jobs = []
        for hk in range(SWA_KV_HEADS):
            k0, v0 = kv(ctx_ref, hk, 0)
            k1, v1 = kv(ctx_ref, hk, 1)
            for half in range(2):
                jobs.append((hk, half, [k0, k1], [v0, v1], [None, None]))
        run(jobs)

    @pl.when(i > 0)
    def _():
        has_prev = i > 1
        has_next = i < nt - 1
        m_prev = kp >= qp
        m_next = kp <= qp
        jobs = []
        for hk in range(SWA_KV_HEADS):
            c0k, c0v = kv(ctx_ref, hk, 0)
            c1k, c1v = kv(ctx_ref, hk, 1)
            o0k, o0v = kv(own_ref, hk, 0)
            o1k, o1v = kv(own_ref, hk, 1)
            pk, pv = kv(prev_ref, hk, 1)
            nk, nv = kv(next_ref, hk, 0)
            jobs.append((hk, 0, [c0k, c1k, pk, o0k, o1k], [c0v, c1v, pv, o0v, o1v],
                         [None, None, jnp.logical_and(m_prev, has_prev), None, m_next]))
            jobs.append((hk, 1, [c0k, c1k, o0k, o1k, nk], [c0v, c1v, o0v, o1v, nv],
                         [None, None, m_prev, None, jnp.logical_and(m_next, has_next)]))
        run(jobs)

    o_ref[...] = ot_s[...].T.astype(o_ref.dtype)


def windowed_attention(attn_t, sink_rows):
    b, nt, _, tm = attn_t.shape
    kvrows = 2 * SWA_KV_HEADS * HEAD_DIM
    kvblk = ROW_SWA_KV // kvrows
    assert ROW_SWA_KV % kvrows == 0 and tm == 2 * ATTN_BLOCK
    return pl.pallas_call(
        _swa_kernel,
        out_shape=jax.ShapeDtypeStruct((b, nt * tm, SWA_WIDTH), BF16),
        grid=(b, nt),
        in_specs=[pl.BlockSpec((None, None, SWA_WIDTH, tm), lambda bi, i: (bi, i, 0, 0)),
                  pl.BlockSpec((None, None, kvrows, tm), lambda bi, i: (bi, i, kvblk, 0)),
                  pl.BlockSpec((None, None, kvrows, tm), lambda bi, i: (bi, jnp.maximum(i - 1, 1), kvblk, 0)),
                  pl.BlockSpec((None, None, kvrows, tm), lambda bi, i: (bi, jnp.minimum(i + 1, nt - 1), kvblk, 0)),
                  pl.BlockSpec((None, None, kvrows, tm), lambda bi, i: (bi, 0, kvblk, 0)),
                  pl.BlockSpec((SWA_KV_HEADS, 1, SWA_GROUP * ATTN_BLOCK), lambda bi, i: (0, 0, 0))],
        out_specs=pl.BlockSpec((None, tm, SWA_WIDTH), lambda bi, i: (bi, i, 0)),
        scratch_shapes=[pltpu.VMEM((2 * SWA_KV_HEADS, SWA_MAX_KEY_BLOCKS * ATTN_BLOCK, SWA_GROUP * ATTN_BLOCK), F32),
                        pltpu.VMEM((2 * SWA_KV_HEADS, SWA_MAX_KEY_BLOCKS * ATTN_BLOCK, SWA_GROUP * ATTN_BLOCK), BF16),
                        pltpu.VMEM((2 * SWA_KV_HEADS, SWA_GROUP * ATTN_BLOCK), F32),
                        pltpu.VMEM((SWA_WIDTH, tm), F32)],
        compiler_params=_cparams(("arbitrary", "arbitrary")),
        name="windowed_attention",
    )(attn_t, attn_t, attn_t, attn_t, attn_t, sink_rows)


DIFF_ACC_ROWS = HEAD_DIM + 16
DIFF_Q_TILES = 4


def _diff_kernel(q_ref, k_ref, v_ref, lam_ref, g_ref, o_ref, m_s, a_s, acc_s, s_s, p_s, ot_s, *, lam_init):
    nkv = k_ref.shape[0]
    last = nkv - 1
    ncomp = 2 * DIFF_HEADS
    ones = jnp.ones((DIFF_ACC_ROWS - HEAD_DIM, k_ref.shape[-1]), BF16)

    def q_comp(c):
        return jnp.concatenate([q_ref[t, c * DIFF_QK:(c + 1) * DIFF_QK, :] for t in range(q_ref.shape[0])], axis=1)

    def add_values(j, slot, rescale):
        vt = v_ref[j]
        for c in range(ncomp):
            h = c // 2
            v1 = jnp.concatenate([vt[h * HEAD_DIM:(h + 1) * HEAD_DIM, :], ones], axis=0)
            old = a_s[slot, c:c + 1, :] * acc_s[c] if rescale else acc_s[c]
            acc_s[c] = old + jnp.dot(v1, p_s[slot, c], preferred_element_type=F32)

    def key_norms(j, mx):
        kf = k_ref[j].astype(F32)
        sq = kf * kf
        n2 = jnp.concatenate([jnp.sum(sq[c * DIFF_QK:(c + 1) * DIFF_QK, :], axis=0, keepdims=True)
                              for c in range(ncomp)], axis=0)
        return jnp.maximum(mx, n2)

    kmax2 = jnp.max(lax.fori_loop(0, nkv, key_norms, jnp.zeros((ncomp, k_ref.shape[-1]), F32)),
                    axis=1, keepdims=True)
    for c in range(ncomp):
        qf = q_comp(c).astype(F32)
        m_s[c:c + 1, :] = jnp.sqrt(jnp.sum(qf * qf, axis=0, keepdims=True) * kmax2[c:c + 1, :])
    acc_s[...] = jnp.zeros(acc_s.shape, F32)

    def bounded_probs(j, slot):
        kt = k_ref[j]
        for c in range(ncomp):
            s = _dot_t0(kt[c * DIFF_QK:(c + 1) * DIFF_QK, :], q_comp(c))
            p_s[slot, c] = jnp.exp2(s - m_s[c:c + 1, :]).astype(BF16)

    bounded_probs(0, 0)

    def bounded_pair(jj, carry):
        j = 2 * jj
        bounded_probs(j + 1, 1)
        add_values(j, 0, False)
        bounded_probs(j + 2, 0)
        add_values(j + 1, 1, False)
        return carry

    lax.fori_loop(0, last // 2, bounded_pair, 0)
    add_values(last, 0, False)

    min_sum = acc_s[0, HEAD_DIM:HEAD_DIM + 1, :]
    for c in range(1, ncomp):
        min_sum = jnp.minimum(min_sum, acc_s[c, HEAD_DIM:HEAD_DIM + 1, :])
    lost_range = jnp.min(min_sum) < 2.0 ** ATTN_MIN_SUM_LOG2

    def scores(j, slot):
        kt = k_ref[j]
        for c in range(ncomp):
            s_s[slot, c] = _dot_t0(kt[c * DIFF_QK:(c + 1) * DIFF_QK, :], q_comp(c))

    def softmax(slot):
        for c in range(ncomp):
            s = s_s[slot, c]
            m_old = m_s[c:c + 1, :]
            m_new = jnp.maximum(m_old, jnp.max(s, axis=0, keepdims=True))
            m_s[c:c + 1, :] = m_new
            a_s[slot, c:c + 1, :] = jnp.exp2(m_old - m_new)
            p_s[slot, c] = jnp.exp2(s - m_new).astype(BF16)

    @pl.when(lost_range)
    def _():
        m_s[...] = jnp.full(m_s.shape, NEG_INF, F32)
        acc_s[...] = jnp.zeros(acc_s.shape, F32)
        scores(0, 0)
        scores(min(1, last), 1)
        softmax(0)

        def pair(jj, carry):
            j = 2 * jj
            scores(j + 2, 0)
            softmax(1)
            add_values(j, 0, True)
            scores(jnp.minimum(j + 3, last), 1)
            softmax(0)
            add_values(j + 1, 1, True)
            return carry

        lax.fori_loop(0, last // 2, pair, 0)
        add_values(last, 0, True)

    lam_v = lam_ref[...]
    lam = (jnp.exp(jnp.sum(lam_v[0:1, :] * lam_v[1:2, :], axis=1, keepdims=True))
           - jnp.exp(jnp.sum(lam_v[2:3, :] * lam_v[3:4, :], axis=1, keepdims=True)) + lam_init)

    def head_out(c):
        return acc_s[c, 0:HEAD_DIM, :] / acc_s[c, HEAD_DIM:HEAD_DIM + 1, :]

    for h in range(DIFF_HEADS):
        o = head_out(2 * h) - lam * head_out(2 * h + 1)
        o = o * lax.rsqrt(jnp.mean(jnp.square(o), axis=0, keepdims=True) + LN_EPS)
        ot_s[h * HEAD_DIM:(h + 1) * HEAD_DIM, :] = (o * (1.0 - lam_init)) * g_ref[...]
    o_ref[...] = ot_s[...].T.astype(o_ref.dtype)


def _diff_call(q_tiles, q_feat_blk, kv, nk, nq, diff_lam, norm_g, lam_init):
    b, nqt, _, tm = q_tiles.shape
    assert nk % 2 == 1 and nqt % nq == 0
    tq = nq * tm
    ncomp = 2 * DIFF_HEADS
    kern = functools.partial(_diff_kernel, lam_init=lam_init)
    return pl.pallas_call(
        kern,
        out_shape=jax.ShapeDtypeStruct((b, nqt * tm, DIFF_WIDTH), BF16),
        grid=(b, nqt // nq),
        in_specs=[pl.BlockSpec((None, nq, DIFF_WIDTH, tm), lambda bi, i: (bi, i, q_feat_blk, 0)),
                  pl.BlockSpec((None, nk, DIFF_WIDTH, tm), lambda bi, i: (bi, 0, ROW_DIFF_K // DIFF_WIDTH, 0)),
                  pl.BlockSpec((None, nk, DIFF_WIDTH, tm), lambda bi, i: (bi, 0, ROW_DIFF_V // DIFF_WIDTH, 0)),
                  pl.BlockSpec((4, DIFF_QK), lambda bi, i: (0, 0)),
                  pl.BlockSpec((HEAD_DIM, 1), lambda bi, i: (0, 0))],
        out_specs=pl.BlockSpec((None, tq, DIFF_WIDTH), lambda bi, i: (bi, i, 0)),
        scratch_shapes=[pltpu.VMEM((ncomp, tq), F32), pltpu.VMEM((2, ncomp, tq), F32),
                        pltpu.VMEM((ncomp, DIFF_ACC_ROWS, tq), F32), pltpu.VMEM((2, ncomp, tm, tq), F32),
                        pltpu.VMEM((2, ncomp, tm, tq), BF16), pltpu.VMEM((DIFF_WIDTH, tq), F32)],
        compiler_params=_cparams(("arbitrary", "arbitrary")),
        name="differential_attention",
    )(q_tiles, kv, kv, diff_lam, norm_g.reshape(HEAD_DIM, 1))


def differential_attention(attn_t, diff_lam, norm_g, lam_init, need_ctx):
    b, nt, _, tm = attn_t.shape
    q_lat = attn_t[:, 1:, ROW_DIFF_Q:ROW_DIFF_Q + DIFF_WIDTH, :]
    lat = _diff_call(q_lat, 0, attn_t, nt, math.gcd(DIFF_Q_TILES, nt - 1), diff_lam, norm_g, lam_init)
    if need_ctx:
        ctx = _diff_call(attn_t[:, 0:1], ROW_DIFF_Q // DIFF_WIDTH, attn_t, 1, 1, diff_lam, norm_g, lam_init)
    else:
        ctx = jnp.zeros((b, tm, DIFF_WIDTH), BF16)
    return jnp.concatenate([ctx, lat], axis=1)


def _mod_rows(mod_ref, k, is_ctx):
    return jnp.where(is_ctx, mod_ref[0, k:k + 1, :], mod_ref[1, k:k + 1, :])


def _is_ctx_rows(tm, ctx_len):
    r0 = pl.program_id(1) * tm
    return (r0 + lax.broadcasted_iota(jnp.int32, (tm, 1), 0)) < ctx_len


def _deepnorm_ln(h, branch, gate, g, b, alpha):
    x = alpha * h + gate * branch
    mu = jnp.mean(x, axis=-1, keepdims=True)
    xc = x - mu
    var = jnp.mean(xc * xc, axis=-1, keepdims=True)
    return xc * lax.rsqrt(var + LN_EPS) * g + b


def _outproj_kernel(lru_ref, swa_ref, diff_ref, h_ref, mod_ref, w_ref, g_ref, b_ref, o_ref, *, alpha, ctx_len):
    tm = h_ref.shape[0]
    is_ctx = _is_ctx_rows(tm, ctx_len)
    m = jnp.dot(lru_ref[...], w_ref[0:LRU_WIDTH, :], preferred_element_type=F32)
    m = m + jnp.dot(swa_ref[...], w_ref[LRU_WIDTH:LRU_WIDTH + SWA_WIDTH, :], preferred_element_type=F32)
    m = m + jnp.dot(diff_ref[...], w_ref[LRU_WIDTH + SWA_WIDTH:MIX_WIDTH, :], preferred_element_type=F32)
    o_ref[...] = _deepnorm_ln(h_ref[...], m, _mod_rows(mod_ref, 2, is_ctx), g_ref[...], b_ref[...], alpha)


def _row_tile(t):
    for tm in range(640, 15, -16):
        if t % tm == 0:
            return tm
    raise ValueError(f"no row tile for T={t}")


def output_projection(lru, swa, diff, h, modtab, w_out, layer, ln_g, ln_b, alpha, ctx_len):
    b, t, d = h.shape
    tm = _row_tile(t)
    kern = functools.partial(_outproj_kernel, alpha=alpha, ctx_len=ctx_len)
    row = lambda w: pl.BlockSpec((None, tm, w), lambda bi, i: (bi, i, 0))
    return pl.pallas_call(
        kern,
        out_shape=jax.ShapeDtypeStruct((b, t, d), F32),
        grid=(b, t // tm),
        in_specs=[row(LRU_WIDTH), row(SWA_WIDTH), row(DIFF_WIDTH), row(d),
                  pl.BlockSpec((None, 2, 6, d), lambda bi, i: (bi, 0, 0, 0)),
                  _layer_spec((MIX_WIDTH, d), layer), _const_spec((1, d)), _const_spec((1, d))],
        out_specs=row(d),
        compiler_params=_cparams(("arbitrary", "arbitrary")),
        name="output_projection",
    )(lru, swa, diff, h, modtab, w_out, ln_g, ln_b)


FF_CHUNK = 256


def _swiglu_hidden(u, w1_ref, w3_ref, g_s, scale=None):
    ff = w1_ref.shape[-1]
    for c0 in range(0, ff, FF_CHUNK):
        h1 = jnp.dot(u, w1_ref[:, c0:c0 + FF_CHUNK], preferred_element_type=F32)
        h3 = jnp.dot(u, w3_ref[:, c0:c0 + FF_CHUNK], preferred_element_type=F32)
        g = (h1 * jax.nn.sigmoid(h1)) * h3
        if scale is not None:
            g = g * scale
        g_s[:, c0:c0 + FF_CHUNK] = g.astype(BF16)


def _ffn_kernel(h_ref, mod_ref, w1_ref, w3_ref, w2_ref, g_ref, b_ref, o_ref, g_s, *, alpha, ctx_len):
    tm = h_ref.shape[0]
    is_ctx = _is_ctx_rows(tm, ctx_len)
    h = h_ref[...]
    u = (h * (1.0 + _mod_rows(mod_ref, 4, is_ctx)) + _mod_rows(mod_ref, 3, is_ctx)).astype(BF16)
    _swiglu_hidden(u, w1_ref, w3_ref, g_s)
    f = jnp.dot(g_s[...], w2_ref[...], preferred_element_type=F32)
    o_ref[...] = _deepnorm_ln(h, f, _mod_rows(mod_ref, 5, is_ctx), g_ref[...], b_ref[...], alpha)


def dense_ffn(h, modtab, w1, w3, w2, j, ln_g, ln_b, alpha, ctx_len):
    b, t, d = h.shape
    ff = w1.shape[-1]
    assert ff % FF_CHUNK == 0
    tm = _row_tile(t)
    kern = functools.partial(_ffn_kernel, alpha=alpha, ctx_len=ctx_len)
    row = pl.BlockSpec((None, tm, d), lambda bi, i: (bi, i, 0))
    return pl.pallas_call(
        kern,
        out_shape=jax.ShapeDtypeStruct((b, t, d), F32),
        grid=(b, t // tm),
        in_specs=[row, pl.BlockSpec((None, 2, 6, d), lambda bi, i: (bi, 0, 0, 0)),
                  _layer_spec((d, ff), j), _layer_spec((d, ff), j), _layer_spec((ff, d), j),
                  _const_spec((1, d)), _const_spec((1, d))],
        out_specs=row,
        scratch_shapes=[pltpu.VMEM((tm, ff), BF16)],
        compiler_params=_cparams(("arbitrary", "arbitrary")),
        name="dense_ffn",
    )(h, modtab, w1, w3, w2, ln_g, ln_b)


MOE_TILE = 512
ROUTE_E1, ROUTE_E2, ROUTE_R1, ROUTE_R2 = 0, 1, 2, 3


def _route_kernel(h_ref, mod_ref, rw_ref, rb_ref, ri_ref, rg_ref, cnt_ref, cnt_s, *, ctx_len):
    tm = h_ref.shape[0]
    is_ctx = _is_ctx_rows(tm, ctx_len)

    @pl.when(jnp.logical_and(pl.program_id(0) == 0, pl.program_id(1) == 0))
    def _():
        cnt_s[...] = jnp.zeros(cnt_s.shape, F32)

    h = h_ref[...]
    u = (h * (1.0 + _mod_rows(mod_ref, 4, is_ctx)) + _mod_rows(mod_ref, 3, is_ctx)).astype(BF16)
    logits = jnp.dot(u, rw_ref[...], preferred_element_type=F32) + rb_ref[...]
    lane = lax.broadcasted_iota(jnp.int32, logits.shape, 1)
    m1 = jnp.max(logits, axis=-1, keepdims=True)
    i1 = jnp.min(jnp.where(logits == m1, lane, LANES), axis=-1, keepdims=True)
    rest = jnp.where(lane == i1, -jnp.inf, logits)
    m2 = jnp.max(rest, axis=-1, keepdims=True)
    i2 = jnp.min(jnp.where(rest == m2, lane, LANES), axis=-1, keepdims=True)
    e = jnp.exp(m2 - m1)
    g1 = 1.0 / (1.0 + e)
    g2 = e / (1.0 + e)
    sel1 = lane == i1
    sel2 = lane == i2
    picked = jnp.where(jnp.logical_or(sel1, sel2), 1.0, 0.0)
    before = lax.broadcasted_iota(jnp.int32, (tm, tm), 1) < lax.broadcasted_iota(jnp.int32, (tm, tm), 0)
    prefix = jnp.dot(jnp.where(before, 1.0, 0.0).astype(BF16), picked.astype(BF16), preferred_element_type=F32)
    rank = cnt_s[...] + prefix
    r1 = jnp.sum(jnp.where(sel1, rank, 0.0), axis=-1, keepdims=True)
    r2 = jnp.sum(jnp.where(sel2, rank, 0.0), axis=-1, keepdims=True)
    cnt_s[...] = cnt_s[...] + jnp.sum(picked, axis=0, keepdims=True)
    cnt_ref[...] = cnt_s[...]
    rg_ref[...] = jnp.where(lane == 0, g1, jnp.where(lane == 1, g2, 0.0))
    r1_hi = jnp.floor(r1 * (1.0 / 256.0))
    r2_hi = jnp.floor(r2 * (1.0 / 256.0))
    cols = (i1.astype(F32), i2.astype(F32), r1_hi, r1 - 256.0 * r1_hi, r2_hi, r2 - 256.0 * r2_hi)
    table = jnp.zeros(logits.shape, F32)
    for k, col in enumerate(cols):
        table = jnp.where(lane == k, col, table)
    pick = (lax.broadcasted_iota(jnp.int32, (8, LANES), 0) == lax.broadcasted_iota(jnp.int32, (8, LANES), 1))
    rows = lax.dot_general(jnp.where(pick, 1.0, 0.0).astype(BF16), table.astype(BF16),
                           (((1,), (1,)), ((), ())), preferred_element_type=F32)
    out = jnp.concatenate([rows[0:2], 256.0 * rows[2:3] + rows[3:4], 256.0 * rows[4:5] + rows[5:6],
                           jnp.zeros((4, tm), F32)], axis=0)
    ri_ref[...] = out.astype(jnp.int32)


def moe_route(h, modtab, router_w, router_b, ctx_len):
    b, t, d = h.shape
    tm = _row_tile(t)
    nt = t // tm
    kern = functools.partial(_route_kernel, ctx_len=ctx_len)
    row = lambda w: pl.BlockSpec((None, tm, w), lambda bi, i: (bi, i, 0))
    return pl.pallas_call(
        kern,
        out_shape=(jax.ShapeDtypeStruct((b * nt, 8, tm), jnp.int32), jax.ShapeDtypeStruct((b, t, LANES), F32),
                   jax.ShapeDtypeStruct((1, LANES), F32)),
        grid=(b, nt),
        in_specs=[row(d), pl.BlockSpec((None, 2, 6, d), lambda bi, i: (bi, 0, 0, 0)),
                  _const_spec((d, LANES)), _const_spec((1, LANES))],
        out_specs=(pl.BlockSpec((None, 8, tm), lambda bi, i: (bi * nt + i, 0, 0)), row(LANES),
                   pl.BlockSpec((1, LANES), lambda bi, i: (0, 0))),
        scratch_shapes=[pltpu.VMEM((1, LANES), F32)],
        compiler_params=_cparams(("arbitrary", "arbitrary")),
        name="moe_route",
    )(h, modtab, router_w, router_b)


def _row_dma_params(vmem=VMEM_LIMIT):
    return pltpu.CompilerParams(dimension_semantics=("arbitrary", "arbitrary"), vmem_limit_bytes=vmem,
                                disable_bounds_checks=True)


def _dispatch_kernel(s1_ref, s2_ref, ends_ref, h_ref, mod_ref, xs_out, w_s, z_s, sem, zsem, *, ctx_len):
    tm = h_ref.shape[0]
    g = pl.program_id(0) * pl.num_programs(1) + pl.program_id(1)
    last = pl.num_programs(0) * pl.num_programs(1) - 1
    cur = g % 2

    @pl.when(g == 0)
    def _():
        z_s[...] = jnp.zeros(z_s.shape, F32)
        n_experts = ends_ref.shape[0]

        def zero_tile(tile):
            row0 = pl.multiple_of(tile * MOE_TILE, MOE_TILE)
            return pltpu.make_async_copy(z_s, xs_out.at[pl.ds(row0, MOE_TILE), :], zsem)

        for e in range(n_experts):
            zero_tile(jnp.maximum(ends_ref[e] - 1, 0)).start()
        for e in range(n_experts):
            zero_tile(0).wait()

        def zero_unused(tile, carry):
            zero_tile(tile).start()
            zero_tile(tile).wait()
            return carry

        lax.fori_loop(ends_ref[n_experts - 1], xs_out.shape[0] // MOE_TILE, zero_unused, 0)

    def drain(buf):
        for k in range(TOP_K):
            pltpu.make_async_copy(w_s.at[buf], xs_out.at[pl.ds(0, tm), :], sem.at[buf, k]).wait()

    @pl.when(g >= 2)
    def _():
        drain(cur)

    is_ctx = _is_ctx_rows(tm, ctx_len)
    h = h_ref[...]
    w_s[cur] = h * (1.0 + _mod_rows(mod_ref, 4, is_ctx)) + _mod_rows(mod_ref, 3, is_ctx)

    def row_copy(r, slots, k):
        return pltpu.make_async_copy(w_s.at[cur, pl.ds(r, 1), :], xs_out.at[pl.ds(slots[0, r], 1), :],
                                     sem.at[cur, k])

    def issue(r, carry):
        row_copy(r, s1_ref, 0).start()
        row_copy(r, s2_ref, 1).start(priority=1)
        return carry

    lax.fori_loop(0, tm, issue, 0, unroll=8)

    @pl.when(g == last)
    def _():
        @pl.when(g >= 1)
        def _():
            drain(1 - cur)
        drain(cur)


def moe_dispatch(h, modtab, slot1, slot2, ends, n_slots, ctx_len):
    b, t, d = h.shape
    tm = _row_tile(t)
    nt = t // tm
    kern = functools.partial(_dispatch_kernel, ctx_len=ctx_len)
    slot_spec = pl.BlockSpec((None, 1, tm), lambda bi, i: (bi * nt + i, 0, 0), memory_space=pltpu.SMEM)
    return pl.pallas_call(
        kern,
        out_shape=jax.ShapeDtypeStruct((n_slots, d), F32),
        grid=(b, nt),
        in_specs=[slot_spec, slot_spec, pl.BlockSpec(memory_space=pltpu.SMEM),
                  pl.BlockSpec((None, tm, d), lambda bi, i: (bi, i, 0)),
                  pl.BlockSpec((None, 2, 6, d), lambda bi, i: (bi, 0, 0, 0))],
        out_specs=pl.BlockSpec(memory_space=pl.ANY),
        scratch_shapes=[pltpu.VMEM((2, tm, d), F32), pltpu.VMEM((MOE_TILE, d), F32),
                        pltpu.SemaphoreType.DMA((2, TOP_K)), pltpu.SemaphoreType.DMA(())],
        compiler_params=_row_dma_params(),
        name="moe_dispatch",
    )(slot1.reshape(b * nt, 1, tm), slot2.reshape(b * nt, 1, tm), ends, h, modtab)


def _gffn_kernel(te_ref, na_ref, xs_ref, w1_ref, w3_ref, w2_ref, ys_ref, g_s):
    del te_ref
    i = pl.program_id(0)

    @pl.when(i < na_ref[0])
    def _():
        _swiglu_hidden(xs_ref[...].astype(BF16), w1_ref, w3_ref, g_s)
        ys_ref[...] = jnp.dot(g_s[...], w2_ref[...], preferred_element_type=F32)

    @pl.when(i >= na_ref[0])
    def _():
        ys_ref[...] = jnp.zeros(ys_ref.shape, ys_ref.dtype)


def moe_grouped_ffn(xs, tile_expert, n_active, w1, w3, w2, j):
    n_slots, dh = xs.shape
    _, _, d, ff = w1.shape
    tm = MOE_TILE
    n_tiles = n_slots // tm
    return pl.pallas_call(
        _gffn_kernel,
        out_shape=jax.ShapeDtypeStruct((n_slots, dh), F32),
        grid_spec=pltpu.PrefetchScalarGridSpec(
            num_scalar_prefetch=2,
            grid=(n_tiles,),
            in_specs=[pl.BlockSpec((tm, dh), lambda i, te, na: (jnp.minimum(i, na[0] - 1), 0)),
                      pl.BlockSpec((None, None, d, ff), lambda i, te, na: (j, te[i], 0, 0)),
                      pl.BlockSpec((None, None, d, ff), lambda i, te, na: (j, te[i], 0, 0)),
                      pl.BlockSpec((None, None, ff, d), lambda i, te, na: (j, te[i], 0, 0))],
            out_specs=pl.BlockSpec((tm, dh), lambda i, te, na: (i, 0)),
            scratch_shapes=[pltpu.VMEM((tm, ff), BF16)]),
        compiler_params=_cparams(("arbitrary",)),
        name="moe_grouped_ffn",
    )(tile_expert, n_active, xs, w1, w3, w2)


def _combine_kernel(s1_ref, s2_ref, n1_ref, n2_ref, ys_hbm, h_ref, mod_ref, rg_ref, g_ref, b_ref, o_ref, y_s, sem,
                    *, alpha, ctx_len, tile0):
    tm = h_ref.shape[0]
    g = pl.program_id(0) * pl.num_programs(1) + pl.program_id(1)
    last = pl.num_programs(0) * pl.num_programs(1) - 1
    cur = g % 2

    def gather(slots1, slots2, buf):
        def row_copy(r, slots, k):
            return pltpu.make_async_copy(ys_hbm.at[pl.ds(slots[0, r], 1), :], y_s.at[buf, k, pl.ds(r, 1), :],
                                         sem.at[buf, k])

        def issue(r, carry):
            row_copy(r, slots1, 0).start()
            row_copy(r, slots2, 1).start(priority=1)
            return carry

        lax.fori_loop(0, tm, issue, 0, unroll=8)

    @pl.when(g == 0)
    def _():
        gather(s1_ref, s2_ref, cur)

    @pl.when(g < last)
    def _():
        gather(n1_ref, n2_ref, 1 - cur)

    for k in range(TOP_K):
        pltpu.make_async_copy(ys_hbm.at[pl.ds(0, tm), :], y_s.at[cur, k], sem.at[cur, k]).wait()

    is_ctx = ((pl.program_id(1) + tile0) * tm + lax.broadcasted_iota(jnp.int32, (tm, 1), 0)) < ctx_len
    f = rg_ref[:, 0:1] * y_s[cur, 0] + rg_ref[:, 1:2] * y_s[cur, 1]
    o_ref[...] = _deepnorm_ln(h_ref[...], f, _mod_rows(mod_ref, 5, is_ctx), g_ref[...], b_ref[...], alpha)


def moe_combine(ys, slot1, slot2, gates, h, modtab, ln_g, ln_b, alpha, ctx_len, tm, row0):
    b, t, d = h.shape
    assert row0 % tm == 0 and (t - row0) % tm == 0
    tile0 = row0 // tm
    n = (t - row0) // tm
    steps = b * n
    kern = functools.partial(_combine_kernel, alpha=alpha, ctx_len=ctx_len, tile0=tile0)
    cur_spec = pl.BlockSpec((None, 1, tm), lambda bi, i: (bi * n + i, 0, 0), memory_space=pltpu.SMEM)
    nxt_spec = pl.BlockSpec((None, 1, tm), lambda bi, i: (jnp.minimum(bi * n + i + 1, steps - 1), 0, 0),
                            memory_space=pltpu.SMEM)
    row = lambda w: pl.BlockSpec((None, tm, w), lambda bi, i: (bi, i + tile0, 0))
    s1 = slot1[:, row0:].reshape(steps, 1, tm)
    s2 = slot2[:, row0:].reshape(steps, 1, tm)
    return pl.pallas_call(
        kern,
        out_shape=jax.ShapeDtypeStruct((b, t - row0, d), F32),
        grid=(b, n),
        in_specs=[cur_spec, cur_spec, nxt_spec, nxt_spec, pl.BlockSpec(memory_space=pl.ANY), row(d),
                  pl.BlockSpec((None, 2, 6, d), lambda bi, i: (bi, 0, 0, 0)), row(LANES),
                  _const_spec((1, d)), _const_spec((1, d))],
        out_specs=pl.BlockSpec((None, tm, d), lambda bi, i: (bi, i, 0)),
        scratch_shapes=[pltpu.VMEM((2, TOP_K, tm, d), F32), pltpu.SemaphoreType.DMA((2, TOP_K))],
        compiler_params=_row_dma_params(),
        name="moe_combine",
    )(s1, s2, s1, s2, ys, h, modtab, gates, ln_g, ln_b)


def moe_ffn(h, modtab, router_w, router_b, w1, w3, w2, j, ln_g, ln_b, alpha, ctx_len, latent_only):
    b, t, d = h.shape
    n_experts = w1.shape[1]
    n_tiles = -(-(TOP_K * b * t) // MOE_TILE) + n_experts
    ri, gates, cnt = moe_route(h, modtab, router_w, router_b, ctx_len)
    counts = cnt[0, :n_experts].astype(jnp.int32)
    tiles_e = (counts + MOE_TILE - 1) // MOE_TILE
    ends = jnp.cumsum(tiles_e)
    offs = (ends - tiles_e) * MOE_TILE
    n_active = ends[-1:]
    tile_ids = jnp.minimum(jnp.arange(n_tiles, dtype=jnp.int32), n_active - 1)
    tile_expert = jnp.sum((ends[None, :] <= tile_ids[:, None]).astype(jnp.int32), axis=1)

    def slots(e, r):
        off = jnp.zeros_like(r)
        for k in range(n_experts):
            off = jnp.where(e == k, offs[k], off)
        return (off + r).reshape(b, t)

    slot1 = slots(ri[:, ROUTE_E1, :], ri[:, ROUTE_R1, :])
    slot2 = slots(ri[:, ROUTE_E2, :], ri[:, ROUTE_R2, :])
    xs = moe_dispatch(h, modtab, slot1, slot2, ends.astype(jnp.int32), n_tiles * MOE_TILE, ctx_len)
    ys = moe_grouped_ffn(xs, tile_expert, n_active.astype(jnp.int32), w1, w3, w2, j)
    tm, row0 = (TOK_TILE, ctx_len) if latent_only else (_row_tile(t), 0)
    return moe_combine(ys, slot1, slot2, gates, h, modtab, ln_g, ln_b, alpha, ctx_len, tm, row0)


def _rope_table(rows, ctx_len):
    def table(rot_dim):
        n_freq = rot_dim // 4
        inv = ROPE_BASE ** (-jnp.arange(n_freq, dtype=F32) / n_freq)
        row = jnp.repeat(jnp.arange(rows, dtype=F32), GRID_W)
        col = jnp.tile(jnp.arange(GRID_W, dtype=F32), rows)
        ang = jnp.concatenate([row[:, None] * inv, col[:, None] * inv], -1)
        cos, sin = jnp.cos(ang), jnp.sin(ang)
        reps = LANES // rot_dim
        cos_t = jnp.tile(jnp.concatenate([cos, cos], -1), (1, reps))
        sin_t = jnp.tile(jnp.concatenate([-sin, sin], -1), (1, reps))
        ident = (jnp.ones((ctx_len, LANES), F32), jnp.zeros((ctx_len, LANES), F32))
        return jnp.concatenate([ident[0], cos_t], 0), jnp.concatenate([ident[1], sin_t], 0)
    c64, s64 = table(HEAD_DIM)
    c32, s32 = table(DIFF_QK)
    return jnp.concatenate([c64, s64, c32, s32], axis=-1)


def _gate_dense(gate_w):
    cols = []
    for d in range(2):
        for g in range(2):
            cols.append(jax.scipy.linalg.block_diag(*[gate_w[d, g, k] for k in range(LRU_BLOCKS)]))
    return jnp.concatenate(cols, axis=1)


def kernel(x, c, ctx, c_ctx, ada_w, ada_b, w_in, w_out, lru_conv_w, lru_conv_b, lru_gate_w, lru_gate_b, lru_lam,
           swa_sink, diff_lam, diff_norm_g, ln_g, ln_b, ffn_w1, ffn_w3, ffn_w2, moe_router_w, moe_router_b,
           moe_w1, moe_w3, moe_w2):
    b, s, d = x.shape
    ctx_len = ctx.shape[1]
    depth = ada_w.shape[0]
    n_experts = moe_router_w.shape[-1]
    alpha = (2.0 * depth) ** 0.25
    assert s % GRID_W == 0 and s % TOK_TILE == 0 and ctx_len == TOK_TILE

    h = jnp.concatenate([ctx, x], axis=1)
    ropetab = _rope_table(s // GRID_W, ctx_len)

    rows = 8 * ((b + 1 + 7) // 8)
    cvec = jnp.zeros((rows, d), F32).at[:b].set(c).at[b].set(c_ctx)
    mods = ada_modulation(cvec, ada_w, ada_b).reshape(depth, rows, 6, d)
    modtab = jnp.stack([jnp.broadcast_to(mods[:, b:b + 1], (depth, b, 6, d)), mods[:, :b]], axis=2)

    sink_rows = jnp.repeat((swa_sink * math.log2(math.e)).reshape(depth, SWA_KV_HEADS, 1, SWA_GROUP), ATTN_BLOCK,
                           axis=-1)

    w_in, w_out, ffn_w1, ffn_w3, ffn_w2, moe_w1, moe_w3, moe_w2 = (
        w.astype(BF16) for w in (w_in, w_out, ffn_w1, ffn_w3, ffn_w2, moe_w1, moe_w3, moe_w2))

    for layer in range(depth):
        lam_init = 0.8 - 0.6 * math.exp(-0.3 * layer)
        mt = modtab[layer]
        lru_xg, attn_t = input_projection(h, mt, w_in, layer, ropetab)
        lru = rglru_mix(lru_xg, lru_conv_w[layer], lru_conv_b[layer].reshape(1, LRU_WIDTH),
                        _gate_dense(lru_gate_w[layer]).astype(BF16), lru_gate_b[layer].reshape(1, 4 * LRU_WIDTH),
                        lru_lam[layer], ctx_len)
        swa = windowed_attention(attn_t, sink_rows[layer])
        dif = differential_attention(attn_t, diff_lam[layer], diff_norm_g[layer], lam_init, layer < depth - 1)
        h = output_projection(lru, swa, dif, h, mt, w_out, layer, ln_g[layer, 0:1], ln_b[layer, 0:1], alpha, ctx_len)
        j = layer // 2
        if layer % 2 == 0:
            h = dense_ffn(h, mt, ffn_w1, ffn_w3, ffn_w2, j, ln_g[layer, 1:2], ln_b[layer, 1:2], alpha, ctx_len)
        else:
            rw = jnp.zeros((d, LANES), F32).at[:, :n_experts].set(moe_router_w[j]).astype(BF16)
            rb = jnp.full((1, LANES), NEG_INF, F32).at[0, :n_experts].set(moe_router_b[j])
            h = moe_ffn(h, mt, rw, rb, moe_w1, moe_w3, moe_w2, j, ln_g[layer, 1:2], ln_b[layer, 1:2], alpha, ctx_len,
                        latent_only=layer == depth - 1)
    return h if depth % 2 == 0 else h[:, ctx_len:, :]
```

```python
import functools
import math

import jax
import jax.numpy as jnp
from jax import lax
from jax.experimental import pallas as pl
from jax.experimental.pallas import tpu as pltpu

F32 = jnp.float32
BF16 = jnp.bfloat16

GRID_W = 64
HEAD_DIM = 64
LRU_WIDTH = 256
LRU_BLOCKS = 4
LRU_BLOCK = LRU_WIDTH // LRU_BLOCKS
CONV_W = 4
LRU_C = 8.0
SWA_Q_HEADS = 8
SWA_KV_HEADS = 2
SWA_GROUP = SWA_Q_HEADS // SWA_KV_HEADS
SWA_WIDTH = SWA_Q_HEADS * HEAD_DIM
ATTN_BLOCK = 128
DIFF_HEADS = 4
DIFF_QK = HEAD_DIM // 2
DIFF_WIDTH = DIFF_HEADS * HEAD_DIM
MIX_WIDTH = LRU_WIDTH + SWA_WIDTH + DIFF_WIDTH
TOP_K = 2
ROPE_BASE = 10000.0
LN_EPS = 1e-5
NEG_INF = -1e30
ATTN_MIN_SUM_LOG2 = -100.0
COL_LRU_X, COL_LRU_G, COL_SWA_Q, COL_SWA_K, COL_SWA_V, COL_DIFF_Q, COL_DIFF_K, COL_DIFF_V, IN_COLS = (
    0, 256, 512, 1024, 1152, 1280, 1536, 1792, 2048)
ATTN_COLS = IN_COLS - COL_SWA_Q
ROW_SWA_Q, ROW_SWA_KV, ROW_DIFF_Q, ROW_DIFF_K, ROW_DIFF_V = 0, 512, 768, 1024, 1280

LANES = 128
TOK_TILE = 256
VMEM_LIMIT = 56 * 1024 * 1024


def _cparams(sem, vmem=VMEM_LIMIT):
    return pltpu.CompilerParams(dimension_semantics=sem, vmem_limit_bytes=vmem)


def _const_spec(shape):
    nd = len(shape)
    return pl.BlockSpec(shape, lambda *_: (0,) * nd, pipeline_mode=pl.Buffered(1))


def _layer_spec(shape, layer):
    nd = len(shape)
    return pl.BlockSpec((None,) + tuple(shape), lambda *_: (layer,) + (0,) * nd, pipeline_mode=pl.Buffered(1))


def _ada_kernel(c_ref, w_ref, b_ref, o_ref):
    c = c_ref[...]
    s = (c * jax.nn.sigmoid(c)).astype(BF16)
    o_ref[...] = jnp.dot(s, w_ref[...].astype(BF16), preferred_element_type=F32) + b_ref[...]


def ada_modulation(cvec, ada_w, ada_b):
    depth, d, n = ada_w.shape
    r = cvec.shape[0]
    tn = 1536
    assert n % tn == 0
    return pl.pallas_call(
        _ada_kernel,
        out_shape=jax.ShapeDtypeStruct((depth, r, n), F32),
        grid=(depth, n // tn),
        in_specs=[pl.BlockSpec((r, d), lambda l, j: (0, 0)),
                  pl.BlockSpec((None, d, tn), lambda l, j: (l, 0, j)),
                  pl.BlockSpec((None, 1, tn), lambda l, j: (l, 0, j))],
        out_specs=pl.BlockSpec((None, r, tn), lambda l, j: (l, 0, j)),
        compiler_params=_cparams(("arbitrary", "arbitrary")),
        name="ada_modulation",
    )(cvec, ada_w, ada_b.reshape(depth, 1, n))


def _rope_group(x, cos, sin_signed, half, lane_mod):
    swapped = jnp.where(lane_mod < half, pltpu.roll(x, LANES - half, axis=1), pltpu.roll(x, half, axis=1))
    return x * cos + swapped * sin_signed


INPROJ_GROUP = 512


def _inproj_kernel(h_ref, mod_ref, w_ref, rope_ref, lru_ref, attn_ref, *, swa_scale, diff_scale):
    h = h_ref[...]
    u = (h * (1.0 + mod_ref[1:2, :]) + mod_ref[0:1, :]).astype(BF16)
    lane = lax.broadcasted_iota(jnp.int32, (1, LANES), 1)
    cos64, sin64 = rope_ref[:, 0:LANES], rope_ref[:, LANES:2 * LANES]
    cos32, sin32 = rope_ref[:, 2 * LANES:3 * LANES], rope_ref[:, 3 * LANES:4 * LANES]
    mod64, mod32 = lane % HEAD_DIM, lane % DIFF_QK
    for g0 in list(range(COL_SWA_Q, IN_COLS, INPROJ_GROUP)) + list(range(0, COL_SWA_Q, INPROJ_GROUP)):
        acc = jnp.dot(u, w_ref[:, g0:g0 + INPROJ_GROUP], preferred_element_type=F32)
        if g0 < COL_SWA_Q:
            lru_ref[:, g0:g0 + INPROJ_GROUP] = acc
            continue
        for c0 in range(g0, g0 + INPROJ_GROUP, LANES):
            x = acc[:, c0 - g0:c0 - g0 + LANES]
            if c0 < COL_SWA_V:
                x = _rope_group(x, cos64, sin64, HEAD_DIM // 2, mod64)
                if c0 < COL_SWA_K:
                    x = x * swa_scale
            elif COL_DIFF_Q <= c0 < COL_DIFF_V:
                x = _rope_group(x, cos32, sin32, DIFF_QK // 2, mod32)
                if c0 < COL_DIFF_K:
                    x = x * diff_scale
            r0 = c0 - COL_SWA_Q
            attn_ref[r0:r0 + LANES, :] = x.astype(BF16).T


def input_projection(h, modtab, w_in, layer, ropetab):
    b, t, d = h.shape
    tm = TOK_TILE
    nt = t // tm
    log2e = math.log2(math.e)
    kern = functools.partial(_inproj_kernel, swa_scale=HEAD_DIM ** -0.5 * log2e, diff_scale=DIFF_QK ** -0.5 * log2e)
    return pl.pallas_call(
        kern,
        out_shape=(jax.ShapeDtypeStruct((b, t, COL_SWA_Q), F32),
                   jax.ShapeDtypeStruct((b, nt, ATTN_COLS, tm), BF16)),
        grid=(b, nt),
        in_specs=[pl.BlockSpec((None, tm, d), lambda bi, i: (bi, i, 0)),
                  pl.BlockSpec((None, None, 6, d), lambda bi, i: (bi, jnp.minimum(i, 1), 0, 0)),
                  _layer_spec((d, IN_COLS), layer),
                  pl.BlockSpec((tm, 4 * LANES), lambda bi, i: (i, 0))],
        out_specs=(pl.BlockSpec((None, tm, COL_SWA_Q), lambda bi, i: (bi, i, 0)),
                   pl.BlockSpec((None, None, ATTN_COLS, tm), lambda bi, i: (bi, i, 0, 0))),
        compiler_params=_cparams(("arbitrary", "arbitrary")),
        name="input_projection",
    )(h, modtab, w_in, ropetab)


def _neg_expm1(z):
    series = z * (1.0 + z * (1 / 2 + z * (1 / 6 + z * (1 / 24 + z * (1 / 120 + z * (1 / 720))))))
    return jnp.where(z > -1.0 / 32, -series, 1.0 - jnp.exp(z))


def _lru_kernel(xg_ref, cw_ref, cb_ref, wg_ref, gb_ref, lam_ref, o_ref, a_s, b_s, h_s, *, chunk, ctx_chunks):
    t, c = o_ref.shape
    nchunk = t // chunk
    row = lax.broadcasted_iota(jnp.int32, (chunk, 1), 0)
    sp = jax.nn.softplus(-lam_ref[...])
    cw = cw_ref[...]
    cb = cb_ref[...]

    def coeffs(j, carry):
        r0 = pl.multiple_of(j * chunk, chunk)
        x = xg_ref[pl.ds(r0, chunk), 0:c]
        prev8 = xg_ref[pl.ds(pl.multiple_of(jnp.maximum(r0 - 8, 0), 8), 8), 0:c]
        next8 = xg_ref[pl.ds(pl.multiple_of(jnp.minimum(r0 + chunk, t - 8), 8), 8), 0:c]
        seg_start = jnp.logical_or(j == 0, j == ctx_chunks)
        seg_end = jnp.logical_or(j == ctx_chunks - 1, j == nchunk - 1)
        pm1 = jnp.where(seg_start, 0.0, prev8[7:8, :])
        np0 = jnp.where(seg_end, 0.0, next8[0:1, :])
        np1 = jnp.where(seg_end, 0.0, next8[1:2, :])
        xm1 = jnp.where(row == 0, pm1, pltpu.roll(x, 1, axis=0))
        xp1 = jnp.where(row == chunk - 1, np0, pltpu.roll(x, chunk - 1, axis=0))
        xp2 = jnp.where(row == chunk - 1, np1, jnp.where(row == chunk - 2, np0, pltpu.roll(x, chunk - 2, axis=0)))
        u = cw[0:1, :] * xm1 + cw[1:2, :] * x + cw[2:3, :] * xp1 + cw[3:4, :] * xp2 + cb
        g = jnp.dot(u.astype(BF16), wg_ref[...], preferred_element_type=F32) + gb_ref[...]
        for d in range(2):
            r = jax.nn.sigmoid(g[:, (2 * d) * c:(2 * d + 1) * c])
            i = jax.nn.sigmoid(g[:, (2 * d + 1) * c:(2 * d + 2) * c])
            log_a = (-LRU_C) * r * sp[d:d + 1, :]
            a_s[d, pl.ds(r0, chunk), :] = jnp.exp(log_a)
            b_s[d, pl.ds(r0, chunk), :] = jnp.sqrt(_neg_expm1(2.0 * log_a)) * (i * u)
        return carry

    lax.fori_loop(0, nchunk, coeffs, 0)

    sub = lax.broadcasted_iota(jnp.int32, (8, 1), 0)

    def tile_scan(a, b, carry, reverse):
        for s in (1, 2, 4):
            shift, seen = (8 - s, sub < 8 - s) if reverse else (s, sub >= s)
            a_prev = jnp.where(seen, pltpu.roll(a, shift, axis=0), 1.0)
            b_prev = jnp.where(seen, pltpu.roll(b, shift, axis=0), 0.0)
            b = a * b_prev + b
            a = a * a_prev
        h = a * carry + b
        return h, (h[0:1, :] if reverse else h[7:8, :])

    def scan_segment(lo, n8, hf, hb):
        def body(k, hh):
            hf, hb = hh
            f0 = pl.multiple_of(lo + k * 8, 8)
            r0 = pl.multiple_of(lo + (n8 - 1 - k) * 8, 8)
            fw, hf = tile_scan(a_s[0, pl.ds(f0, 8), :], b_s[0, pl.ds(f0, 8), :], hf, False)
            bw, hb = tile_scan(a_s[1, pl.ds(r0, 8), :], b_s[1, pl.ds(r0, 8), :], hb, True)
            h_s[0, pl.ds(f0, 8), :] = fw
            h_s[1, pl.ds(r0, 8), :] = bw
            return hf, hb
        return lax.fori_loop(0, n8, body, (hf, hb))

    zero = jnp.zeros((1, c), F32)
    ctx_rows = ctx_chunks * chunk
    hf, hb = scan_segment(0, ctx_rows // 8, zero, zero)
    scan_segment(ctx_rows, (t - ctx_rows) // 8, hf, hb)

    def finish(j, carry):
        r0 = pl.multiple_of(j * chunk, chunk)
        y = h_s[0, pl.ds(r0, chunk), :] + h_s[1, pl.ds(r0, chunk), :]
        gate = xg_ref[pl.ds(r0, chunk), c:2 * c]
        o_ref[pl.ds(r0, chunk), :] = (y * jax.nn.gelu(gate)).astype(o_ref.dtype)
        return carry

    lax.fori_loop(0, nchunk, finish, 0)


def rglru_mix(lru_xg, conv_w, conv_b, gate_dense, gate_b, lam, ctx_len):
    b, t, c2 = lru_xg.shape
    c = c2 // 2
    chunk = TOK_TILE
    assert t % chunk == 0 and ctx_len % chunk == 0
    kern = functools.partial(_lru_kernel, chunk=chunk, ctx_chunks=ctx_len // chunk)
    return pl.pallas_call(
        kern,
        out_shape=jax.ShapeDtypeStruct((b, t, c), BF16),
        grid=(b,),
        in_specs=[pl.BlockSpec((None, t, c2), lambda bi: (bi, 0, 0), pipeline_mode=pl.Buffered(1)),
                  _const_spec((CONV_W, c)), _const_spec((1, c)), _const_spec((c, 4 * c)), _const_spec((1, 4 * c)),
                  _const_spec((2, c))],
        out_specs=pl.BlockSpec((None, t, c), lambda bi: (bi, 0, 0)),
        scratch_shapes=[pltpu.VMEM((2, t, c), F32), pltpu.VMEM((2, t, c), F32), pltpu.VMEM((2, t, c), F32)],
        compiler_params=_cparams(("arbitrary",)),
        name="rglru_mix",
    )(lru_xg, conv_w, conv_b, gate_dense, gate_b, lam)


def _dot_t0(a, b):
    return lax.dot_general(a, b, (((0,), (0,)), ((), ())), preferred_element_type=F32)


SWA_MAX_KEY_BLOCKS = 5


def _swa_kernel(q_ref, own_ref, prev_ref, next_ref, ctx_ref, sink_ref, o_ref, s_s, p_s, l_s, ot_s):
    i = pl.program_id(1)
    nt = pl.num_programs(1)
    blk = ATTN_BLOCK
    nq = SWA_GROUP * blk
    kp = lax.broadcasted_iota(jnp.int32, (blk, nq), 0)
    qp = lax.broadcasted_iota(jnp.int32, (blk, nq), 1) % blk

    def kv(ref, hk, half):
        k = ref[hk * HEAD_DIM:(hk + 1) * HEAD_DIM, half * blk:(half + 1) * blk]
        v = ref[SWA_KV_HEADS * HEAD_DIM + hk * HEAD_DIM:SWA_KV_HEADS * HEAD_DIM + (hk + 1) * HEAD_DIM,
                half * blk:(half + 1) * blk]
        return k, v

    def q_tile(hk, half):
        return jnp.concatenate(
            [q_ref[(hk * SWA_GROUP + g) * HEAD_DIM:(hk * SWA_GROUP + g + 1) * HEAD_DIM,
                   half * blk:(half + 1) * blk] for g in range(SWA_GROUP)], axis=1)

    def put(hk, half, o):
        for g in range(SWA_GROUP):
            f0 = (hk * SWA_GROUP + g) * HEAD_DIM
            ot_s[f0:f0 + HEAD_DIM, half * blk:(half + 1) * blk] = o[:, g * blk:(g + 1) * blk]

    def run_bounded(jobs):
        ones = jnp.ones((16, SWA_MAX_KEY_BLOCKS * blk), BF16)
        for a, (hk, half, keys, _, masks) in enumerate(jobs):
            qt = q_tile(hk, half)
            qf = qt.astype(F32)
            kmax2 = jnp.zeros((1, 1), F32)
            for kt in keys:
                kf = kt.astype(F32)
                kmax2 = jnp.maximum(kmax2, jnp.max(jnp.sum(kf * kf, axis=0, keepdims=True), axis=1, keepdims=True))
            ref = jnp.maximum(jnp.sqrt(jnp.sum(qf * qf, axis=0, keepdims=True) * kmax2), sink_ref[hk])
            l_s[a:a + 1, :] = ref
            for j, (kt, msk) in enumerate(zip(keys, masks)):
                sj = _dot_t0(kt, qt)
                if msk is not None:
                    sj = jnp.where(msk, sj, NEG_INF)
                p_s[a, j * blk:(j + 1) * blk, :] = jnp.exp2(sj - ref).astype(BF16)
        worst = None
        for a, (hk, half, keys, vals, _) in enumerate(jobs):
            nk = len(keys) * blk
            v1 = jnp.concatenate([jnp.concatenate(vals, axis=1), ones[:, 0:nk]], axis=0)
            oe = jnp.dot(v1, p_s[a, 0:nk, :], preferred_element_type=F32)
            l = oe[HEAD_DIM:HEAD_DIM + 1, :] + jnp.exp2(sink_ref[hk] - l_s[a:a + 1, :])
            put(hk, half, oe[0:HEAD_DIM, :] / l)
            worst = l if worst is None else jnp.minimum(worst, l)
        return jnp.min(worst) < 2.0 ** ATTN_MIN_SUM_LOG2

    def run_exact(jobs):
        for a, (hk, half, keys, _, _) in enumerate(jobs):
            qt = q_tile(hk, half)
            for j, kt in enumerate(keys):
                s_s[a, j * blk:(j + 1) * blk, :] = _dot_t0(kt, qt)
        for a, (hk, _, keys, _, masks) in enumerate(jobs):
            sink = sink_ref[hk]
            s = []
            for j, msk in enumerate(masks):
                sj = s_s[a, j * blk:(j + 1) * blk, :]
                s.append(sj if msk is None else jnp.where(msk, sj, NEG_INF))
            m = sink
            for sj in s:
                m = jnp.maximum(m, jnp.max(sj, axis=0, keepdims=True))
            l = jnp.exp2(sink - m)
            for j, sj in enumerate(s):
                pj = jnp.exp2(sj - m)
                l = l + jnp.sum(pj, axis=0, keepdims=True)
                p_s[a, j * blk:(j + 1) * blk, :] = pj.astype(BF16)
            l_s[a:a + 1, :] = l
        for a, (hk, half, keys, vals, _) in enumerate(jobs):
            nk = len(keys) * blk
            vt = jnp.concatenate(vals, axis=1)
            put(hk, half, jnp.dot(vt, p_s[a, 0:nk, :], preferred_element_type=F32) / l_s[a:a + 1, :])

    def run(jobs):
        lost_range = run_bounded(jobs)

        @pl.when(lost_range)
        def _():
            run_exact(jobs)

    @pl.when(i == 0)
    def _():
        jobs = []
        for hk in range(SWA_KV_HEADS):
            k0, v0 = kv(ctx_ref, hk, 0)
            k1, v1 = kv(ctx_ref, hk, 1)
            for half in range(2):
                jobs.append((hk, half, [k0, k1], [v0, v1], [None, None]))
        run(jobs)

    @pl.when(i > 0)
    def _():
        has_prev = i > 1
        has_next = i < nt - 1
        m_prev = kp >= qp
        m_next = kp <= qp
        jobs = []
        for hk in range(SWA_KV_HEADS):
            c0k, c0v = kv(ctx_ref, hk, 0)
            c1k, c1v = kv(ctx_ref, hk, 1)
            o0k, o0v = kv(own_ref, hk, 0)
            o1k, o1v = kv(own_ref, hk, 1)
            pk, pv = kv(prev_ref, hk, 1)
            nk, nv = kv(next_ref, hk, 0)
            jobs.append((hk, 0, [c0k, c1k, pk, o0k, o1k], [c0v, c1v, pv, o0v, o1v],
                         [None, None, jnp.logical_and(m_prev, has_prev), None, m_next]))
            jobs.append((hk, 1, [c0k, c1k, o0k, o1k, nk], [c0v, c1v, o0v, o1v, nv],
                         [None, None, m_prev, None, jnp.logical_and(m_next, has_next)]))
        run(jobs)

    o_ref[...] = ot_s[...].T.astype(o_ref.dtype)


def windowed_attention(attn_t, sink_rows):
    b, nt, _, tm = attn_t.shape
    kvrows = 2 * SWA_KV_HEADS * HEAD_DIM
    kvblk = ROW_SWA_KV // kvrows
    assert ROW_SWA_KV % kvrows == 0 and tm == 2 * ATTN_BLOCK
    return pl.pallas_call(
        _swa_kernel,
        out_shape=jax.ShapeDtypeStruct((b, nt * tm, SWA_WIDTH), BF16),
        grid=(b, nt),
        in_specs=[pl.BlockSpec((None, None, SWA_WIDTH, tm), lambda bi, i: (bi, i, 0, 0)),
                  pl.BlockSpec((None, None, kvrows, tm), lambda bi, i: (bi, i, kvblk, 0)),
                  pl.BlockSpec((None, None, kvrows, tm), lambda bi, i: (bi, jnp.maximum(i - 1, 1), kvblk, 0)),
                  pl.BlockSpec((None, None, kvrows, tm), lambda bi, i: (bi, jnp.minimum(i + 1, nt - 1), kvblk, 0)),
                  pl.BlockSpec((None, None, kvrows, tm), lambda bi, i: (bi, 0, kvblk, 0)),
                  pl.BlockSpec((SWA_KV_HEADS, 1, SWA_GROUP * ATTN_BLOCK), lambda bi, i: (0, 0, 0))],
        out_specs=pl.BlockSpec((None, tm, SWA_WIDTH), lambda bi, i: (bi, i, 0)),
        scratch_shapes=[pltpu.VMEM((2 * SWA_KV_HEADS, SWA_MAX_KEY_BLOCKS * ATTN_BLOCK, SWA_GROUP * ATTN_BLOCK), F32),
                        pltpu.VMEM((2 * SWA_KV_HEADS, SWA_MAX_KEY_BLOCKS * ATTN_BLOCK, SWA_GROUP * ATTN_BLOCK), BF16),
                        pltpu.VMEM((2 * SWA_KV_HEADS, SWA_GROUP * ATTN_BLOCK), F32),
                        pltpu.VMEM((SWA_WIDTH, tm), F32)],
        compiler_params=_cparams(("arbitrary", "arbitrary")),
        name="windowed_attention",
    )(attn_t, attn_t, attn_t, attn_t, attn_t, sink_rows)


DIFF_ACC_ROWS = HEAD_DIM + 16
DIFF_Q_TILES = 4


def _diff_kernel(q_ref, k_ref, v_ref, lam_ref, g_ref, o_ref, m_s, a_s, acc_s, s_s, p_s, ot_s, *, lam_init):
    nkv = k_ref.shape[0]
    last = nkv - 1
    ncomp = 2 * DIFF_HEADS
    ones = jnp.ones((DIFF_ACC_ROWS - HEAD_DIM, k_ref.shape[-1]), BF16)

    def q_comp(c):
        return jnp.concatenate([q_ref[t, c * DIFF_QK:(c + 1) * DIFF_QK, :] for t in range(q_ref.shape[0])], axis=1)

    def add_values(j, slot, rescale):
        vt = v_ref[j]
        for c in range(ncomp):
            h = c // 2
            v1 = jnp.concatenate([vt[h * HEAD_DIM:(h + 1) * HEAD_DIM, :], ones], axis=0)
            old = a_s[slot, c:c + 1, :] * acc_s[c] if rescale else acc_s[c]
            acc_s[c] = old + jnp.dot(v1, p_s[slot, c], preferred_element_type=F32)

    def key_norms(j, mx):
        kf = k_ref[j].astype(F32)
        sq = kf * kf
        n2 = jnp.concatenate([jnp.sum(sq[c * DIFF_QK:(c + 1) * DIFF_QK, :], axis=0, keepdims=True)
                              for c in range(ncomp)], axis=0)
        return jnp.maximum(mx, n2)

    kmax2 = jnp.max(lax.fori_loop(0, nkv, key_norms, jnp.zeros((ncomp, k_ref.shape[-1]), F32)),
                    axis=1, keepdims=True)
    for c in range(ncomp):
        qf = q_comp(c).astype(F32)
        m_s[c:c + 1, :] = jnp.sqrt(jnp.sum(qf * qf, axis=0, keepdims=True) * kmax2[c:c + 1, :])
    acc_s[...] = jnp.zeros(acc_s.shape, F32)

    def bounded_probs(j, slot):
        kt = k_ref[j]
        for c in range(ncomp):
            s = _dot_t0(kt[c * DIFF_QK:(c + 1) * DIFF_QK, :], q_comp(c))
            p_s[slot, c] = jnp.exp2(s - m_s[c:c + 1, :]).astype(BF16)

    bounded_probs(0, 0)

    def bounded_pair(jj, carry):
        j = 2 * jj
        bounded_probs(j + 1, 1)
        add_values(j, 0, False)
        bounded_probs(j + 2, 0)
        add_values(j + 1, 1, False)
        return carry

    lax.fori_loop(0, last // 2, bounded_pair, 0)
    add_values(last, 0, False)

    min_sum = acc_s[0, HEAD_DIM:HEAD_DIM + 1, :]
    for c in range(1, ncomp):
        min_sum = jnp.minimum(min_sum, acc_s[c, HEAD_DIM:HEAD_DIM + 1, :])
    lost_range = jnp.min(min_sum) < 2.0 ** ATTN_MIN_SUM_LOG2

    def scores(j, slot):
        kt = k_ref[j]
        for c in range(ncomp):
            s_s[slot, c] = _dot_t0(kt[c * DIFF_QK:(c + 1) * DIFF_QK, :], q_comp(c))

    def softmax(slot):
        for c in range(ncomp):
            s = s_s[slot, c]
            m_old = m_s[c:c + 1, :]
            m_new = jnp.maximum(m_old, jnp.max(s, axis=0, keepdims=True))
            m_s[c:c + 1, :] = m_new
            a_s[slot, c:c + 1, :] = jnp.exp2(m_old - m_new)
            p_s[slot, c] = jnp.exp2(s - m_new).astype(BF16)

    @pl.when(lost_range)
    def _():
        m_s[...] = jnp.full(m_s.shape, NEG_INF, F32)
        acc_s[...] = jnp.zeros(acc_s.shape, F32)
        scores(0, 0)
        scores(min(1, last), 1)
        softmax(0)

        def pair(jj, carry):
            j = 2 * jj
            scores(j + 2, 0)
            softmax(1)
            add_values(j, 0, True)
            scores(jnp.minimum(j + 3, last), 1)
            softmax(0)
            add_values(j + 1, 1, True)
            return carry

        lax.fori_loop(0, last // 2, pair, 0)
        add_values(last, 0, True)

    lam_v = lam_ref[...]
    lam = (jnp.exp(jnp.sum(lam_v[0:1, :] * lam_v[1:2, :], axis=1, keepdims=True))
           - jnp.exp(jnp.sum(lam_v[2:3, :] * lam_v[3:4, :], axis=1, keepdims=True)) + lam_init)

    def head_out(c):
        return acc_s[c, 0:HEAD_DIM, :] / acc_s[c, HEAD_DIM:HEAD_DIM + 1, :]

    for h in range(DIFF_HEADS):
        o = head_out(2 * h) - lam * head_out(2 * h + 1)
        o = o * lax.rsqrt(jnp.mean(jnp.square(o), axis=0, keepdims=True) + LN_EPS)
        ot_s[h * HEAD_DIM:(h + 1) * HEAD_DIM, :] = (o * (1.0 - lam_init)) * g_ref[...]
    o_ref[...] = ot_s[...].T.astype(o_ref.dtype)


def _diff_call(q_tiles, q_feat_blk, kv, nk, nq, diff_lam, norm_g, lam_init):
    b, nqt, _, tm = q_tiles.shape
    assert nk % 2 == 1 and nqt % nq == 0
    tq = nq * tm
    ncomp = 2 * DIFF_HEADS
    kern = functools.partial(_diff_kernel, lam_init=lam_init)
    return pl.pallas_call(
        kern,
        out_shape=jax.ShapeDtypeStruct((b, nqt * tm, DIFF_WIDTH), BF16),
        grid=(b, nqt // nq),
        in_specs=[pl.BlockSpec((None, nq, DIFF_WIDTH, tm), lambda bi, i: (bi, i, q_feat_blk, 0)),
                  pl.BlockSpec((None, nk, DIFF_WIDTH, tm), lambda bi, i: (bi, 0, ROW_DIFF_K // DIFF_WIDTH, 0)),
                  pl.BlockSpec((None, nk, DIFF_WIDTH, tm), lambda bi, i: (bi, 0, ROW_DIFF_V // DIFF_WIDTH, 0)),
                  pl.BlockSpec((4, DIFF_QK), lambda bi, i: (0, 0)),
                  pl.BlockSpec((HEAD_DIM, 1), lambda bi, i: (0, 0))],
        out_specs=pl.BlockSpec((None, tq, DIFF_WIDTH), lambda bi, i: (bi, i, 0)),
        scratch_shapes=[pltpu.VMEM((ncomp, tq), F32), pltpu.VMEM((2, ncomp, tq), F32),
                        pltpu.VMEM((ncomp, DIFF_ACC_ROWS, tq), F32), pltpu.VMEM((2, ncomp, tm, tq), F32),
                        pltpu.VMEM((2, ncomp, tm, tq), BF16), pltpu.VMEM((DIFF_WIDTH, tq), F32)],
        compiler_params=_cparams(("arbitrary", "arbitrary")),
        name="differential_attention",
    )(q_tiles, kv, kv, diff_lam, norm_g.reshape(HEAD_DIM, 1))


def differential_attention(attn_t, diff_lam, norm_g, lam_init, need_ctx):
    b, nt, _, tm = attn_t.shape
    q_lat = attn_t[:, 1:, ROW_DIFF_Q:ROW_DIFF_Q + DIFF_WIDTH, :]
    lat = _diff_call(q_lat, 0, attn_t, nt, math.gcd(DIFF_Q_TILES, nt - 1), diff_lam, norm_g, lam_init)
    if need_ctx:
        ctx = _diff_call(attn_t[:, 0:1], ROW_DIFF_Q // DIFF_WIDTH, attn_t, 1, 1, diff_lam, norm_g, lam_init)
    else:
        ctx = jnp.zeros((b, tm, DIFF_WIDTH), BF16)
    return jnp.concatenate([ctx, lat], axis=1)


def _mod_rows(mod_ref, k, is_ctx):
    return jnp.where(is_ctx, mod_ref[0, k:k + 1, :], mod_ref[1, k:k + 1, :])


def _is_ctx_rows(tm, ctx_len):
    r0 = pl.program_id(1) * tm
    return (r0 + lax.broadcasted_iota(jnp.int32, (tm, 1), 0)) < ctx_len


def _deepnorm_ln(h, branch, gate, g, b, alpha):
    x = alpha * h + gate * branch
    mu = jnp.mean(x, axis=-1, keepdims=True)
    xc = x - mu
    var = jnp.mean(xc * xc, axis=-1, keepdims=True)
    return xc * lax.rsqrt(var + LN_EPS) * g + b


def _mixed_projection(lru_ref, swa_ref, diff_ref, w_ref, rows=slice(None)):
    m = jnp.dot(lru_ref[rows, :], w_ref[0:LRU_WIDTH, :], preferred_element_type=F32)
    m = m + jnp.dot(swa_ref[rows, :], w_ref[LRU_WIDTH:LRU_WIDTH + SWA_WIDTH, :], preferred_element_type=F32)
    return m + jnp.dot(diff_ref[rows, :], w_ref[LRU_WIDTH + SWA_WIDTH:MIX_WIDTH, :], preferred_element_type=F32)


def _token_mix_residual(lru_ref, swa_ref, diff_ref, h_ref, mod_ref, w_ref, g_ref, b_ref, is_ctx, alpha):
    m = _mixed_projection(lru_ref, swa_ref, diff_ref, w_ref)
    return _deepnorm_ln(h_ref[...], m, _mod_rows(mod_ref, 2, is_ctx), g_ref[...], b_ref[...], alpha)


def _row_tile(t):
    for tm in range(640, 15, -16):
        if t % tm == 0:
            return tm
    raise ValueError(f"no row tile for T={t}")


def _token_mix_specs(tm, d, layer):
    row = lambda w: pl.BlockSpec((None, tm, w), lambda bi, i: (bi, i, 0))
    return [row(LRU_WIDTH), row(SWA_WIDTH), row(DIFF_WIDTH), row(d),
            pl.BlockSpec((None, 2, 6, d), lambda bi, i: (bi, 0, 0, 0)),
            _layer_spec((MIX_WIDTH, d), layer), _const_spec((1, d)), _const_spec((1, d))]


FF_CHUNK = 256


def _swiglu_hidden(u, w1_ref, w3_ref, g_s, scale=None):
    ff = w1_ref.shape[-1]
    for c0 in range(0, ff, FF_CHUNK):
        h1 = jnp.dot(u, w1_ref[:, c0:c0 + FF_CHUNK], preferred_element_type=F32)
        h3 = jnp.dot(u, w3_ref[:, c0:c0 + FF_CHUNK], preferred_element_type=F32)
        g = (h1 * jax.nn.sigmoid(h1)) * h3
        if scale is not None:
            g = g * scale
        g_s[:, c0:c0 + FF_CHUNK] = g.astype(BF16)


def _mix_ffn_kernel(lru_ref, swa_ref, diff_ref, h_ref, mod_ref, wo_ref, g0_ref, b0_ref, w1_ref, w3_ref, w2_ref,
                    g1_ref, b1_ref, o_ref, g_s, *, alpha, ctx_len):
    tm = h_ref.shape[0]
    half = tm // 2
    ctx_rows = _is_ctx_rows(tm, ctx_len)
    rows = [slice(r * half, (r + 1) * half) for r in range(2)]
    is_ctx = [ctx_rows[rw, :] for rw in rows]
    m = [_mixed_projection(lru_ref, swa_ref, diff_ref, wo_ref, rw) for rw in rows]
    h = [None, None]
    for r in range(2):
        h[r] = _deepnorm_ln(h_ref[rows[r], :], m[r], _mod_rows(mod_ref, 2, is_ctx[r]), g0_ref[...], b0_ref[...], alpha)
        u = (h[r] * (1.0 + _mod_rows(mod_ref, 4, is_ctx[r])) + _mod_rows(mod_ref, 3, is_ctx[r])).astype(BF16)
        _swiglu_hidden(u, w1_ref, w3_ref, g_s.at[r])
    for r in range(2):
        f = jnp.dot(g_s[r], w2_ref[...], preferred_element_type=F32)
        o_ref[rows[r], :] = _deepnorm_ln(h[r], f, _mod_rows(mod_ref, 5, is_ctx[r]), g1_ref[...], b1_ref[...], alpha)


def mix_dense_ffn(lru, swa, diff, h, modtab, w_out, layer, ln0_g, ln0_b, w1, w3, w2, j, ln1_g, ln1_b, alpha, ctx_len):
    b, t, d = h.shape
    ff = w1.shape[-1]
    assert ff % FF_CHUNK == 0
    tm = _row_tile(t)
    kern = functools.partial(_mix_ffn_kernel, alpha=alpha, ctx_len=ctx_len)
    return pl.pallas_call(
        kern,
        out_shape=jax.ShapeDtypeStruct((b, t, d), F32),
        grid=(b, t // tm),
        in_specs=_token_mix_specs(tm, d, layer) + [
            _layer_spec((d, ff), j), _layer_spec((d, ff), j), _layer_spec((ff, d), j),
            _const_spec((1, d)), _const_spec((1, d))],
        out_specs=pl.BlockSpec((None, tm, d), lambda bi, i: (bi, i, 0)),
        scratch_shapes=[pltpu.VMEM((2, tm // 2, ff), BF16)],
        compiler_params=_cparams(("arbitrary", "arbitrary")),
        name="mix_dense_ffn",
    )(lru, swa, diff, h, modtab, w_out, ln0_g, ln0_b, w1, w3, w2, ln1_g, ln1_b)


MOE_TILE = 512
ROUTE_E1, ROUTE_E2, ROUTE_R1, ROUTE_R2 = 0, 1, 2, 3


def _mix_route_kernel(lru_ref, swa_ref, diff_ref, h_ref, mod_ref, wo_ref, g0_ref, b0_ref, rw_ref, rb_ref,
                      h1_ref, ri_ref, rg_ref, cnt_ref, cnt_s, *, alpha, ctx_len):
    tm = h_ref.shape[0]
    is_ctx = _is_ctx_rows(tm, ctx_len)

    @pl.when(jnp.logical_and(pl.program_id(0) == 0, pl.program_id(1) == 0))
    def _():
        cnt_s[...] = jnp.zeros(cnt_s.shape, F32)

    h = _token_mix_residual(lru_ref, swa_ref, diff_ref, h_ref, mod_ref, wo_ref, g0_ref, b0_ref, is_ctx, alpha)
    h1_ref[...] = h
    u = (h * (1.0 + _mod_rows(mod_ref, 4, is_ctx)) + _mod_rows(mod_ref, 3, is_ctx)).astype(BF16)
    logits = jnp.dot(u, rw_ref[...], preferred_element_type=F32) + rb_ref[...]
    lane = lax.broadcasted_iota(jnp.int32, logits.shape, 1)
    m1 = jnp.max(logits, axis=-1, keepdims=True)
    i1 = jnp.min(jnp.where(logits == m1, lane, LANES), axis=-1, keepdims=True)
    rest = jnp.where(lane == i1, -jnp.inf, logits)
    m2 = jnp.max(rest, axis=-1, keepdims=True)
    i2 = jnp.min(jnp.where(rest == m2, lane, LANES), axis=-1, keepdims=True)
    e = jnp.exp(m2 - m1)
    g1 = 1.0 / (1.0 + e)
    g2 = e / (1.0 + e)
    sel1 = lane == i1
    sel2 = lane == i2
    picked = jnp.where(jnp.logical_or(sel1, sel2), 1.0, 0.0)
    before = lax.broadcasted_iota(jnp.int32, (tm, tm), 1) < lax.broadcasted_iota(jnp.int32, (tm, tm), 0)
    prefix = jnp.dot(jnp.where(before, 1.0, 0.0).astype(BF16), picked.astype(BF16), preferred_element_type=F32)
    rank = cnt_s[...] + prefix
    r1 = jnp.sum(jnp.where(sel1, rank, 0.0), axis=-1, keepdims=True)
    r2 = jnp.sum(jnp.where(sel2, rank, 0.0), axis=-1, keepdims=True)
    cnt_s[...] = cnt_s[...] + jnp.sum(picked, axis=0, keepdims=True)
    cnt_ref[...] = cnt_s[...]
    rg_ref[...] = jnp.where(lane == 0, g1, jnp.where(lane == 1, g2, 0.0))
    r1_hi = jnp.floor(r1 * (1.0 / 256.0))
    r2_hi = jnp.floor(r2 * (1.0 / 256.0))
    cols = (i1.astype(F32), i2.astype(F32), r1_hi, r1 - 256.0 * r1_hi, r2_hi, r2 - 256.0 * r2_hi)
    table = jnp.zeros(logits.shape, F32)
    for k, col in enumerate(cols):
        table = jnp.where(lane == k, col, table)
    pick = (lax.broadcasted_iota(jnp.int32, (8, LANES), 0) == lax.broadcasted_iota(jnp.int32, (8, LANES), 1))
    rows = lax.dot_general(jnp.where(pick, 1.0, 0.0).astype(BF16), table.astype(BF16),
                           (((1,), (1,)), ((), ())), preferred_element_type=F32)
    out = jnp.concatenate([rows[0:2], 256.0 * rows[2:3] + rows[3:4], 256.0 * rows[4:5] + rows[5:6],
                           jnp.zeros((4, tm), F32)], axis=0)
    ri_ref[...] = out.astype(jnp.int32)


def mix_moe_route(lru, swa, diff, h, modtab, w_out, layer, ln0_g, ln0_b, router_w, router_b, alpha, ctx_len):
    b, t, d = h.shape
    tm = _row_tile(t)
    nt = t // tm
    kern = functools.partial(_mix_route_kernel, alpha=alpha, ctx_len=ctx_len)
    row = lambda w: pl.BlockSpec((None, tm, w), lambda bi, i: (bi, i, 0))
    return pl.pallas_call(
        kern,
        out_shape=(jax.ShapeDtypeStruct((b, t, d), F32), jax.ShapeDtypeStruct((b * nt, 8, tm), jnp.int32),
                   jax.ShapeDtypeStruct((b, t, LANES), F32), jax.ShapeDtypeStruct((1, LANES), F32)),
        grid=(b, nt),
        in_specs=_token_mix_specs(tm, d, layer) + [_const_spec((d, LANES)), _const_spec((1, LANES))],
        out_specs=(row(d), pl.BlockSpec((None, 8, tm), lambda bi, i: (bi * nt + i, 0, 0)), row(LANES),
                   pl.BlockSpec((1, LANES), lambda bi, i: (0, 0))),
        scratch_shapes=[pltpu.VMEM((1, LANES), F32)],
        compiler_params=_cparams(("arbitrary", "arbitrary")),
        name="mix_moe_route",
    )(lru, swa, diff, h, modtab, w_out, ln0_g, ln0_b, router_w, router_b)


def _row_dma_params(vmem=VMEM_LIMIT):
    return pltpu.CompilerParams(dimension_semantics=("arbitrary", "arbitrary"), vmem_limit_bytes=vmem,
                                disable_bounds_checks=True)


def _dispatch_kernel(s1_ref, s2_ref, ends_ref, h_ref, mod_ref, xs_out, w_s, z_s, sem, zsem, *, ctx_len):
    tm = h_ref.shape[0]
    g = pl.program_id(0) * pl.num_programs(1) + pl.program_id(1)
    last = pl.num_programs(0) * pl.num_programs(1) - 1
    cur = g % 2

    @pl.when(g == 0)
    def _():
        z_s[...] = jnp.zeros(z_s.shape, F32)
        n_experts = ends_ref.shape[0]

        def zero_tile(tile):
            row0 = pl.multiple_of(tile * MOE_TILE, MOE_TILE)
            return pltpu.make_async_copy(z_s, xs_out.at[pl.ds(row0, MOE_TILE), :], zsem)

        for e in range(n_experts):
            zero_tile(jnp.maximum(ends_ref[e] - 1, 0)).start()
        for e in range(n_experts):
            zero_tile(0).wait()

        def zero_unused(tile, carry):
            zero_tile(tile).start()
            zero_tile(tile).wait()
            return carry

        lax.fori_loop(ends_ref[n_experts - 1], xs_out.shape[0] // MOE_TILE, zero_unused, 0)

    def drain(buf):
        for k in range(TOP_K):
            pltpu.make_async_copy(w_s.at[buf], xs_out.at[pl.ds(0, tm), :], sem.at[buf, k]).wait()

    @pl.when(g >= 2)
    def _():
        drain(cur)

    is_ctx = _is_ctx_rows(tm, ctx_len)
    h = h_ref[...]
    w_s[cur] = h * (1.0 + _mod_rows(mod_ref, 4, is_ctx)) + _mod_rows(mod_ref, 3, is_ctx)

    def row_copy(r, slots, k):
        return pltpu.make_async_copy(w_s.at[cur, pl.ds(r, 1), :], xs_out.at[pl.ds(slots[0, r], 1), :],
                                     sem.at[cur, k])

    def issue(r, carry):
        row_copy(r, s1_ref, 0).start()
        row_copy(r, s2_ref, 1).start(priority=1)
        return carry

    lax.fori_loop(0, tm, issue, 0, unroll=8)

    @pl.when(g == last)
    def _():
        @pl.when(g >= 1)
        def _():
            drain(1 - cur)
        drain(cur)


def moe_dispatch(h, modtab, slot1, slot2, ends, n_slots, ctx_len):
    b, t, d = h.shape
    tm = _row_tile(t)
    nt = t // tm
    kern = functools.partial(_dispatch_kernel, ctx_len=ctx_len)
    slot_spec = pl.BlockSpec((None, 1, tm), lambda bi, i: (bi * nt + i, 0, 0), memory_space=pltpu.SMEM)
    return pl.pallas_call(
        kern,
        out_shape=jax.ShapeDtypeStruct((n_slots, d), F32),
        grid=(b, nt),
        in_specs=[slot_spec, slot_spec, pl.BlockSpec(memory_space=pltpu.SMEM),
                  pl.BlockSpec((None, tm, d), lambda bi, i: (bi, i, 0)),
                  pl.BlockSpec((None, 2, 6, d), lambda bi, i: (bi, 0, 0, 0))],
        out_specs=pl.BlockSpec(memory_space=pl.ANY),
        scratch_shapes=[pltpu.VMEM((2, tm, d), F32), pltpu.VMEM((MOE_TILE, d), F32),
                        pltpu.SemaphoreType.DMA((2, TOP_K)), pltpu.SemaphoreType.DMA(())],
        compiler_params=_row_dma_params(),
        name="moe_dispatch",
    )(slot1.reshape(b * nt, 1, tm), slot2.reshape(b * nt, 1, tm), ends, h, modtab)


def _gffn_kernel(te_ref, na_ref, xs_ref, w1_ref, w3_ref, w2_ref, ys_ref, g_s):
    del te_ref
    i = pl.program_id(0)

    @pl.when(i < na_ref[0])
    def _():
        _swiglu_hidden(xs_ref[...].astype(BF16), w1_ref, w3_ref, g_s)
        ys_ref[...] = jnp.dot(g_s[...], w2_ref[...], preferred_element_type=F32)

    @pl.when(i >= na_ref[0])
    def _():
        ys_ref[...] = jnp.zeros(ys_ref.shape, ys_ref.dtype)


def moe_grouped_ffn(xs, tile_expert, n_active, w1, w3, w2, j):
    n_slots, dh = xs.shape
    _, _, d, ff = w1.shape
    tm = MOE_TILE
    n_tiles = n_slots // tm
    return pl.pallas_call(
        _gffn_kernel,
        out_shape=jax.ShapeDtypeStruct((n_slots, dh), F32),
        grid_spec=pltpu.PrefetchScalarGridSpec(
            num_scalar_prefetch=2,
            grid=(n_tiles,),
            in_specs=[pl.BlockSpec((tm, dh), lambda i, te, na: (jnp.minimum(i, na[0] - 1), 0)),
                      pl.BlockSpec((None, None, d, ff), lambda i, te, na: (j, te[i], 0, 0)),
                      pl.BlockSpec((None, None, d, ff), lambda i, te, na: (j, te[i], 0, 0)),
                      pl.BlockSpec((None, None, ff, d), lambda i, te, na: (j, te[i], 0, 0))],
            out_specs=pl.BlockSpec((tm, dh), lambda i, te, na: (i, 0)),
            scratch_shapes=[pltpu.VMEM((tm, ff), BF16)]),
        compiler_params=_cparams(("arbitrary",)),
        name="moe_grouped_ffn",
    )(tile_expert, n_active, xs, w1, w3, w2)


def _combine_kernel(s1_ref, s2_ref, n1_ref, n2_ref, ys_hbm, h_ref, mod_ref, rg_ref, g_ref, b_ref, o_ref, y_s, sem,
                    *, alpha, ctx_len, tile0):
    tm = h_ref.shape[0]
    g = pl.program_id(0) * pl.num_programs(1) + pl.program_id(1)
    last = pl.num_programs(0) * pl.num_programs(1) - 1
    cur = g % 2

    def gather(slots1, slots2, buf):
        def row_copy(r, slots, k):
            return pltpu.make_async_copy(ys_hbm.at[pl.ds(slots[0, r], 1), :], y_s.at[buf, k, pl.ds(r, 1), :],
                                         sem.at[buf, k])

        def issue(r, carry):
            row_copy(r, slots1, 0).start()
            row_copy(r, slots2, 1).start(priority=1)
            return carry

        lax.fori_loop(0, tm, issue, 0, unroll=8)

    @pl.when(g == 0)
    def _():
        gather(s1_ref, s2_ref, cur)

    @pl.when(g < last)
    def _():
        gather(n1_ref, n2_ref, 1 - cur)

    for k in range(TOP_K):
        pltpu.make_async_copy(ys_hbm.at[pl.ds(0, tm), :], y_s.at[cur, k], sem.at[cur, k]).wait()

    is_ctx = ((pl.program_id(1) + tile0) * tm + lax.broadcasted_iota(jnp.int32, (tm, 1), 0)) < ctx_len
    f = rg_ref[:, 0:1] * y_s[cur, 0] + rg_ref[:, 1:2] * y_s[cur, 1]
    o_ref[...] = _deepnorm_ln(h_ref[...], f, _mod_rows(mod_ref, 5, is_ctx), g_ref[...], b_ref[...], alpha)


def moe_combine(ys, slot1, slot2, gates, h, modtab, ln_g, ln_b, alpha, ctx_len, tm, row0):
    b, t, d = h.shape
    assert row0 % tm == 0 and (t - row0) % tm == 0
    tile0 = row0 // tm
    n = (t - row0) // tm
    steps = b * n
    kern = functools.partial(_combine_kernel, alpha=alpha, ctx_len=ctx_len, tile0=tile0)
    cur_spec = pl.BlockSpec((None, 1, tm), lambda bi, i: (bi * n + i, 0, 0), memory_space=pltpu.SMEM)
    nxt_spec = pl.BlockSpec((None, 1, tm), lambda bi, i: (jnp.minimum(bi * n + i + 1, steps - 1), 0, 0),
                            memory_space=pltpu.SMEM)
    row = lambda w: pl.BlockSpec((None, tm, w), lambda bi, i: (bi, i + tile0, 0))
    s1 = slot1[:, row0:].reshape(steps, 1, tm)
    s2 = slot2[:, row0:].reshape(steps, 1, tm)
    return pl.pallas_call(
        kern,
        out_shape=jax.ShapeDtypeStruct((b, t - row0, d), F32),
        grid=(b, n),
        in_specs=[cur_spec, cur_spec, nxt_spec, nxt_spec, pl.BlockSpec(memory_space=pl.ANY), row(d),
                  pl.BlockSpec((None, 2, 6, d), lambda bi, i: (bi, 0, 0, 0)), row(LANES),
                  _const_spec((1, d)), _const_spec((1, d))],
        out_specs=pl.BlockSpec((None, tm, d), lambda bi, i: (bi, i, 0)),
        scratch_shapes=[pltpu.VMEM((2, TOP_K, tm, d), F32), pltpu.SemaphoreType.DMA((2, TOP_K))],
        compiler_params=_row_dma_params(),
        name="moe_combine",
    )(s1, s2, s1, s2, ys, h, modtab, gates, ln_g, ln_b)


def mix_moe_ffn(mixed, h, modtab, w_out, layer, ln0_g, ln0_b, router_w, router_b, w1, w3, w2, j, ln1_g, ln1_b,
                alpha, ctx_len, latent_only):
    b, t, d = h.shape
    n_experts = w1.shape[1]
    n_tiles = -(-(TOP_K * b * t) // MOE_TILE) + n_experts
    h, ri, gates, cnt = mix_moe_route(*mixed, h, modtab, w_out, layer, ln0_g, ln0_b, router_w, router_b, alpha,
                                      ctx_len)
    ln_g, ln_b = ln1_g, ln1_b
    counts = cnt[0, :n_experts].astype(jnp.int32)
    tiles_e = (counts + MOE_TILE - 1) // MOE_TILE
    ends = jnp.cumsum(tiles_e)
    offs = (ends - tiles_e) * MOE_TILE
    n_active = ends[-1:]
    tile_ids = jnp.minimum(jnp.arange(n_tiles, dtype=jnp.int32), n_active - 1)
    tile_expert = jnp.sum((ends[None, :] <= tile_ids[:, None]).astype(jnp.int32), axis=1)

    def slots(e, r):
        off = jnp.zeros_like(r)
        for k in range(n_experts):
            off = jnp.where(e == k, offs[k], off)
        return (off + r).reshape(b, t)

    slot1 = slots(ri[:, ROUTE_E1, :], ri[:, ROUTE_R1, :])
    slot2 = slots(ri[:, ROUTE_E2, :], ri[:, ROUTE_R2, :])
    xs = moe_dispatch(h, modtab, slot1, slot2, ends.astype(jnp.int32), n_tiles * MOE_TILE, ctx_len)
    ys = moe_grouped_ffn(xs, tile_expert, n_active.astype(jnp.int32), w1, w3, w2, j)
    tm, row0 = (TOK_TILE, ctx_len) if latent_only else (_row_tile(t), 0)
    return moe_combine(ys, slot1, slot2, gates, h, modtab, ln_g, ln_b, alpha, ctx_len, tm, row0)


def _rope_table(rows, ctx_len):
    def table(rot_dim):
        n_freq = rot_dim // 4
        inv = ROPE_BASE ** (-jnp.arange(n_freq, dtype=F32) / n_freq)
        row = jnp.repeat(jnp.arange(rows, dtype=F32), GRID_W)
        col = jnp.tile(jnp.arange(GRID_W, dtype=F32), rows)
        ang = jnp.concatenate([row[:, None] * inv, col[:, None] * inv], -1)
        cos, sin = jnp.cos(ang), jnp.sin(ang)
        reps = LANES // rot_dim
        cos_t = jnp.tile(jnp.concatenate([cos, cos], -1), (1, reps))
        sin_t = jnp.tile(jnp.concatenate([-sin, sin], -1), (1, reps))
        ident = (jnp.ones((ctx_len, LANES), F32), jnp.zeros((ctx_len, LANES), F32))
        return jnp.concatenate([ident[0], cos_t], 0), jnp.concatenate([ident[1], sin_t], 0)
    c64, s64 = table(HEAD_DIM)
    c32, s32 = table(DIFF_QK)
    return jnp.concatenate([c64, s64, c32, s32], axis=-1)


def _gate_dense(gate_w):
    cols = []
    for d in range(2):
        for g in range(2):
            cols.append(jax.scipy.linalg.block_diag(*[gate_w[d, g, k] for k in range(LRU_BLOCKS)]))
    return jnp.concatenate(cols, axis=1)


def kernel(x, c, ctx, c_ctx, ada_w, ada_b, w_in, w_out, lru_conv_w, lru_conv_b, lru_gate_w, lru_gate_b, lru_lam,
           swa_sink, diff_lam, diff_norm_g, ln_g, ln_b, ffn_w1, ffn_w3, ffn_w2, moe_router_w, moe_router_b,
           moe_w1, moe_w3, moe_w2):
    b, s, d = x.shape
    ctx_len = ctx.shape[1]
    depth = ada_w.shape[0]
    n_experts = moe_router_w.shape[-1]
    alpha = (2.0 * depth) ** 0.25
    assert s % GRID_W == 0 and s % TOK_TILE == 0 and ctx_len == TOK_TILE

    h = jnp.concatenate([ctx, x], axis=1)
    ropetab = _rope_table(s // GRID_W, ctx_len)

    rows = 8 * ((b + 1 + 7) // 8)
    cvec = jnp.zeros((rows, d), F32).at[:b].set(c).at[b].set(c_ctx)
    mods = ada_modulation(cvec, ada_w, ada_b).reshape(depth, rows, 6, d)
    modtab = jnp.stack([jnp.broadcast_to(mods[:, b:b + 1], (depth, b, 6, d)), mods[:, :b]], axis=2)

    sink_rows = jnp.repeat((swa_sink * math.log2(math.e)).reshape(depth, SWA_KV_HEADS, 1, SWA_GROUP), ATTN_BLOCK,
                           axis=-1)

    w_in, w_out, ffn_w1, ffn_w3, ffn_w2, moe_w1, moe_w3, moe_w2 = (
        w.astype(BF16) for w in (w_in, w_out, ffn_w1, ffn_w3, ffn_w2, moe_w1, moe_w3, moe_w2))

    for layer in range(depth):
        lam_init = 0.8 - 0.6 * math.exp(-0.3 * layer)
        mt = modtab[layer]
        lru_xg, attn_t = input_projection(h, mt, w_in, layer, ropetab)
        lru = rglru_mix(lru_xg, lru_conv_w[layer], lru_conv_b[layer].reshape(1, LRU_WIDTH),
                        _gate_dense(lru_gate_w[layer]).astype(BF16), lru_gate_b[layer].reshape(1, 4 * LRU_WIDTH),
                        lru_lam[layer], ctx_len)
        swa = windowed_attention(attn_t, sink_rows[layer])
        dif = differential_attention(attn_t, diff_lam[layer], diff_norm_g[layer], lam_init, layer < depth - 1)
        j = layer // 2
        ln0 = (ln_g[layer, 0:1], ln_b[layer, 0:1])
        ln1 = (ln_g[layer, 1:2], ln_b[layer, 1:2])
        if layer % 2 == 0:
            h = mix_dense_ffn(lru, swa, dif, h, mt, w_out, layer, *ln0, ffn_w1, ffn_w3, ffn_w2, j, *ln1, alpha, ctx_len)
        else:
            rw = jnp.zeros((d, LANES), F32).at[:, :n_experts].set(moe_router_w[j]).astype(BF16)
            rb = jnp.full((1, LANES), NEG_INF, F32).at[0, :n_experts].set(moe_router_b[j])
            h = mix_moe_ffn((lru, swa, dif), h, mt, w_out, layer, *ln0, rw, rb, moe_w1, moe_w3, moe_w2, j, *ln1,
                            alpha, ctx_len, latent_only=layer == depth - 1)
    return h if depth % 2 == 0 else h[:, ctx_len:, :]
```

```python
import functools
import math

import jax
import jax.numpy as jnp
from jax import lax
from jax.experimental import pallas as pl
from jax.experimental.pallas import tpu as pltpu

F32 = jnp.float32
BF16 = jnp.bfloat16

GRID_W = 64
HEAD_DIM = 64
LRU_WIDTH = 256
LRU_BLOCKS = 4
LRU_BLOCK = LRU_WIDTH // LRU_BLOCKS
CONV_W = 4
LRU_C = 8.0
SWA_Q_HEADS = 8
SWA_KV_HEADS = 2
SWA_GROUP = SWA_Q_HEADS // SWA_KV_HEADS
SWA_WIDTH = SWA_Q_HEADS * HEAD_DIM
ATTN_BLOCK = 128
DIFF_HEADS = 4
DIFF_QK = HEAD_DIM // 2
DIFF_WIDTH = DIFF_HEADS * HEAD_DIM
MIX_WIDTH = LRU_WIDTH + SWA_WIDTH + DIFF_WIDTH
TOP_K = 2
ROPE_BASE = 10000.0
LN_EPS = 1e-5
NEG_INF = -1e30
ATTN_MIN_SUM_LOG2 = -100.0
COL_LRU_X, COL_LRU_G, COL_SWA_Q, COL_SWA_K, COL_SWA_V, COL_DIFF_Q, COL_DIFF_K, COL_DIFF_V, IN_COLS = (
    0, 256, 512, 1024, 1152, 1280, 1536, 1792, 2048)
ATTN_COLS = IN_COLS - COL_SWA_Q
ROW_SWA_Q, ROW_SWA_KV, ROW_DIFF_Q, ROW_DIFF_K, ROW_DIFF_V = 0, 512, 768, 1024, 1280

LANES = 128
TOK_TILE = 256
VMEM_LIMIT = 56 * 1024 * 1024


def _cparams(sem, vmem=VMEM_LIMIT):
    return pltpu.CompilerParams(dimension_semantics=sem, vmem_limit_bytes=vmem)


def _const_spec(shape):
    nd = len(shape)
    return pl.BlockSpec(shape, lambda *_: (0,) * nd, pipeline_mode=pl.Buffered(1))


def _layer_spec(shape, layer):
    nd = len(shape)
    return pl.BlockSpec((None,) + tuple(shape), lambda *_: (layer,) + (0,) * nd, pipeline_mode=pl.Buffered(1))


def _ada_kernel(c_ref, w_ref, b_ref, o_ref):
    c = c_ref[...]
    s = (c * jax.nn.sigmoid(c)).astype(BF16)
    o_ref[...] = jnp.dot(s, w_ref[...].astype(BF16), preferred_element_type=F32) + b_ref[...]


def ada_modulation(cvec, ada_w, ada_b):
    depth, d, n = ada_w.shape
    r = cvec.shape[0]
    tn = 1536
    assert n % tn == 0
    return pl.pallas_call(
        _ada_kernel,
        out_shape=jax.ShapeDtypeStruct((depth, r, n), F32),
        grid=(depth, n // tn),
        in_specs=[pl.BlockSpec((r, d), lambda l, j: (0, 0)),
                  pl.BlockSpec((None, d, tn), lambda l, j: (l, 0, j)),
                  pl.BlockSpec((None, 1, tn), lambda l, j: (l, 0, j))],
        out_specs=pl.BlockSpec((None, r, tn), lambda l, j: (l, 0, j)),
        compiler_params=_cparams(("arbitrary", "arbitrary")),
        name="ada_modulation",
    )(cvec, ada_w, ada_b.reshape(depth, 1, n))


def _rope_group(x, cos, sin_signed, half, lane_mod):
    swapped = jnp.where(lane_mod < half, pltpu.roll(x, LANES - half, axis=1), pltpu.roll(x, half, axis=1))
    return x * cos + swapped * sin_signed


INPROJ_GROUP = 512


def _inproj_kernel(h_ref, mod_ref, w_ref, rope_ref, lru_ref, attn_ref, *, swa_scale, diff_scale):
    h = h_ref[...]
    u = (h * (1.0 + mod_ref[1:2, :]) + mod_ref[0:1, :]).astype(BF16)
    lane = lax.broadcasted_iota(jnp.int32, (1, LANES), 1)
    cos64, sin64 = rope_ref[:, 0:LANES], rope_ref[:, LANES:2 * LANES]
    cos32, sin32 = rope_ref[:, 2 * LANES:3 * LANES], rope_ref[:, 3 * LANES:4 * LANES]
    mod64, mod32 = lane % HEAD_DIM, lane % DIFF_QK
    for g0 in list(range(COL_SWA_Q, IN_COLS, INPROJ_GROUP)) + list(range(0, COL_SWA_Q, INPROJ_GROUP)):
        acc = jnp.dot(u, w_ref[:, g0:g0 + INPROJ_GROUP], preferred_element_type=F32)
        if g0 < COL_SWA_Q:
            lru_ref[:, g0:g0 + INPROJ_GROUP] = acc
            continue
        for c0 in range(g0, g0 + INPROJ_GROUP, LANES):
            x = acc[:, c0 - g0:c0 - g0 + LANES]
            if c0 < COL_SWA_V:
                x = _rope_group(x, cos64, sin64, HEAD_DIM // 2, mod64)
                if c0 < COL_SWA_K:
                    x = x * swa_scale
            elif COL_DIFF_Q <= c0 < COL_DIFF_V:
                x = _rope_group(x, cos32, sin32, DIFF_QK // 2, mod32)
                if c0 < COL_DIFF_K:
                    x = x * diff_scale
            r0 = c0 - COL_SWA_Q
            attn_ref[r0:r0 + LANES, :] = x.astype(BF16).T


def input_projection(h, modtab, w_in, layer, ropetab):
    b, t, d = h.shape
    tm = TOK_TILE
    nt = t // tm
    log2e = math.log2(math.e)
    kern = functools.partial(_inproj_kernel, swa_scale=HEAD_DIM ** -0.5 * log2e, diff_scale=DIFF_QK ** -0.5 * log2e)
    return pl.pallas_call(
        kern,
        out_shape=(jax.ShapeDtypeStruct((b, t, COL_SWA_Q), F32),
                   jax.ShapeDtypeStruct((b, nt, ATTN_COLS, tm), BF16)),
        grid=(b, nt),
        in_specs=[pl.BlockSpec((None, tm, d), lambda bi, i: (bi, i, 0)),
                  pl.BlockSpec((None, None, 6, d), lambda bi, i: (bi, jnp.minimum(i, 1), 0, 0)),
                  _layer_spec((d, IN_COLS), layer),
                  pl.BlockSpec((tm, 4 * LANES), lambda bi, i: (i, 0))],
        out_specs=(pl.BlockSpec((None, tm, COL_SWA_Q), lambda bi, i: (bi, i, 0)),
                   pl.BlockSpec((None, None, ATTN_COLS, tm), lambda bi, i: (bi, i, 0, 0))),
        compiler_params=_cparams(("arbitrary", "arbitrary")),
        name="input_projection",
    )(h, modtab, w_in, ropetab)


def _neg_expm1(z):
    series = z * (1.0 + z * (1 / 2 + z * (1 / 6 + z * (1 / 24 + z * (1 / 120 + z * (1 / 720))))))
    return jnp.where(z > -1.0 / 32, -series, 1.0 - jnp.exp(z))


def _lru_kernel(xg_ref, cw_ref, cb_ref, wg_ref, gb_ref, lam_ref, o_ref, a_s, b_s, h_s, *, chunk, ctx_chunks):
    t, c = o_ref.shape
    nchunk = t // chunk
    row = lax.broadcasted_iota(jnp.int32, (chunk, 1), 0)
    sp = jax.nn.softplus(-lam_ref[...])
    cw = cw_ref[...]
    cb = cb_ref[...]

    def coeffs(j, carry):
        r0 = pl.multiple_of(j * chunk, chunk)
        x = xg_ref[pl.ds(r0, chunk), 0:c]
        prev8 = xg_ref[pl.ds(pl.multiple_of(jnp.maximum(r0 - 8, 0), 8), 8), 0:c]
        next8 = xg_ref[pl.ds(pl.multiple_of(jnp.minimum(r0 + chunk, t - 8), 8), 8), 0:c]
        seg_start = jnp.logical_or(j == 0, j == ctx_chunks)
        seg_end = jnp.logical_or(j == ctx_chunks - 1, j == nchunk - 1)
        pm1 = jnp.where(seg_start, 0.0, prev8[7:8, :])
        np0 = jnp.where(seg_end, 0.0, next8[0:1, :])
        np1 = jnp.where(seg_end, 0.0, next8[1:2, :])
        xm1 = jnp.where(row == 0, pm1, pltpu.roll(x, 1, axis=0))
        xp1 = jnp.where(row == chunk - 1, np0, pltpu.roll(x, chunk - 1, axis=0))
        xp2 = jnp.where(row == chunk - 1, np1, jnp.where(row == chunk - 2, np0, pltpu.roll(x, chunk - 2, axis=0)))
        u = cw[0:1, :] * xm1 + cw[1:2, :] * x + cw[2:3, :] * xp1 + cw[3:4, :] * xp2 + cb
        g = jnp.dot(u.astype(BF16), wg_ref[...], preferred_element_type=F32) + gb_ref[...]
        for d in range(2):
            r = jax.nn.sigmoid(g[:, (2 * d) * c:(2 * d + 1) * c])
            i = jax.nn.sigmoid(g[:, (2 * d + 1) * c:(2 * d + 2) * c])
            log_a = (-LRU_C) * r * sp[d:d + 1, :]
            a_s[d, pl.ds(r0, chunk), :] = jnp.exp(log_a)
            b_s[d, pl.ds(r0, chunk), :] = jnp.sqrt(_neg_expm1(2.0 * log_a)) * (i * u)
        return carry

    lax.fori_loop(0, nchunk, coeffs, 0)

    sub = lax.broadcasted_iota(jnp.int32, (8, 1), 0)

    def tile_scan(a, b, carry, reverse):
        for s in (1, 2, 4):
            shift, seen = (8 - s, sub < 8 - s) if reverse else (s, sub >= s)
            a_prev = jnp.where(seen, pltpu.roll(a, shift, axis=0), 1.0)
            b_prev = jnp.where(seen, pltpu.roll(b, shift, axis=0), 0.0)
            b = a * b_prev + b
            a = a * a_prev
        h = a * carry + b
        return h, (h[0:1, :] if reverse else h[7:8, :])

    def scan_segment(lo, n8, hf, hb):
        def body(k, hh):
            hf, hb = hh
            f0 = pl.multiple_of(lo + k * 8, 8)
            r0 = pl.multiple_of(lo + (n8 - 1 - k) * 8, 8)
            fw, hf = tile_scan(a_s[0, pl.ds(f0, 8), :], b_s[0, pl.ds(f0, 8), :], hf, False)
            bw, hb = tile_scan(a_s[1, pl.ds(r0, 8), :], b_s[1, pl.ds(r0, 8), :], hb, True)
            h_s[0, pl.ds(f0, 8), :] = fw
            h_s[1, pl.ds(r0, 8), :] = bw
            return hf, hb
        return lax.fori_loop(0, n8, body, (hf, hb))

    zero = jnp.zeros((1, c), F32)
    ctx_rows = ctx_chunks * chunk
    hf, hb = scan_segment(0, ctx_rows // 8, zero, zero)
    scan_segment(ctx_rows, (t - ctx_rows) // 8, hf, hb)

    def finish(j, carry):
        r0 = pl.multiple_of(j * chunk, chunk)
        y = h_s[0, pl.ds(r0, chunk), :] + h_s[1, pl.ds(r0, chunk), :]
        gate = xg_ref[pl.ds(r0, chunk), c:2 * c]
        o_ref[pl.ds(r0, chunk), :] = (y * jax.nn.gelu(gate)).astype(o_ref.dtype)
        return carry

    lax.fori_loop(0, nchunk, finish, 0)


def rglru_mix(lru_xg, conv_w, conv_b, gate_dense, gate_b, lam, ctx_len):
    b, t, c2 = lru_xg.shape
    c = c2 // 2
    chunk = TOK_TILE
    assert t % chunk == 0 and ctx_len % chunk == 0
    kern = functools.partial(_lru_kernel, chunk=chunk, ctx_chunks=ctx_len // chunk)
    return pl.pallas_call(
        kern,
        out_shape=jax.ShapeDtypeStruct((b, t, c), BF16),
        grid=(b,),
        in_specs=[pl.BlockSpec((None, t, c2), lambda bi: (bi, 0, 0), pipeline_mode=pl.Buffered(1)),
                  _const_spec((CONV_W, c)), _const_spec((1, c)), _const_spec((c, 4 * c)), _const_spec((1, 4 * c)),
                  _const_spec((2, c))],
        out_specs=pl.BlockSpec((None, t, c), lambda bi: (bi, 0, 0)),
        scratch_shapes=[pltpu.VMEM((2, t, c), F32), pltpu.VMEM((2, t, c), F32), pltpu.VMEM((2, t, c), F32)],
        compiler_params=_cparams(("arbitrary",)),
        name="rglru_mix",
    )(lru_xg, conv_w, conv_b, gate_dense, gate_b, lam)


def _dot_t0(a, b):
    return lax.dot_general(a, b, (((0,), (0,)), ((), ())), preferred_element_type=F32)


SWA_MAX_KEY_BLOCKS = 5


def _swa_kernel(q_ref, own_ref, prev_ref, next_ref, ctx_ref, sink_ref, o_ref, s_s, p_s, l_s, ot_s):
    i = pl.program_id(1)
    nt = pl.num_programs(1)
    blk = ATTN_BLOCK
    nq = SWA_GROUP * blk
    kp = lax.broadcasted_iota(jnp.int32, (blk, nq), 0)
    qp = lax.broadcasted_iota(jnp.int32, (blk, nq), 1) % blk

    def kv(ref, hk, half):
        k = ref[hk * HEAD_DIM:(hk + 1) * HEAD_DIM, half * blk:(half + 1) * blk]
        v = ref[SWA_KV_HEADS * HEAD_DIM + hk * HEAD_DIM:SWA_KV_HEADS * HEAD_DIM + (hk + 1) * HEAD_DIM,
                half * blk:(half + 1) * blk]
        return k, v

    def q_tile(hk, half):
        return jnp.concatenate(
            [q_ref[(hk * SWA_GROUP + g) * HEAD_DIM:(hk * SWA_GROUP + g + 1) * HEAD_DIM,
                   half * blk:(half + 1) * blk] for g in range(SWA_GROUP)], axis=1)

    def put(hk, half, o):
        for g in range(SWA_GROUP):
            f0 = (hk * SWA_GROUP + g) * HEAD_DIM
            ot_s[f0:f0 + HEAD_DIM, half * blk:(half + 1) * blk] = o[:, g * blk:(g + 1) * blk]

    def run_bounded(jobs):
        ones = jnp.ones((16, SWA_MAX_KEY_BLOCKS * blk), BF16)
        for a, (hk, half, keys, _, masks) in enumerate(jobs):
            qt = q_tile(hk, half)
            qf = qt.astype(F32)
            kmax2 = jnp.zeros((1, 1), F32)
            for kt in keys:
                kf = kt.astype(F32)
                kmax2 = jnp.maximum(kmax2, jnp.max(jnp.sum(kf * kf, axis=0, keepdims=True), axis=1, keepdims=True))
            ref = jnp.maximum(jnp.sqrt(jnp.sum(qf * qf, axis=0, keepdims=True) * kmax2), sink_ref[hk])
            l_s[a:a + 1, :] = ref
            for j, (kt, msk) in enumerate(zip(keys, masks)):
                sj = _dot_t0(kt, qt)
                if msk is not None:
                    sj = jnp.where(msk, sj, NEG_INF)
                p_s[a, j * blk:(j + 1) * blk, :] = jnp.exp2(sj - ref).astype(BF16)
        worst = None
        for a, (hk, half, keys, vals, _) in enumerate(jobs):
            nk = len(keys) * blk
            v1 = jnp.concatenate([jnp.concatenate(vals, axis=1), ones[:, 0:nk]], axis=0)
            oe = jnp.dot(v1, p_s[a, 0:nk, :], preferred_element_type=F32)
            l = oe[HEAD_DIM:HEAD_DIM + 1, :] + jnp.exp2(sink_ref[hk] - l_s[a:a + 1, :])
            put(hk, half, oe[0:HEAD_DIM, :] / l)
            worst = l if worst is None else jnp.minimum(worst, l)
        return jnp.min(worst) < 2.0 ** ATTN_MIN_SUM_LOG2

    def run_exact(jobs):
        for a, (hk, half, keys, _, _) in enumerate(jobs):
            qt = q_tile(hk, half)
            for j, kt in enumerate(keys):
                s_s[a, j * blk:(j + 1) * blk, :] = _dot_t0(kt, qt)
        for a, (hk, _, keys, _, masks) in enumerate(jobs):
            sink = sink_ref[hk]
            s = []
            for j, msk in enumerate(masks):
                sj = s_s[a, j * blk:(j + 1) * blk, :]
                s.append(sj if msk is None else jnp.where(msk, sj, NEG_INF))
            m = sink
            for sj in s:
                m = jnp.maximum(m, jnp.max(sj, axis=0, keepdims=True))
            l = jnp.exp2(sink - m)
            for j, sj in enumerate(s):
                pj = jnp.exp2(sj - m)
                l = l + jnp.sum(pj, axis=0, keepdims=True)
                p_s[a, j * blk:(j + 1) * blk, :] = pj.astype(BF16)
            l_s[a:a + 1, :] = l
        for a, (hk, half, keys, vals, _) in enumerate(jobs):
            nk = len(keys) * blk
            vt = jnp.concatenate(vals, axis=1)
            put(hk, half, jnp.dot(vt, p_s[a, 0:nk, :], preferred_element_type=F32) / l_s[a:a + 1, :])

    def run(jobs):
        lost_range = run_bounded(jobs)

        @pl.when(lost_range)
        def _():
            run_exact(jobs)

    @pl.when(i == 0)
    def _():
        jobs = []
        for hk in range(SWA_KV_HEADS):
            k0, v0 = kv(ctx_ref, hk, 0)
            k1, v1 = kv(ctx_ref, hk, 1)
            for half in range(2):
                jobs.append((hk, half, [k0, k1], [v0, v1], [None, None]))
        run(jobs)

    @pl.when(i > 0)
    def _():
        has_prev = i > 1
        has_next = i < nt - 1
        m_prev = kp >= qp
        m_next = kp <= qp
        jobs = []
        for hk in range(SWA_KV_HEADS):
            c0k, c0v = kv(ctx_ref, hk, 0)
            c1k, c1v = kv(ctx_ref, hk, 1)
            o0k, o0v = kv(own_ref, hk, 0)
            o1k, o1v = kv(own_ref, hk, 1)
            pk, pv = kv(prev_ref, hk, 1)
            nk, nv = kv(next_ref, hk, 0)
            jobs.append((hk, 0, [c0k, c1k, pk, o0k, o1k], [c0v, c1v, pv, o0v, o1v],
                         [None, None, jnp.logical_and(m_prev, has_prev), None, m_next]))
            jobs.append((hk, 1, [c0k, c1k, o0k, o1k, nk], [c0v, c1v, o0v, o1v, nv],
                         [None, None, m_prev, None, jnp.logical_and(m_next, has_next)]))
        run(jobs)

    o_ref[...] = ot_s[...].T.astype(o_ref.dtype)


def windowed_attention(attn_t, sink_rows):
    b, nt, _, tm = attn_t.shape
    kvrows = 2 * SWA_KV_HEADS * HEAD_DIM
    kvblk = ROW_SWA_KV // kvrows
    assert ROW_SWA_KV % kvrows == 0 and tm == 2 * ATTN_BLOCK
    return pl.pallas_call(
        _swa_kernel,
        out_shape=jax.ShapeDtypeStruct((b, nt * tm, SWA_WIDTH), BF16),
        grid=(b, nt),
        in_specs=[pl.BlockSpec((None, None, SWA_WIDTH, tm), lambda bi, i: (bi, i, 0, 0)),
                  pl.BlockSpec((None, None, kvrows, tm), lambda bi, i: (bi, i, kvblk, 0)),
                  pl.BlockSpec((None, None, kvrows, tm), lambda bi, i: (bi, jnp.maximum(i - 1, 1), kvblk, 0)),
                  pl.BlockSpec((None, None, kvrows, tm), lambda bi, i: (bi, jnp.minimum(i + 1, nt - 1), kvblk, 0)),
                  pl.BlockSpec((None, None, kvrows, tm), lambda bi, i: (bi, 0, kvblk, 0)),
                  pl.BlockSpec((SWA_KV_HEADS, 1, SWA_GROUP * ATTN_BLOCK), lambda bi, i: (0, 0, 0))],
        out_specs=pl.BlockSpec((None, tm, SWA_WIDTH), lambda bi, i: (bi, i, 0)),
        scratch_shapes=[pltpu.VMEM((2 * SWA_KV_HEADS, SWA_MAX_KEY_BLOCKS * ATTN_BLOCK, SWA_GROUP * ATTN_BLOCK), F32),
                        pltpu.VMEM((2 * SWA_KV_HEADS, SWA_MAX_KEY_BLOCKS * ATTN_BLOCK, SWA_GROUP * ATTN_BLOCK), BF16),
                        pltpu.VMEM((2 * SWA_KV_HEADS, SWA_GROUP * ATTN_BLOCK), F32),
                        pltpu.VMEM((SWA_WIDTH, tm), F32)],
        compiler_params=_cparams(("arbitrary", "arbitrary")),
        name="windowed_attention",
    )(attn_t, attn_t, attn_t, attn_t, attn_t, sink_rows)


DIFF_ACC_ROWS = HEAD_DIM + 16
DIFF_Q_TILES = 4


def _diff_kernel(q_ref, k_ref, v_ref, lam_ref, g_ref, o_ref, m_s, a_s, acc_s, s_s, p_s, ot_s, kn_s, *, lam_init):
    nkv = k_ref.shape[0]
    last = nkv - 1
    ncomp = 2 * DIFF_HEADS
    ones = jnp.ones((DIFF_ACC_ROWS - HEAD_DIM, k_ref.shape[-1]), BF16)

    def q_comp(c):
        return jnp.concatenate([q_ref[t, c * DIFF_QK:(c + 1) * DIFF_QK, :] for t in range(q_ref.shape[0])], axis=1)

    def add_values(j, slot, rescale):
        vt = v_ref[j]
        for c in range(ncomp):
            h = c // 2
            v1 = jnp.concatenate([vt[h * HEAD_DIM:(h + 1) * HEAD_DIM, :], ones], axis=0)
            old = a_s[slot, c:c + 1, :] * acc_s[c] if rescale else acc_s[c]
            acc_s[c] = old + jnp.dot(v1, p_s[slot, c], preferred_element_type=F32)

    def key_norms(j, mx):
        kf = k_ref[j].astype(F32)
        sq = kf * kf
        n2 = jnp.concatenate([jnp.sum(sq[c * DIFF_QK:(c + 1) * DIFF_QK, :], axis=0, keepdims=True)
                              for c in range(ncomp)], axis=0)
        return jnp.maximum(mx, n2)

    @pl.when(pl.program_id(1) == 0)
    def _():
        kn_s[...] = lax.fori_loop(0, nkv, key_norms, jnp.zeros(kn_s.shape, F32))

    kmax2 = jnp.max(kn_s[...], axis=1, keepdims=True)
    for c in range(ncomp):
        qf = q_comp(c).astype(F32)
        m_s[c:c + 1, :] = jnp.sqrt(jnp.sum(qf * qf, axis=0, keepdims=True) * kmax2[c:c + 1, :])
    acc_s[...] = jnp.zeros(acc_s.shape, F32)

    def bounded_probs(j, slot):
        kt = k_ref[j]
        for c in range(ncomp):
            s = _dot_t0(kt[c * DIFF_QK:(c + 1) * DIFF_QK, :], q_comp(c))
            p_s[slot, c] = jnp.exp2(s - m_s[c:c + 1, :]).astype(BF16)

    bounded_probs(0, 0)

    def bounded_pair(jj, carry):
        j = 2 * jj
        bounded_probs(j + 1, 1)
        add_values(j, 0, False)
        bounded_probs(j + 2, 0)
        add_values(j + 1, 1, False)
        return carry

    lax.fori_loop(0, last // 2, bounded_pair, 0)
    add_values(last, 0, False)

    min_sum = acc_s[0, HEAD_DIM:HEAD_DIM + 1, :]
    for c in range(1, ncomp):
        min_sum = jnp.minimum(min_sum, acc_s[c, HEAD_DIM:HEAD_DIM + 1, :])
    lost_range = jnp.min(min_sum) < 2.0 ** ATTN_MIN_SUM_LOG2

    def scores(j, slot):
        kt = k_ref[j]
        for c in range(ncomp):
            s_s[slot, c] = _dot_t0(kt[c * DIFF_QK:(c + 1) * DIFF_QK, :], q_comp(c))

    def softmax(slot):
        for c in range(ncomp):
            s = s_s[slot, c]
            m_old = m_s[c:c + 1, :]
            m_new = jnp.maximum(m_old, jnp.max(s, axis=0, keepdims=True))
            m_s[c:c + 1, :] = m_new
            a_s[slot, c:c + 1, :] = jnp.exp2(m_old - m_new)
            p_s[slot, c] = jnp.exp2(s - m_new).astype(BF16)

    @pl.when(lost_range)
    def _():
        m_s[...] = jnp.full(m_s.shape, NEG_INF, F32)
        acc_s[...] = jnp.zeros(acc_s.shape, F32)
        scores(0, 0)
        scores(min(1, last), 1)
        softmax(0)

        def pair(jj, carry):
            j = 2 * jj
            scores(j + 2, 0)
            softmax(1)
            add_values(j, 0, True)
            scores(jnp.minimum(j + 3, last), 1)
            softmax(0)
            add_values(j + 1, 1, True)
            return carry

        lax.fori_loop(0, last // 2, pair, 0)
        add_values(last, 0, True)

    lam_v = lam_ref[...]
    lam = (jnp.exp(jnp.sum(lam_v[0:1, :] * lam_v[1:2, :], axis=1, keepdims=True))
           - jnp.exp(jnp.sum(lam_v[2:3, :] * lam_v[3:4, :], axis=1, keepdims=True)) + lam_init)

    def head_out(c):
        return acc_s[c, 0:HEAD_DIM, :] / acc_s[c, HEAD_DIM:HEAD_DIM + 1, :]

    for h in range(DIFF_HEADS):
        o = head_out(2 * h) - lam * head_out(2 * h + 1)
        o = o * lax.rsqrt(jnp.mean(jnp.square(o), axis=0, keepdims=True) + LN_EPS)
        ot_s[h * HEAD_DIM:(h + 1) * HEAD_DIM, :] = (o * (1.0 - lam_init)) * g_ref[...]
    o_ref[...] = ot_s[...].T.astype(o_ref.dtype)


def _diff_call(q_tiles, q_feat_blk, kv, nk, nq, diff_lam, norm_g, lam_init):
    b, nqt, _, tm = q_tiles.shape
    assert nk % 2 == 1 and nqt % nq == 0
    tq = nq * tm
    ncomp = 2 * DIFF_HEADS
    kern = functools.partial(_diff_kernel, lam_init=lam_init)
    return pl.pallas_call(
        kern,
        out_shape=jax.ShapeDtypeStruct((b, nqt * tm, DIFF_WIDTH), BF16),
        grid=(b, nqt // nq),
        in_specs=[pl.BlockSpec((None, nq, DIFF_WIDTH, tm), lambda bi, i: (bi, i, q_feat_blk, 0)),
                  pl.BlockSpec((None, nk, DIFF_WIDTH, tm), lambda bi, i: (bi, 0, ROW_DIFF_K // DIFF_WIDTH, 0)),
                  pl.BlockSpec((None, nk, DIFF_WIDTH, tm), lambda bi, i: (bi, 0, ROW_DIFF_V // DIFF_WIDTH, 0)),
                  pl.BlockSpec((4, DIFF_QK), lambda bi, i: (0, 0)),
                  pl.BlockSpec((HEAD_DIM, 1), lambda bi, i: (0, 0))],
        out_specs=pl.BlockSpec((None, tq, DIFF_WIDTH), lambda bi, i: (bi, i, 0)),
        scratch_shapes=[pltpu.VMEM((ncomp, tq), F32), pltpu.VMEM((2, ncomp, tq), F32),
                        pltpu.VMEM((ncomp, DIFF_ACC_ROWS, tq), F32), pltpu.VMEM((2, ncomp, tm, tq), F32),
                        pltpu.VMEM((2, ncomp, tm, tq), BF16), pltpu.VMEM((DIFF_WIDTH, tq), F32),
                        pltpu.VMEM((ncomp, tm), F32)],
        compiler_params=_cparams(("arbitrary", "arbitrary")),
        name="differential_attention",
    )(q_tiles, kv, kv, diff_lam, norm_g.reshape(HEAD_DIM, 1))


def differential_attention(attn_t, diff_lam, norm_g, lam_init, need_ctx):
    b, nt, _, tm = attn_t.shape
    q_lat = attn_t[:, 1:, ROW_DIFF_Q:ROW_DIFF_Q + DIFF_WIDTH, :]
    lat = _diff_call(q_lat, 0, attn_t, nt, math.gcd(DIFF_Q_TILES, nt - 1), diff_lam, norm_g, lam_init)
    if need_ctx:
        ctx = _diff_call(attn_t[:, 0:1], ROW_DIFF_Q // DIFF_WIDTH, attn_t, 1, 1, diff_lam, norm_g, lam_init)
    else:
        ctx = jnp.zeros((b, tm, DIFF_WIDTH), BF16)
    return jnp.concatenate([ctx, lat], axis=1)


def _mod_rows(mod_ref, k, is_ctx):
    return jnp.where(is_ctx, mod_ref[0, k:k + 1, :], mod_ref[1, k:k + 1, :])


def _is_ctx_rows(tm, ctx_len):
    r0 = pl.program_id(1) * tm
    return (r0 + lax.broadcasted_iota(jnp.int32, (tm, 1), 0)) < ctx_len


def _deepnorm_ln(h, branch, gate, g, b, alpha):
    x = alpha * h + gate * branch
    mu = jnp.mean(x, axis=-1, keepdims=True)
    xc = x - mu
    var = jnp.mean(xc * xc, axis=-1, keepdims=True)
    return xc * lax.rsqrt(var + LN_EPS) * g + b


def _mixed_projection(lru_ref, swa_ref, diff_ref, w_ref, rows=slice(None)):
    m = jnp.dot(lru_ref[rows, :], w_ref[0:LRU_WIDTH, :], preferred_element_type=F32)
    m = m + jnp.dot(swa_ref[rows, :], w_ref[LRU_WIDTH:LRU_WIDTH + SWA_WIDTH, :], preferred_element_type=F32)
    return m + jnp.dot(diff_ref[rows, :], w_ref[LRU_WIDTH + SWA_WIDTH:MIX_WIDTH, :], preferred_element_type=F32)


def _row_tile(t):
    for tm in range(640, 15, -16):
        if t % tm == 0:
            return tm
    raise ValueError(f"no row tile for T={t}")


def _token_mix_specs(tm, d, layer):
    row = lambda w: pl.BlockSpec((None, tm, w), lambda bi, i: (bi, i, 0))
    return [row(LRU_WIDTH), row(SWA_WIDTH), row(DIFF_WIDTH), row(d),
            pl.BlockSpec((None, 2, 6, d), lambda bi, i: (bi, 0, 0, 0)),
            _layer_spec((MIX_WIDTH, d), layer), _const_spec((1, d)), _const_spec((1, d))]


FF_CHUNK = 256


def _swiglu_hidden(u, w1_ref, w3_ref, g_s, scale=None):
    ff = w1_ref.shape[-1]
    for c0 in range(0, ff, FF_CHUNK):
        h1 = jnp.dot(u, w1_ref[:, c0:c0 + FF_CHUNK], preferred_element_type=F32)
        h3 = jnp.dot(u, w3_ref[:, c0:c0 + FF_CHUNK], preferred_element_type=F32)
        g = (h1 * jax.nn.sigmoid(h1)) * h3
        if scale is not None:
            g = g * scale
        g_s[:, c0:c0 + FF_CHUNK] = g.astype(BF16)


def _mix_ffn_kernel(lru_ref, swa_ref, diff_ref, h_ref, mod_ref, wo_ref, g0_ref, b0_ref, w1_ref, w3_ref, w2_ref,
                    g1_ref, b1_ref, o_ref, g_s, *, alpha, ctx_len):
    tm = h_ref.shape[0]
    half = tm // 2
    ctx_rows = _is_ctx_rows(tm, ctx_len)
    rows = [slice(r * half, (r + 1) * half) for r in range(2)]
    is_ctx = [ctx_rows[rw, :] for rw in rows]
    m = [_mixed_projection(lru_ref, swa_ref, diff_ref, wo_ref, rw) for rw in rows]
    h = [None, None]
    for r in range(2):
        h[r] = _deepnorm_ln(h_ref[rows[r], :], m[r], _mod_rows(mod_ref, 2, is_ctx[r]), g0_ref[...], b0_ref[...], alpha)
        u = (h[r] * (1.0 + _mod_rows(mod_ref, 4, is_ctx[r])) + _mod_rows(mod_ref, 3, is_ctx[r])).astype(BF16)
        _swiglu_hidden(u, w1_ref, w3_ref, g_s.at[r])
    for r in range(2):
        f = jnp.dot(g_s[r], w2_ref[...], preferred_element_type=F32)
        o_ref[rows[r], :] = _deepnorm_ln(h[r], f, _mod_rows(mod_ref, 5, is_ctx[r]), g1_ref[...], b1_ref[...], alpha)


def mix_dense_ffn(lru, swa, diff, h, modtab, w_out, layer, ln0_g, ln0_b, w1, w3, w2, j, ln1_g, ln1_b, alpha, ctx_len):
    b, t, d = h.shape
    ff = w1.shape[-1]
    assert ff % FF_CHUNK == 0
    tm = _row_tile(t)
    kern = functools.partial(_mix_ffn_kernel, alpha=alpha, ctx_len=ctx_len)
    return pl.pallas_call(
        kern,
        out_shape=jax.ShapeDtypeStruct((b, t, d), F32),
        grid=(b, t // tm),
        in_specs=_token_mix_specs(tm, d, layer) + [
            _layer_spec((d, ff), j), _layer_spec((d, ff), j), _layer_spec((ff, d), j),
            _const_spec((1, d)), _const_spec((1, d))],
        out_specs=pl.BlockSpec((None, tm, d), lambda bi, i: (bi, i, 0)),
        scratch_shapes=[pltpu.VMEM((2, tm // 2, ff), BF16)],
        compiler_params=_cparams(("arbitrary", "arbitrary")),
        name="mix_dense_ffn",
    )(lru, swa, diff, h, modtab, w_out, ln0_g, ln0_b, w1, w3, w2, ln1_g, ln1_b)


MOE_TILE = 512
ROUTE_E1, ROUTE_E2, ROUTE_R1, ROUTE_R2 = 0, 1, 2, 3


def _mix_route_kernel(lru_ref, swa_ref, diff_ref, h_ref, mod_ref, wo_ref, g0_ref, b0_ref, rw_ref, rb_ref,
                      h1_ref, ri_ref, rg_ref, cnt_ref, cnt_s, *, alpha, ctx_len):
    tm = h_ref.shape[0]
    half = tm // 2
    ctx_rows = _is_ctx_rows(tm, ctx_len)

    @pl.when(jnp.logical_and(pl.program_id(0) == 0, pl.program_id(1) == 0))
    def _():
        cnt_s[...] = jnp.zeros(cnt_s.shape, F32)

    halves = [slice(r * half, (r + 1) * half) for r in range(2)]
    mixed = [_mixed_projection(lru_ref, swa_ref, diff_ref, wo_ref, rw) for rw in halves]
    lane = lax.broadcasted_iota(jnp.int32, (half, LANES), 1)
    before = lax.broadcasted_iota(jnp.int32, (half, half), 1) < lax.broadcasted_iota(jnp.int32, (half, half), 0)
    before = jnp.where(before, 1.0, 0.0).astype(BF16)
    count = cnt_s[...]
    tables = []
    for rw, m in zip(halves, mixed):
        is_ctx = ctx_rows[rw, :]
        h = _deepnorm_ln(h_ref[rw, :], m, _mod_rows(mod_ref, 2, is_ctx), g0_ref[...], b0_ref[...], alpha)
        h1_ref[rw, :] = h
        u = (h * (1.0 + _mod_rows(mod_ref, 4, is_ctx)) + _mod_rows(mod_ref, 3, is_ctx)).astype(BF16)
        logits = jnp.dot(u, rw_ref[...], preferred_element_type=F32) + rb_ref[...]
        m1 = jnp.max(logits, axis=-1, keepdims=True)
        i1 = jnp.min(jnp.where(logits == m1, lane, LANES), axis=-1, keepdims=True)
        rest = jnp.where(lane == i1, -jnp.inf, logits)
        m2 = jnp.max(rest, axis=-1, keepdims=True)
        i2 = jnp.min(jnp.where(rest == m2, lane, LANES), axis=-1, keepdims=True)
        e = jnp.exp(m2 - m1)
        g1 = 1.0 / (1.0 + e)
        g2 = e / (1.0 + e)
        sel1 = lane == i1
        sel2 = lane == i2
        picked = jnp.where(jnp.logical_or(sel1, sel2), 1.0, 0.0)
        rank = count + jnp.dot(before, picked.astype(BF16), preferred_element_type=F32)
        r1 = jnp.sum(jnp.where(sel1, rank, 0.0), axis=-1, keepdims=True)
        r2 = jnp.sum(jnp.where(sel2, rank, 0.0), axis=-1, keepdims=True)
        count = count + jnp.sum(picked, axis=0, keepdims=True)
        rg_ref[rw, :] = jnp.where(lane == 0, g1, jnp.where(lane == 1, g2, 0.0))
        r1_hi = jnp.floor(r1 * (1.0 / 256.0))
        r2_hi = jnp.floor(r2 * (1.0 / 256.0))
        cols = (i1.astype(F32), i2.astype(F32), r1_hi, r1 - 256.0 * r1_hi, r2_hi, r2 - 256.0 * r2_hi)
        table = jnp.zeros(logits.shape, F32)
        for k, col in enumerate(cols):
            table = jnp.where(lane == k, col, table)
        tables.append(table)
    cnt_s[...] = count
    cnt_ref[...] = count
    table = jnp.concatenate(tables, axis=0)
    pick = (lax.broadcasted_iota(jnp.int32, (8, LANES), 0) == lax.broadcasted_iota(jnp.int32, (8, LANES), 1))
    rows = lax.dot_general(jnp.where(pick, 1.0, 0.0).astype(BF16), table.astype(BF16),
                           (((1,), (1,)), ((), ())), preferred_element_type=F32)
    out = jnp.concatenate([rows[0:2], 256.0 * rows[2:3] + rows[3:4], 256.0 * rows[4:5] + rows[5:6],
                           jnp.zeros((4, tm), F32)], axis=0)
    ri_ref[...] = out.astype(jnp.int32)


def mix_moe_route(lru, swa, diff, h, modtab, w_out, layer, ln0_g, ln0_b, router_w, router_b, alpha, ctx_len):
    b, t, d = h.shape
    tm = _row_tile(t)
    nt = t // tm
    kern = functools.partial(_mix_route_kernel, alpha=alpha, ctx_len=ctx_len)
    row = lambda w: pl.BlockSpec((None, tm, w), lambda bi, i: (bi, i, 0))
    return pl.pallas_call(
        kern,
        out_shape=(jax.ShapeDtypeStruct((b, t, d), F32), jax.ShapeDtypeStruct((b * nt, 8, tm), jnp.int32),
                   jax.ShapeDtypeStruct((b, t, LANES), F32), jax.ShapeDtypeStruct((1, LANES), F32)),
        grid=(b, nt),
        in_specs=_token_mix_specs(tm, d, layer) + [_const_spec((d, LANES)), _const_spec((1, LANES))],
        out_specs=(row(d), pl.BlockSpec((None, 8, tm), lambda bi, i: (bi * nt + i, 0, 0)), row(LANES),
                   pl.BlockSpec((1, LANES), lambda bi, i: (0, 0))),
        scratch_shapes=[pltpu.VMEM((1, LANES), F32)],
        compiler_params=_cparams(("arbitrary", "arbitrary")),
        name="mix_moe_route",
    )(lru, swa, diff, h, modtab, w_out, ln0_g, ln0_b, router_w, router_b)


def _row_dma_params(vmem=VMEM_LIMIT):
    return pltpu.CompilerParams(dimension_semantics=("arbitrary", "arbitrary"), vmem_limit_bytes=vmem,
                                disable_bounds_checks=True)


def _dispatch_kernel(s1_ref, s2_ref, ends_ref, h_ref, mod_ref, xs_out, w_s, z_s, sem, zsem, *, ctx_len):
    tm = h_ref.shape[0]
    g = pl.program_id(0) * pl.num_programs(1) + pl.program_id(1)
    last = pl.num_programs(0) * pl.num_programs(1) - 1
    cur = g % 2

    @pl.when(g == 0)
    def _():
        z_s[...] = jnp.zeros(z_s.shape, F32)
        n_experts = ends_ref.shape[0]

        def zero_tile(tile):
            row0 = pl.multiple_of(tile * MOE_TILE, MOE_TILE)
            return pltpu.make_async_copy(z_s, xs_out.at[pl.ds(row0, MOE_TILE), :], zsem)

        for e in range(n_experts):
            zero_tile(jnp.maximum(ends_ref[e] - 1, 0)).start()
        for e in range(n_experts):
            zero_tile(0).wait()

        def zero_unused(tile, carry):
            zero_tile(tile).start()
            zero_tile(tile).wait()
            return carry

        lax.fori_loop(ends_ref[n_experts - 1], xs_out.shape[0] // MOE_TILE, zero_unused, 0)

    def drain(buf):
        for k in range(TOP_K):
            pltpu.make_async_copy(w_s.at[buf], xs_out.at[pl.ds(0, tm), :], sem.at[buf, k]).wait()

    @pl.when(g >= 2)
    def _():
        drain(cur)

    is_ctx = _is_ctx_rows(tm, ctx_len)
    h = h_ref[...]
    w_s[cur] = h * (1.0 + _mod_rows(mod_ref, 4, is_ctx)) + _mod_rows(mod_ref, 3, is_ctx)

    def row_copy(r, slots, k):
        return pltpu.make_async_copy(w_s.at[cur, pl.ds(r, 1), :], xs_out.at[pl.ds(slots[0, r], 1), :],
                                     sem.at[cur, k])

    def issue(r, carry):
        row_copy(r, s1_ref, 0).start()
        row_copy(r, s2_ref, 1).start(priority=1)
        return carry

    lax.fori_loop(0, tm, issue, 0, unroll=8)

    @pl.when(g == last)
    def _():
        @pl.when(g >= 1)
        def _():
            drain(1 - cur)
        drain(cur)


def moe_dispatch(h, modtab, slot1, slot2, ends, n_slots, ctx_len):
    b, t, d = h.shape
    tm = _row_tile(t)
    nt = t // tm
    kern = functools.partial(_dispatch_kernel, ctx_len=ctx_len)
    slot_spec = pl.BlockSpec((None, 1, tm), lambda bi, i: (bi * nt + i, 0, 0), memory_space=pltpu.SMEM)
    return pl.pallas_call(
        kern,
        out_shape=jax.ShapeDtypeStruct((n_slots, d), F32),
        grid=(b, nt),
        in_specs=[slot_spec, slot_spec, pl.BlockSpec(memory_space=pltpu.SMEM),
                  pl.BlockSpec((None, tm, d), lambda bi, i: (bi, i, 0)),
                  pl.BlockSpec((None, 2, 6, d), lambda bi, i: (bi, 0, 0, 0))],
        out_specs=pl.BlockSpec(memory_space=pl.ANY),
        scratch_shapes=[pltpu.VMEM((2, tm, d), F32), pltpu.VMEM((MOE_TILE, d), F32),
                        pltpu.SemaphoreType.DMA((2, TOP_K)), pltpu.SemaphoreType.DMA(())],
        compiler_params=_row_dma_params(),
        name="moe_dispatch",
    )(slot1.reshape(b * nt, 1, tm), slot2.reshape(b * nt, 1, tm), ends, h, modtab)


def _gffn_kernel(te_ref, na_ref, xs_ref, w1_ref, w3_ref, w2_ref, ys_ref, g_s):
    del te_ref
    i = pl.program_id(0)

    @pl.when(i < na_ref[0])
    def _():
        _swiglu_hidden(xs_ref[...].astype(BF16), w1_ref, w3_ref, g_s)
        ys_ref[...] = jnp.dot(g_s[...], w2_ref[...], preferred_element_type=F32)

    @pl.when(i >= na_ref[0])
    def _():
        ys_ref[...] = jnp.zeros(ys_ref.shape, ys_ref.dtype)


def moe_grouped_ffn(xs, tile_expert, n_active, w1, w3, w2, j):
    n_slots, dh = xs.shape
    _, _, d, ff = w1.shape
    tm = MOE_TILE
    n_tiles = n_slots // tm
    return pl.pallas_call(
        _gffn_kernel,
        out_shape=jax.ShapeDtypeStruct((n_slots, dh), F32),
        grid_spec=pltpu.PrefetchScalarGridSpec(
            num_scalar_prefetch=2,
            grid=(n_tiles,),
            in_specs=[pl.BlockSpec((tm, dh), lambda i, te, na: (jnp.minimum(i, na[0] - 1), 0)),
                      pl.BlockSpec((None, None, d, ff), lambda i, te, na: (j, te[i], 0, 0)),
                      pl.BlockSpec((None, None, d, ff), lambda i, te, na: (j, te[i], 0, 0)),
                      pl.BlockSpec((None, None, ff, d), lambda i, te, na: (j, te[i], 0, 0))],
            out_specs=pl.BlockSpec((tm, dh), lambda i, te, na: (i, 0)),
            scratch_shapes=[pltpu.VMEM((tm, ff), BF16)]),
        compiler_params=_cparams(("arbitrary",)),
        name="moe_grouped_ffn",
    )(tile_expert, n_active, xs, w1, w3, w2)


def _combine_kernel(s1_ref, s2_ref, n1_ref, n2_ref, ys_hbm, h_ref, mod_ref, rg_ref, g_ref, b_ref, o_ref, y_s, sem,
                    *, alpha, ctx_len, tile0):
    tm = h_ref.shape[0]
    g = pl.program_id(0) * pl.num_programs(1) + pl.program_id(1)
    last = pl.num_programs(0) * pl.num_programs(1) - 1
    cur = g % 2

    def gather(slots1, slots2, buf):
        def row_copy(r, slots, k):
            return pltpu.make_async_copy(ys_hbm.at[pl.ds(slots[0, r], 1), :], y_s.at[buf, k, pl.ds(r, 1), :],
                                         sem.at[buf, k])

        def issue(r, carry):
            row_copy(r, slots1, 0).start()
            row_copy(r, slots2, 1).start(priority=1)
            return carry

        lax.fori_loop(0, tm, issue, 0, unroll=8)

    @pl.when(g == 0)
    def _():
        gather(s1_ref, s2_ref, cur)

    @pl.when(g < last)
    def _():
        gather(n1_ref, n2_ref, 1 - cur)

    for k in range(TOP_K):
        pltpu.make_async_copy(ys_hbm.at[pl.ds(0, tm), :], y_s.at[cur, k], sem.at[cur, k]).wait()

    is_ctx = ((pl.program_id(1) + tile0) * tm + lax.broadcasted_iota(jnp.int32, (tm, 1), 0)) < ctx_len
    f = rg_ref[:, 0:1] * y_s[cur, 0] + rg_ref[:, 1:2] * y_s[cur, 1]
    o_ref[...] = _deepnorm_ln(h_ref[...], f, _mod_rows(mod_ref, 5, is_ctx), g_ref[...], b_ref[...], alpha)


def moe_combine(ys, slot1, slot2, gates, h, modtab, ln_g, ln_b, alpha, ctx_len, tm, row0):
    b, t, d = h.shape
    assert row0 % tm == 0 and (t - row0) % tm == 0
    tile0 = row0 // tm
    n = (t - row0) // tm
    steps = b * n
    kern = functools.partial(_combine_kernel, alpha=alpha, ctx_len=ctx_len, tile0=tile0)
    cur_spec = pl.BlockSpec((None, 1, tm), lambda bi, i: (bi * n + i, 0, 0), memory_space=pltpu.SMEM)
    nxt_spec = pl.BlockSpec((None, 1, tm), lambda bi, i: (jnp.minimum(bi * n + i + 1, steps - 1), 0, 0),
                            memory_space=pltpu.SMEM)
    row = lambda w: pl.BlockSpec((None, tm, w), lambda bi, i: (bi, i + tile0, 0))
    s1 = slot1[:, row0:].reshape(steps, 1, tm)
    s2 = slot2[:, row0:].reshape(steps, 1, tm)
    return pl.pallas_call(
        kern,
        out_shape=jax.ShapeDtypeStruct((b, t - row0, d), F32),
        grid=(b, n),
        in_specs=[cur_spec, cur_spec, nxt_spec, nxt_spec, pl.BlockSpec(memory_space=pl.ANY), row(d),
                  pl.BlockSpec((None, 2, 6, d), lambda bi, i: (bi, 0, 0, 0)), row(LANES),
                  _const_spec((1, d)), _const_spec((1, d))],
        out_specs=pl.BlockSpec((None, tm, d), lambda bi, i: (bi, i, 0)),
        scratch_shapes=[pltpu.VMEM((2, TOP_K, tm, d), F32), pltpu.SemaphoreType.DMA((2, TOP_K))],
        compiler_params=_row_dma_params(),
        name="moe_combine",
    )(s1, s2, s1, s2, ys, h, modtab, gates, ln_g, ln_b)


def mix_moe_ffn(mixed, h, modtab, w_out, layer, ln0_g, ln0_b, router_w, router_b, w1, w3, w2, j, ln1_g, ln1_b,
                alpha, ctx_len, latent_only):
    b, t, d = h.shape
    n_experts = w1.shape[1]
    n_tiles = -(-(TOP_K * b * t) // MOE_TILE) + n_experts
    h, ri, gates, cnt = mix_moe_route(*mixed, h, modtab, w_out, layer, ln0_g, ln0_b, router_w, router_b, alpha,
                                      ctx_len)
    ln_g, ln_b = ln1_g, ln1_b
    counts = cnt[0, :n_experts].astype(jnp.int32)
    tiles_e = (counts + MOE_TILE - 1) // MOE_TILE
    ends = jnp.cumsum(tiles_e)
    offs = (ends - tiles_e) * MOE_TILE
    n_active = ends[-1:]
    tile_ids = jnp.minimum(jnp.arange(n_tiles, dtype=jnp.int32), n_active - 1)
    tile_expert = jnp.sum((ends[None, :] <= tile_ids[:, None]).astype(jnp.int32), axis=1)

    def slots(e, r):
        off = jnp.zeros_like(r)
        for k in range(n_experts):
            off = jnp.where(e == k, offs[k], off)
        return (off + r).reshape(b, t)

    slot1 = slots(ri[:, ROUTE_E1, :], ri[:, ROUTE_R1, :])
    slot2 = slots(ri[:, ROUTE_E2, :], ri[:, ROUTE_R2, :])
    xs = moe_dispatch(h, modtab, slot1, slot2, ends.astype(jnp.int32), n_tiles * MOE_TILE, ctx_len)
    ys = moe_grouped_ffn(xs, tile_expert, n_active.astype(jnp.int32), w1, w3, w2, j)
    tm, row0 = (TOK_TILE, ctx_len) if latent_only else (_row_tile(t), 0)
    return moe_combine(ys, slot1, slot2, gates, h, modtab, ln_g, ln_b, alpha, ctx_len, tm, row0)


def _rope_table(rows, ctx_len):
    def table(rot_dim):
        n_freq = rot_dim // 4
        inv = ROPE_BASE ** (-jnp.arange(n_freq, dtype=F32) / n_freq)
        row = jnp.repeat(jnp.arange(rows, dtype=F32), GRID_W)
        col = jnp.tile(jnp.arange(GRID_W, dtype=F32), rows)
        ang = jnp.concatenate([row[:, None] * inv, col[:, None] * inv], -1)
        cos, sin = jnp.cos(ang), jnp.sin(ang)
        reps = LANES // rot_dim
        cos_t = jnp.tile(jnp.concatenate([cos, cos], -1), (1, reps))
        sin_t = jnp.tile(jnp.concatenate([-sin, sin], -1), (1, reps))
        ident = (jnp.ones((ctx_len, LANES), F32), jnp.zeros((ctx_len, LANES), F32))
        return jnp.concatenate([ident[0], cos_t], 0), jnp.concatenate([ident[1], sin_t], 0)
    c64, s64 = table(HEAD_DIM)
    c32, s32 = table(DIFF_QK)
    return jnp.concatenate([c64, s64, c32, s32], axis=-1)


def _gate_dense(gate_w):
    cols = []
    for d in range(2):
        for g in range(2):
            cols.append(jax.scipy.linalg.block_diag(*[gate_w[d, g, k] for k in range(LRU_BLOCKS)]))
    return jnp.concatenate(cols, axis=1)


def kernel(x, c, ctx, c_ctx, ada_w, ada_b, w_in, w_out, lru_conv_w, lru_conv_b, lru_gate_w, lru_gate_b, lru_lam,
           swa_sink, diff_lam, diff_norm_g, ln_g, ln_b, ffn_w1, ffn_w3, ffn_w2, moe_router_w, moe_router_b,
           moe_w1, moe_w3, moe_w2):
    b, s, d = x.shape
    ctx_len = ctx.shape[1]
    depth = ada_w.shape[0]
    n_experts = moe_router_w.shape[-1]
    alpha = (2.0 * depth) ** 0.25
    assert s % GRID_W == 0 and s % TOK_TILE == 0 and ctx_len == TOK_TILE

    h = jnp.concatenate([ctx, x], axis=1)
    ropetab = _rope_table(s // GRID_W, ctx_len)

    rows = 8 * ((b + 1 + 7) // 8)
    cvec = jnp.zeros((rows, d), F32).at[:b].set(c).at[b].set(c_ctx)
    mods = ada_modulation(cvec, ada_w, ada_b).reshape(depth, rows, 6, d)
    modtab = jnp.stack([jnp.broadcast_to(mods[:, b:b + 1], (depth, b, 6, d)), mods[:, :b]], axis=2)

    sink_rows = jnp.repeat((swa_sink * math.log2(math.e)).reshape(depth, SWA_KV_HEADS, 1, SWA_GROUP), ATTN_BLOCK,
                           axis=-1)

    w_in, w_out, ffn_w1, ffn_w3, ffn_w2, moe_w1, moe_w3, moe_w2 = (
        w.astype(BF16) for w in (w_in, w_out, ffn_w1, ffn_w3, ffn_w2, moe_w1, moe_w3, moe_w2))

    for layer in range(depth):
        lam_init = 0.8 - 0.6 * math.exp(-0.3 * layer)
        mt = modtab[layer]
        lru_xg, attn_t = input_projection(h, mt, w_in, layer, ropetab)
        lru = rglru_mix(lru_xg, lru_conv_w[layer], lru_conv_b[layer].reshape(1, LRU_WIDTH),
                        _gate_dense(lru_gate_w[layer]).astype(BF16), lru_gate_b[layer].reshape(1, 4 * LRU_WIDTH),
                        lru_lam[layer], ctx_len)
        swa = windowed_attention(attn_t, sink_rows[layer])
        dif = differential_attention(attn_t, diff_lam[layer], diff_norm_g[layer], lam_init, layer < depth - 1)
        j = layer // 2
        ln0 = (ln_g[layer, 0:1], ln_b[layer, 0:1])
        ln1 = (ln_g[layer, 1:2], ln_b[layer, 1:2])
        if layer % 2 == 0:
            h = mix_dense_ffn(lru, swa, dif, h, mt, w_out, layer, *ln0, ffn_w1, ffn_w3, ffn_w2, j, *ln1, alpha, ctx_len)
        else:
            rw = jnp.zeros((d, LANES), F32).at[:, :n_experts].set(moe_router_w[j]).astype(BF16)
            rb = jnp.full((1, LANES), NEG_INF, F32).at[0, :n_experts].set(moe_router_b[j])
            h = mix_moe_ffn((lru, swa, dif), h, mt, w_out, layer, *ln0, rw, rb, moe_w1, moe_w3, moe_w2, j, *ln1,
                            alpha, ctx_len, latent_only=layer == depth - 1)
    return h if depth % 2 == 0 else h[:, ctx_len:, :]
```

```python
import functools
import math

import jax
import jax.numpy as jnp
from jax import lax
from jax.experimental import pallas as pl
from jax.experimental.pallas import tpu as pltpu

F32 = jnp.float32
BF16 = jnp.bfloat16

GRID_W = 64
HEAD_DIM = 64
LRU_WIDTH = 256
LRU_BLOCKS = 4
LRU_BLOCK = LRU_WIDTH // LRU_BLOCKS
CONV_W = 4
LRU_C = 8.0
SWA_Q_HEADS = 8
SWA_KV_HEADS = 2
SWA_GROUP = SWA_Q_HEADS // SWA_KV_HEADS
SWA_WIDTH = SWA_Q_HEADS * HEAD_DIM
ATTN_BLOCK = 128
DIFF_HEADS = 4
DIFF_QK = HEAD_DIM // 2
DIFF_WIDTH = DIFF_HEADS * HEAD_DIM
MIX_WIDTH = LRU_WIDTH + SWA_WIDTH + DIFF_WIDTH
TOP_K = 2
ROPE_BASE = 10000.0
LN_EPS = 1e-5
NEG_INF = -1e30
ATTN_MIN_SUM_LOG2 = -100.0
COL_LRU_X, COL_LRU_G, COL_SWA_Q, COL_SWA_K, COL_SWA_V, COL_DIFF_Q, COL_DIFF_K, COL_DIFF_V, IN_COLS = (
    0, 256, 512, 1024, 1152, 1280, 1536, 1792, 2048)
ATTN_COLS = IN_COLS - COL_SWA_Q
ROW_SWA_Q, ROW_SWA_KV, ROW_DIFF_Q, ROW_DIFF_K, ROW_DIFF_V = 0, 512, 768, 1024, 1280

LANES = 128
TOK_TILE = 256
VMEM_LIMIT = 56 * 1024 * 1024


def _cparams(sem, vmem=VMEM_LIMIT):
    return pltpu.CompilerParams(dimension_semantics=sem, vmem_limit_bytes=vmem)


def _const_spec(shape):
    nd = len(shape)
    return pl.BlockSpec(shape, lambda *_: (0,) * nd, pipeline_mode=pl.Buffered(1))


def _layer_spec(shape, layer):
    nd = len(shape)
    return pl.BlockSpec((None,) + tuple(shape), lambda *_: (layer,) + (0,) * nd, pipeline_mode=pl.Buffered(1))


def _ada_kernel(c_ref, w_ref, b_ref, o_ref):
    c = c_ref[...]
    s = (c * jax.nn.sigmoid(c)).astype(BF16)
    o_ref[...] = jnp.dot(s, w_ref[...].astype(BF16), preferred_element_type=F32) + b_ref[...]


def ada_modulation(cvec, ada_w, ada_b):
    depth, d, n = ada_w.shape
    r = cvec.shape[0]
    tn = 1536
    assert n % tn == 0
    return pl.pallas_call(
        _ada_kernel,
        out_shape=jax.ShapeDtypeStruct((depth, r, n), F32),
        grid=(depth, n // tn),
        in_specs=[pl.BlockSpec((r, d), lambda l, j: (0, 0)),
                  pl.BlockSpec((None, d, tn), lambda l, j: (l, 0, j)),
                  pl.BlockSpec((None, 1, tn), lambda l, j: (l, 0, j))],
        out_specs=pl.BlockSpec((None, r, tn), lambda l, j: (l, 0, j)),
        compiler_params=_cparams(("arbitrary", "arbitrary")),
        name="ada_modulation",
    )(cvec, ada_w, ada_b.reshape(depth, 1, n))


def _rope_group(x, cos, sin_signed, half, lane_mod):
    swapped = jnp.where(lane_mod < half, pltpu.roll(x, LANES - half, axis=1), pltpu.roll(x, half, axis=1))
    return x * cos + swapped * sin_signed


INPROJ_GROUP = 512


def _inproj_kernel(h_ref, mod_ref, w_ref, rope_ref, lru_ref, attn_ref, *, swa_scale, diff_scale):
    h = h_ref[...]
    u = (h * (1.0 + mod_ref[1:2, :]) + mod_ref[0:1, :]).astype(BF16)
    lane = lax.broadcasted_iota(jnp.int32, (1, LANES), 1)
    cos64, sin64 = rope_ref[:, 0:LANES], rope_ref[:, LANES:2 * LANES]
    cos32, sin32 = rope_ref[:, 2 * LANES:3 * LANES], rope_ref[:, 3 * LANES:4 * LANES]
    mod64, mod32 = lane % HEAD_DIM, lane % DIFF_QK
    for g0 in list(range(COL_SWA_Q, IN_COLS, INPROJ_GROUP)) + list(range(0, COL_SWA_Q, INPROJ_GROUP)):
        acc = jnp.dot(u, w_ref[:, g0:g0 + INPROJ_GROUP], preferred_element_type=F32)
        if g0 < COL_SWA_Q:
            lru_ref[:, g0:g0 + INPROJ_GROUP] = acc
            continue
        for c0 in range(g0, g0 + INPROJ_GROUP, LANES):
            x = acc[:, c0 - g0:c0 - g0 + LANES]
            if c0 < COL_SWA_V:
                x = _rope_group(x, cos64, sin64, HEAD_DIM // 2, mod64)
                if c0 < COL_SWA_K:
                    x = x * swa_scale
            elif COL_DIFF_Q <= c0 < COL_DIFF_V:
                x = _rope_group(x, cos32, sin32, DIFF_QK // 2, mod32)
                if c0 < COL_DIFF_K:
                    x = x * diff_scale
            r0 = c0 - COL_SWA_Q
            attn_ref[r0:r0 + LANES, :] = x.astype(BF16).T


def input_projection(h, modtab, w_in, layer, ropetab):
    b, t, d = h.shape
    tm = TOK_TILE
    nt = t // tm
    log2e = math.log2(math.e)
    kern = functools.partial(_inproj_kernel, swa_scale=HEAD_DIM ** -0.5 * log2e, diff_scale=DIFF_QK ** -0.5 * log2e)
    return pl.pallas_call(
        kern,
        out_shape=(jax.ShapeDtypeStruct((b, t, COL_SWA_Q), F32),
                   jax.ShapeDtypeStruct((b, nt, ATTN_COLS, tm), BF16)),
        grid=(b, nt),
        in_specs=[pl.BlockSpec((None, tm, d), lambda bi, i: (bi, i, 0)),
                  pl.BlockSpec((None, None, 6, d), lambda bi, i: (bi, jnp.minimum(i, 1), 0, 0)),
                  _layer_spec((d, IN_COLS), layer),
                  pl.BlockSpec((tm, 4 * LANES), lambda bi, i: (i, 0))],
        out_specs=(pl.BlockSpec((None, tm, COL_SWA_Q), lambda bi, i: (bi, i, 0)),
                   pl.BlockSpec((None, None, ATTN_COLS, tm), lambda bi, i: (bi, i, 0, 0))),
        compiler_params=_cparams(("arbitrary", "arbitrary")),
        name="input_projection",
    )(h, modtab, w_in, ropetab)


def _neg_expm1(z, exp_z):
    series = z * (1.0 + z * (1 / 2 + z * (1 / 6 + z * (1 / 24 + z * (1 / 120 + z * (1 / 720))))))
    return jnp.where(z > -1.0 / 32, -series, 1.0 - exp_z)


def _lru_kernel(xg_ref, cw_ref, cb_ref, wg_ref, gb_ref, lam_ref, o_ref, a_s, b_s, h_s, *, chunk, ctx_chunks):
    t, c = o_ref.shape
    nchunk = t // chunk
    row = lax.broadcasted_iota(jnp.int32, (chunk, 1), 0)
    sp = jax.nn.softplus(-lam_ref[...])
    cw = cw_ref[...]
    cb = cb_ref[...]

    def coeffs(j, carry):
        r0 = pl.multiple_of(j * chunk, chunk)
        x = xg_ref[pl.ds(r0, chunk), 0:c]
        prev8 = xg_ref[pl.ds(pl.multiple_of(jnp.maximum(r0 - 8, 0), 8), 8), 0:c]
        next8 = xg_ref[pl.ds(pl.multiple_of(jnp.minimum(r0 + chunk, t - 8), 8), 8), 0:c]
        seg_start = jnp.logical_or(j == 0, j == ctx_chunks)
        seg_end = jnp.logical_or(j == ctx_chunks - 1, j == nchunk - 1)
        pm1 = jnp.where(seg_start, 0.0, prev8[7:8, :])
        np0 = jnp.where(seg_end, 0.0, next8[0:1, :])
        np1 = jnp.where(seg_end, 0.0, next8[1:2, :])
        xm1 = jnp.where(row == 0, pm1, pltpu.roll(x, 1, axis=0))
        xp1 = jnp.where(row == chunk - 1, np0, pltpu.roll(x, chunk - 1, axis=0))
        xp2 = jnp.where(row == chunk - 1, np1, jnp.where(row == chunk - 2, np0, pltpu.roll(x, chunk - 2, axis=0)))
        u = cw[0:1, :] * xm1 + cw[1:2, :] * x + cw[2:3, :] * xp1 + cw[3:4, :] * xp2 + cb
        g = jnp.dot(u.astype(BF16), wg_ref[...], preferred_element_type=F32) + gb_ref[...]
        for d in range(2):
            r = jax.nn.sigmoid(g[:, (2 * d) * c:(2 * d + 1) * c])
            i = jax.nn.sigmoid(g[:, (2 * d + 1) * c:(2 * d + 2) * c])
            log_a = (-LRU_C) * r * sp[d:d + 1, :]
            a = jnp.exp(log_a)
            a_s[d, pl.ds(r0, chunk), :] = a
            b_s[d, pl.ds(r0, chunk), :] = jnp.sqrt(_neg_expm1(2.0 * log_a, a * a)) * (i * u)
        return carry

    lax.fori_loop(0, nchunk, coeffs, 0)

    sub = lax.broadcasted_iota(jnp.int32, (8, 1), 0)

    def tile_scan(a, b, carry, reverse):
        for s in (1, 2, 4):
            shift, seen = (8 - s, sub < 8 - s) if reverse else (s, sub >= s)
            a_prev = jnp.where(seen, pltpu.roll(a, shift, axis=0), 1.0)
            b_prev = jnp.where(seen, pltpu.roll(b, shift, axis=0), 0.0)
            b = a * b_prev + b
            a = a * a_prev
        h = a * carry + b
        return h, (h[0:1, :] if reverse else h[7:8, :])

    def scan_segment(lo, n8, hf, hb):
        def body(k, hh):
            hf, hb = hh
            f0 = pl.multiple_of(lo + k * 8, 8)
            r0 = pl.multiple_of(lo + (n8 - 1 - k) * 8, 8)
            fw, hf = tile_scan(a_s[0, pl.ds(f0, 8), :], b_s[0, pl.ds(f0, 8), :], hf, False)
            bw, hb = tile_scan(a_s[1, pl.ds(r0, 8), :], b_s[1, pl.ds(r0, 8), :], hb, True)
            h_s[0, pl.ds(f0, 8), :] = fw
            h_s[1, pl.ds(r0, 8), :] = bw
            return hf, hb
        return lax.fori_loop(0, n8, body, (hf, hb))

    zero = jnp.zeros((1, c), F32)
    ctx_rows = ctx_chunks * chunk
    hf, hb = scan_segment(0, ctx_rows // 8, zero, zero)
    scan_segment(ctx_rows, (t - ctx_rows) // 8, hf, hb)

    def finish(j, carry):
        r0 = pl.multiple_of(j * chunk, chunk)
        y = h_s[0, pl.ds(r0, chunk), :] + h_s[1, pl.ds(r0, chunk), :]
        gate = xg_ref[pl.ds(r0, chunk), c:2 * c]
        o_ref[pl.ds(r0, chunk), :] = (y * jax.nn.gelu(gate)).astype(o_ref.dtype)
        return carry

    lax.fori_loop(0, nchunk, finish, 0)


def rglru_mix(lru_xg, conv_w, conv_b, gate_dense, gate_b, lam, ctx_len):
    b, t, c2 = lru_xg.shape
    c = c2 // 2
    chunk = TOK_TILE
    assert t % chunk == 0 and ctx_len % chunk == 0
    kern = functools.partial(_lru_kernel, chunk=chunk, ctx_chunks=ctx_len // chunk)
    return pl.pallas_call(
        kern,
        out_shape=jax.ShapeDtypeStruct((b, t, c), BF16),
        grid=(b,),
        in_specs=[pl.BlockSpec((None, t, c2), lambda bi: (bi, 0, 0), pipeline_mode=pl.Buffered(1)),
                  _const_spec((CONV_W, c)), _const_spec((1, c)), _const_spec((c, 4 * c)), _const_spec((1, 4 * c)),
                  _const_spec((2, c))],
        out_specs=pl.BlockSpec((None, t, c), lambda bi: (bi, 0, 0)),
        scratch_shapes=[pltpu.VMEM((2, t, c), F32), pltpu.VMEM((2, t, c), F32), pltpu.VMEM((2, t, c), F32)],
        compiler_params=_cparams(("arbitrary",)),
        name="rglru_mix",
    )(lru_xg, conv_w, conv_b, gate_dense, gate_b, lam)


def _dot_t0(a, b):
    return lax.dot_general(a, b, (((0,), (0,)), ((), ())), preferred_element_type=F32)


SWA_MAX_KEY_BLOCKS = 5


def _swa_kernel(q_ref, own_ref, prev_ref, next_ref, ctx_ref, sink_ref, o_ref, s_s, p_s, l_s, ot_s):
    i = pl.program_id(1)
    nt = pl.num_programs(1)
    blk = ATTN_BLOCK
    nq = SWA_GROUP * blk
    kp = lax.broadcasted_iota(jnp.int32, (blk, nq), 0)
    qp = lax.broadcasted_iota(jnp.int32, (blk, nq), 1) % blk

    def kv(ref, hk, half):
        k = ref[hk * HEAD_DIM:(hk + 1) * HEAD_DIM, half * blk:(half + 1) * blk]
        v = ref[SWA_KV_HEADS * HEAD_DIM + hk * HEAD_DIM:SWA_KV_HEADS * HEAD_DIM + (hk + 1) * HEAD_DIM,
                half * blk:(half + 1) * blk]
        return k, v

    def q_tile(hk, half):
        return jnp.concatenate(
            [q_ref[(hk * SWA_GROUP + g) * HEAD_DIM:(hk * SWA_GROUP + g + 1) * HEAD_DIM,
                   half * blk:(half + 1) * blk] for g in range(SWA_GROUP)], axis=1)

    def put(hk, half, o):
        for g in range(SWA_GROUP):
            f0 = (hk * SWA_GROUP + g) * HEAD_DIM
            ot_s[f0:f0 + HEAD_DIM, half * blk:(half + 1) * blk] = o[:, g * blk:(g + 1) * blk]

    def run_bounded(jobs):
        ones = jnp.ones((16, SWA_MAX_KEY_BLOCKS * blk), BF16)
        for a, (hk, half, keys, _, masks) in enumerate(jobs):
            qt = q_tile(hk, half)
            qf = qt.astype(F32)
            kmax2 = jnp.zeros((1, 1), F32)
            for kt in keys:
                kf = kt.astype(F32)
                kmax2 = jnp.maximum(kmax2, jnp.max(jnp.sum(kf * kf, axis=0, keepdims=True), axis=1, keepdims=True))
            ref = jnp.maximum(jnp.sqrt(jnp.sum(qf * qf, axis=0, keepdims=True) * kmax2), sink_ref[hk])
            l_s[a:a + 1, :] = ref
            for j, (kt, msk) in enumerate(zip(keys, masks)):
                sj = _dot_t0(kt, qt)
                if msk is not None:
                    sj = jnp.where(msk, sj, NEG_INF)
                p_s[a, j * blk:(j + 1) * blk, :] = jnp.exp2(sj - ref).astype(BF16)
        worst = None
        for a, (hk, half, keys, vals, _) in enumerate(jobs):
            nk = len(keys) * blk
            v1 = jnp.concatenate([jnp.concatenate(vals, axis=1), ones[:, 0:nk]], axis=0)
            oe = jnp.dot(v1, p_s[a, 0:nk, :], preferred_element_type=F32)
            l = oe[HEAD_DIM:HEAD_DIM + 1, :] + jnp.exp2(sink_ref[hk] - l_s[a:a + 1, :])
            put(hk, half, oe[0:HEAD_DIM, :] / l)
            worst = l if worst is None else jnp.minimum(worst, l)
        return jnp.min(worst) < 2.0 ** ATTN_MIN_SUM_LOG2

    def run_exact(jobs):
        for a, (hk, half, keys, _, _) in enumerate(jobs):
            qt = q_tile(hk, half)
            for j, kt in enumerate(keys):
                s_s[a, j * blk:(j + 1) * blk, :] = _dot_t0(kt, qt)
        for a, (hk, _, keys, _, masks) in enumerate(jobs):
            sink = sink_ref[hk]
            s = []
            for j, msk in enumerate(masks):
                sj = s_s[a, j * blk:(j + 1) * blk, :]
                s.append(sj if msk is None else jnp.where(msk, sj, NEG_INF))
            m = sink
            for sj in s:
                m = jnp.maximum(m, jnp.max(sj, axis=0, keepdims=True))
            l = jnp.exp2(sink - m)
            for j, sj in enumerate(s):
                pj = jnp.exp2(sj - m)
                l = l + jnp.sum(pj, axis=0, keepdims=True)
                p_s[a, j * blk:(j + 1) * blk, :] = pj.astype(BF16)
            l_s[a:a + 1, :] = l
        for a, (hk, half, keys, vals, _) in enumerate(jobs):
            nk = len(keys) * blk
            vt = jnp.concatenate(vals, axis=1)
            put(hk, half, jnp.dot(vt, p_s[a, 0:nk, :], preferred_element_type=F32) / l_s[a:a + 1, :])

    def run(jobs):
        lost_range = run_bounded(jobs)

        @pl.when(lost_range)
        def _():
            run_exact(jobs)

    @pl.when(i == 0)
    def _():
        jobs = []
        for hk in range(SWA_KV_HEADS):
            k0, v0 = kv(ctx_ref, hk, 0)
            k1, v1 = kv(ctx_ref, hk, 1)
            for half in range(2):
                jobs.append((hk, half, [k0, k1], [v0, v1], [None, None]))
        run(jobs)

    @pl.when(i > 0)
    def _():
        has_prev = i > 1
        has_next = i < nt - 1
        m_prev = kp >= qp
        m_next = kp <= qp
        jobs = []
        for hk in range(SWA_KV_HEADS):
            c0k, c0v = kv(ctx_ref, hk, 0)
            c1k, c1v = kv(ctx_ref, hk, 1)
            o0k, o0v = kv(own_ref, hk, 0)
            o1k, o1v = kv(own_ref, hk, 1)
            pk, pv = kv(prev_ref, hk, 1)
            nk, nv = kv(next_ref, hk, 0)
            jobs.append((hk, 0, [c0k, c1k, pk, o0k, o1k], [c0v, c1v, pv, o0v, o1v],
                         [None, None, jnp.logical_and(m_prev, has_prev), None, m_next]))
            jobs.append((hk, 1, [c0k, c1k, o0k, o1k, nk], [c0v, c1v, o0v, o1v, nv],
                         [None, None, m_prev, None, jnp.logical_and(m_next, has_next)]))
        run(jobs)

    o_ref[...] = ot_s[...].T.astype(o_ref.dtype)


def windowed_attention(attn_t, sink_rows):
    b, nt, _, tm = attn_t.shape
    kvrows = 2 * SWA_KV_HEADS * HEAD_DIM
    kvblk = ROW_SWA_KV // kvrows
    assert ROW_SWA_KV % kvrows == 0 and tm == 2 * ATTN_BLOCK
    return pl.pallas_call(
        _swa_kernel,
        out_shape=jax.ShapeDtypeStruct((b, nt * tm, SWA_WIDTH), BF16),
        grid=(b, nt),
        in_specs=[pl.BlockSpec((None, None, SWA_WIDTH, tm), lambda bi, i: (bi, i, 0, 0)),
                  pl.BlockSpec((None, None, kvrows, tm), lambda bi, i: (bi, i, kvblk, 0)),
                  pl.BlockSpec((None, None, kvrows, tm), lambda bi, i: (bi, jnp.maximum(i - 1, 1), kvblk, 0)),
                  pl.BlockSpec((None, None, kvrows, tm), lambda bi, i: (bi, jnp.minimum(i + 1, nt - 1), kvblk, 0)),
                  pl.BlockSpec((None, None, kvrows, tm), lambda bi, i: (bi, 0, kvblk, 0)),
                  pl.BlockSpec((SWA_KV_HEADS, 1, SWA_GROUP * ATTN_BLOCK), lambda bi, i: (0, 0, 0))],
        out_specs=pl.BlockSpec((None, tm, SWA_WIDTH), lambda bi, i: (bi, i, 0)),
        scratch_shapes=[pltpu.VMEM((2 * SWA_KV_HEADS, SWA_MAX_KEY_BLOCKS * ATTN_BLOCK, SWA_GROUP * ATTN_BLOCK), F32),
                        pltpu.VMEM((2 * SWA_KV_HEADS, SWA_MAX_KEY_BLOCKS * ATTN_BLOCK, SWA_GROUP * ATTN_BLOCK), BF16),
                        pltpu.VMEM((2 * SWA_KV_HEADS, SWA_GROUP * ATTN_BLOCK), F32),
                        pltpu.VMEM((SWA_WIDTH, tm), F32)],
        compiler_params=_cparams(("arbitrary", "arbitrary")),
        name="windowed_attention",
    )(attn_t, attn_t, attn_t, attn_t, attn_t, sink_rows)


DIFF_ACC_ROWS = HEAD_DIM + 16
DIFF_Q_TILES = 4


def _diff_kernel(q_ref, k_ref, v_ref, lam_ref, g_ref, o_ref, m_s, a_s, acc_s, s_s, p_s, ot_s, kn_s, *, lam_init):
    nkv = k_ref.shape[0]
    last = nkv - 1
    ncomp = 2 * DIFF_HEADS
    ones = jnp.ones((DIFF_ACC_ROWS - HEAD_DIM, k_ref.shape[-1]), BF16)

    def q_comp(c):
        return jnp.concatenate([q_ref[t, c * DIFF_QK:(c + 1) * DIFF_QK, :] for t in range(q_ref.shape[0])], axis=1)

    def add_values(j, slot, rescale):
        vt = v_ref[j]
        for c in range(ncomp):
            h = c // 2
            v1 = jnp.concatenate([vt[h * HEAD_DIM:(h + 1) * HEAD_DIM, :], ones], axis=0)
            old = a_s[slot, c:c + 1, :] * acc_s[c] if rescale else acc_s[c]
            acc_s[c] = old + jnp.dot(v1, p_s[slot, c], preferred_element_type=F32)

    def key_norms(j, mx):
        kf = k_ref[j].astype(F32)
        sq = kf * kf
        n2 = jnp.concatenate([jnp.sum(sq[c * DIFF_QK:(c + 1) * DIFF_QK, :], axis=0, keepdims=True)
                              for c in range(ncomp)], axis=0)
        return jnp.maximum(mx, n2)

    @pl.when(pl.program_id(1) == 0)
    def _():
        kn_s[...] = lax.fori_loop(0, nkv, key_norms, jnp.zeros(kn_s.shape, F32))

    kmax2 = jnp.max(kn_s[...], axis=1, keepdims=True)
    for c in range(ncomp):
        qf = q_comp(c).astype(F32)
        m_s[c:c + 1, :] = jnp.sqrt(jnp.sum(qf * qf, axis=0, keepdims=True) * kmax2[c:c + 1, :])
    acc_s[...] = jnp.zeros(acc_s.shape, F32)

    def bounded_probs(j, slot):
        kt = k_ref[j]
        for c in range(ncomp):
            s = _dot_t0(kt[c * DIFF_QK:(c + 1) * DIFF_QK, :], q_comp(c))
            p_s[slot, c] = jnp.exp2(s - m_s[c:c + 1, :]).astype(BF16)

    bounded_probs(0, 0)

    def bounded_pair(jj, carry):
        j = 2 * jj
        bounded_probs(j + 1, 1)
        add_values(j, 0, False)
        bounded_probs(j + 2, 0)
        add_values(j + 1, 1, False)
        return carry

    lax.fori_loop(0, last // 2, bounded_pair, 0)
    add_values(last, 0, False)

    min_sum = acc_s[0, HEAD_DIM:HEAD_DIM + 1, :]
    for c in range(1, ncomp):
        min_sum = jnp.minimum(min_sum, acc_s[c, HEAD_DIM:HEAD_DIM + 1, :])
    lost_range = jnp.min(min_sum) < 2.0 ** ATTN_MIN_SUM_LOG2

    def scores(j, slot):
        kt = k_ref[j]
        for c in range(ncomp):
            s_s[slot, c] = _dot_t0(kt[c * DIFF_QK:(c + 1) * DIFF_QK, :], q_comp(c))

    def softmax(slot):
        for c in range(ncomp):
            s = s_s[slot, c]
            m_old = m_s[c:c + 1, :]
            m_new = jnp.maximum(m_old, jnp.max(s, axis=0, keepdims=True))
            m_s[c:c + 1, :] = m_new
            a_s[slot, c:c + 1, :] = jnp.exp2(m_old - m_new)
            p_s[slot, c] = jnp.exp2(s - m_new).astype(BF16)

    @pl.when(lost_range)
    def _():
        m_s[...] = jnp.full(m_s.shape, NEG_INF, F32)
        acc_s[...] = jnp.zeros(acc_s.shape, F32)
        scores(0, 0)
        scores(min(1, last), 1)
        softmax(0)

        def pair(jj, carry):
            j = 2 * jj
            scores(j + 2, 0)
            softmax(1)
            add_values(j, 0, True)
            scores(jnp.minimum(j + 3, last), 1)
            softmax(0)
            add_values(j + 1, 1, True)
            return carry

        lax.fori_loop(0, last // 2, pair, 0)
        add_values(last, 0, True)

    lam_v = lam_ref[...]
    lam = (jnp.exp(jnp.sum(lam_v[0:1, :] * lam_v[1:2, :], axis=1, keepdims=True))
           - jnp.exp(jnp.sum(lam_v[2:3, :] * lam_v[3:4, :], axis=1, keepdims=True)) + lam_init)

    def head_out(c):
        return acc_s[c, 0:HEAD_DIM, :] / acc_s[c, HEAD_DIM:HEAD_DIM + 1, :]

    for h in range(DIFF_HEADS):
        o = head_out(2 * h) - lam * head_out(2 * h + 1)
        o = o * lax.rsqrt(jnp.mean(jnp.square(o), axis=0, keepdims=True) + LN_EPS)
        ot_s[h * HEAD_DIM:(h + 1) * HEAD_DIM, :] = (o * (1.0 - lam_init)) * g_ref[...]
    o_ref[...] = ot_s[...].T.astype(o_ref.dtype)


def _diff_call(q_tiles, q_feat_blk, kv, nk, nq, diff_lam, norm_g, lam_init):
    b, nqt, _, tm = q_tiles.shape
    assert nk % 2 == 1 and nqt % nq == 0
    tq = nq * tm
    ncomp = 2 * DIFF_HEADS
    kern = functools.partial(_diff_kernel, lam_init=lam_init)
    return pl.pallas_call(
        kern,
        out_shape=jax.ShapeDtypeStruct((b, nqt * tm, DIFF_WIDTH), BF16),
        grid=(b, nqt // nq),
        in_specs=[pl.BlockSpec((None, nq, DIFF_WIDTH, tm), lambda bi, i: (bi, i, q_feat_blk, 0)),
                  pl.BlockSpec((None, nk, DIFF_WIDTH, tm), lambda bi, i: (bi, 0, ROW_DIFF_K // DIFF_WIDTH, 0)),
                  pl.BlockSpec((None, nk, DIFF_WIDTH, tm), lambda bi, i: (bi, 0, ROW_DIFF_V // DIFF_WIDTH, 0)),
                  pl.BlockSpec((4, DIFF_QK), lambda bi, i: (0, 0)),
                  pl.BlockSpec((HEAD_DIM, 1), lambda bi, i: (0, 0))],
        out_specs=pl.BlockSpec((None, tq, DIFF_WIDTH), lambda bi, i: (bi, i, 0)),
        scratch_shapes=[pltpu.VMEM((ncomp, tq), F32), pltpu.VMEM((2, ncomp, tq), F32),
                        pltpu.VMEM((ncomp, DIFF_ACC_ROWS, tq), F32), pltpu.VMEM((2, ncomp, tm, tq), F32),
                        pltpu.VMEM((2, ncomp, tm, tq), BF16), pltpu.VMEM((DIFF_WIDTH, tq), F32),
                        pltpu.VMEM((ncomp, tm), F32)],
        compiler_params=_cparams(("arbitrary", "arbitrary")),
        name="differential_attention",
    )(q_tiles, kv, kv, diff_lam, norm_g.reshape(HEAD_DIM, 1))


def differential_attention(attn_t, diff_lam, norm_g, lam_init, need_ctx):
    b, nt, _, tm = attn_t.shape
    q_lat = attn_t[:, 1:, ROW_DIFF_Q:ROW_DIFF_Q + DIFF_WIDTH, :]
    lat = _diff_call(q_lat, 0, attn_t, nt, math.gcd(DIFF_Q_TILES, nt - 1), diff_lam, norm_g, lam_init)
    if need_ctx:
        ctx = _diff_call(attn_t[:, 0:1], ROW_DIFF_Q // DIFF_WIDTH, attn_t, 1, 1, diff_lam, norm_g, lam_init)
    else:
        ctx = jnp.zeros((b, tm, DIFF_WIDTH), BF16)
    return jnp.concatenate([ctx, lat], axis=1)


def _mod_rows(mod_ref, k, is_ctx):
    return jnp.where(is_ctx, mod_ref[0, k:k + 1, :], mod_ref[1, k:k + 1, :])


def _is_ctx_rows(tm, ctx_len):
    r0 = pl.program_id(1) * tm
    return (r0 + lax.broadcasted_iota(jnp.int32, (tm, 1), 0)) < ctx_len


def _deepnorm_ln(h, branch, gate, g, b, alpha):
    x = alpha * h + gate * branch
    mu = jnp.mean(x, axis=-1, keepdims=True)
    xc = x - mu
    var = jnp.mean(xc * xc, axis=-1, keepdims=True)
    return xc * lax.rsqrt(var + LN_EPS) * g + b


def _mixed_projection(lru_ref, swa_ref, diff_ref, w_ref, rows=slice(None)):
    m = jnp.dot(lru_ref[rows, :], w_ref[0:LRU_WIDTH, :], preferred_element_type=F32)
    m = m + jnp.dot(swa_ref[rows, :], w_ref[LRU_WIDTH:LRU_WIDTH + SWA_WIDTH, :], preferred_element_type=F32)
    return m + jnp.dot(diff_ref[rows, :], w_ref[LRU_WIDTH + SWA_WIDTH:MIX_WIDTH, :], preferred_element_type=F32)


def _row_tile(t):
    for tm in range(640, 15, -16):
        if t % tm == 0:
            return tm
    raise ValueError(f"no row tile for T={t}")


def _token_mix_specs(tm, d, layer):
    row = lambda w: pl.BlockSpec((None, tm, w), lambda bi, i: (bi, i, 0))
    return [row(LRU_WIDTH), row(SWA_WIDTH), row(DIFF_WIDTH), row(d),
            pl.BlockSpec((None, 2, 6, d), lambda bi, i: (bi, 0, 0, 0)),
            _layer_spec((MIX_WIDTH, d), layer), _const_spec((1, d)), _const_spec((1, d))]


FF_CHUNK = 256


def _swiglu_hidden(u, w1_ref, w3_ref, g_s):
    ff = w1_ref.shape[-1]
    for c0 in range(0, ff, FF_CHUNK):
        h1 = jnp.dot(u, w1_ref[:, c0:c0 + FF_CHUNK], preferred_element_type=F32)
        h3 = jnp.dot(u, w3_ref[:, c0:c0 + FF_CHUNK], preferred_element_type=F32)
        g_s[:, c0:c0 + FF_CHUNK] = ((h1 * jax.nn.sigmoid(h1)) * h3).astype(BF16)


def _mix_ffn_kernel(lru_ref, swa_ref, diff_ref, h_ref, mod_ref, wo_ref, g0_ref, b0_ref, w1_ref, w3_ref, w2_ref,
                    g1_ref, b1_ref, o_ref, g_s, *, alpha, ctx_len):
    tm = h_ref.shape[0]
    half = tm // 2
    ctx_rows = _is_ctx_rows(tm, ctx_len)
    rows = [slice(r * half, (r + 1) * half) for r in range(2)]
    is_ctx = [ctx_rows[rw, :] for rw in rows]
    m = [_mixed_projection(lru_ref, swa_ref, diff_ref, wo_ref, rw) for rw in rows]
    h = [None, None]
    for r in range(2):
        h[r] = _deepnorm_ln(h_ref[rows[r], :], m[r], _mod_rows(mod_ref, 2, is_ctx[r]), g0_ref[...], b0_ref[...], alpha)
        u = (h[r] * (1.0 + _mod_rows(mod_ref, 4, is_ctx[r])) + _mod_rows(mod_ref, 3, is_ctx[r])).astype(BF16)
        _swiglu_hidden(u, w1_ref, w3_ref, g_s.at[r])
    for r in range(2):
        f = jnp.dot(g_s[r], w2_ref[...], preferred_element_type=F32)
        o_ref[rows[r], :] = _deepnorm_ln(h[r], f, _mod_rows(mod_ref, 5, is_ctx[r]), g1_ref[...], b1_ref[...], alpha)


def mix_dense_ffn(lru, swa, diff, h, modtab, w_out, layer, ln0_g, ln0_b, w1, w3, w2, j, ln1_g, ln1_b, alpha, ctx_len):
    b, t, d = h.shape
    ff = w1.shape[-1]
    assert ff % FF_CHUNK == 0
    tm = _row_tile(t)
    kern = functools.partial(_mix_ffn_kernel, alpha=alpha, ctx_len=ctx_len)
    return pl.pallas_call(
        kern,
        out_shape=jax.ShapeDtypeStruct((b, t, d), F32),
        grid=(b, t // tm),
        in_specs=_token_mix_specs(tm, d, layer) + [
            _layer_spec((d, ff), j), _layer_spec((d, ff), j), _layer_spec((ff, d), j),
            _const_spec((1, d)), _const_spec((1, d))],
        out_specs=pl.BlockSpec((None, tm, d), lambda bi, i: (bi, i, 0)),
        scratch_shapes=[pltpu.VMEM((2, tm // 2, ff), BF16)],
        compiler_params=_cparams(("arbitrary", "arbitrary")),
        name="mix_dense_ffn",
    )(lru, swa, diff, h, modtab, w_out, ln0_g, ln0_b, w1, w3, w2, ln1_g, ln1_b)


MOE_TILE = 512
ROUTE_E1, ROUTE_E2, ROUTE_R1, ROUTE_R2 = 0, 1, 2, 3


def _mix_route_kernel(lru_ref, swa_ref, diff_ref, h_ref, mod_ref, wo_ref, g0_ref, b0_ref, rw_ref, rb_ref,
                      h1_ref, ri_ref, rg_ref, cnt_ref, cnt_s, *, alpha, ctx_len):
    tm = h_ref.shape[0]
    half = tm // 2
    ctx_rows = _is_ctx_rows(tm, ctx_len)

    @pl.when(jnp.logical_and(pl.program_id(0) == 0, pl.program_id(1) == 0))
    def _():
        cnt_s[...] = jnp.zeros(cnt_s.shape, F32)

    halves = [slice(r * half, (r + 1) * half) for r in range(2)]
    mixed = [_mixed_projection(lru_ref, swa_ref, diff_ref, wo_ref, rw) for rw in halves]
    lane = lax.broadcasted_iota(jnp.int32, (half, LANES), 1)
    before = lax.broadcasted_iota(jnp.int32, (half, half), 1) < lax.broadcasted_iota(jnp.int32, (half, half), 0)
    before = jnp.where(before, 1.0, 0.0).astype(BF16)
    count = cnt_s[...]
    tables = []
    for rw, m in zip(halves, mixed):
        is_ctx = ctx_rows[rw, :]
        h = _deepnorm_ln(h_ref[rw, :], m, _mod_rows(mod_ref, 2, is_ctx), g0_ref[...], b0_ref[...], alpha)
        h1_ref[rw, :] = h
        u = (h * (1.0 + _mod_rows(mod_ref, 4, is_ctx)) + _mod_rows(mod_ref, 3, is_ctx)).astype(BF16)
        logits = jnp.dot(u, rw_ref[...], preferred_element_type=F32) + rb_ref[...]
        m1 = jnp.max(logits, axis=-1, keepdims=True)
        i1 = jnp.min(jnp.where(logits == m1, lane, LANES), axis=-1, keepdims=True)
        rest = jnp.where(lane == i1, -jnp.inf, logits)
        m2 = jnp.max(rest, axis=-1, keepdims=True)
        i2 = jnp.min(jnp.where(rest == m2, lane, LANES), axis=-1, keepdims=True)
        e = jnp.exp(m2 - m1)
        g1 = 1.0 / (1.0 + e)
        g2 = e / (1.0 + e)
        sel1 = lane == i1
        sel2 = lane == i2
        picked = jnp.where(jnp.logical_or(sel1, sel2), 1.0, 0.0)
        rank = count + jnp.dot(before, picked.astype(BF16), preferred_element_type=F32)
        r1 = jnp.sum(jnp.where(sel1, rank, 0.0), axis=-1, keepdims=True)
        r2 = jnp.sum(jnp.where(sel2, rank, 0.0), axis=-1, keepdims=True)
        count = count + jnp.sum(picked, axis=0, keepdims=True)
        rg_ref[rw, :] = jnp.where(lane == 0, g1, jnp.where(lane == 1, g2, 0.0))
        r1_hi = jnp.floor(r1 * (1.0 / 256.0))
        r2_hi = jnp.floor(r2 * (1.0 / 256.0))
        cols = (i1.astype(F32), i2.astype(F32), r1_hi, r1 - 256.0 * r1_hi, r2_hi, r2 - 256.0 * r2_hi)
        table = jnp.zeros(logits.shape, F32)
        for k, col in enumerate(cols):
            table = jnp.where(lane == k, col, table)
        tables.append(table)
    cnt_s[...] = count
    cnt_ref[...] = count
    table = jnp.concatenate(tables, axis=0)
    pick = (lax.broadcasted_iota(jnp.int32, (8, LANES), 0) == lax.broadcasted_iota(jnp.int32, (8, LANES), 1))
    rows = lax.dot_general(jnp.where(pick, 1.0, 0.0).astype(BF16), table.astype(BF16),
                           (((1,), (1,)), ((), ())), preferred_element_type=F32)
    out = jnp.concatenate([rows[0:2], 256.0 * rows[2:3] + rows[3:4], 256.0 * rows[4:5] + rows[5:6],
                           jnp.zeros((4, tm), F32)], axis=0)
    ri_ref[...] = out.astype(jnp.int32)


def mix_moe_route(lru, swa, diff, h, modtab, w_out, layer, ln0_g, ln0_b, router_w, router_b, alpha, ctx_len):
    b, t, d = h.shape
    tm = _row_tile(t)
    nt = t // tm
    kern = functools.partial(_mix_route_kernel, alpha=alpha, ctx_len=ctx_len)
    row = lambda w: pl.BlockSpec((None, tm, w), lambda bi, i: (bi, i, 0))
    return pl.pallas_call(
        kern,
        out_shape=(jax.ShapeDtypeStruct((b, t, d), F32), jax.ShapeDtypeStruct((b * nt, 8, tm), jnp.int32),
                   jax.ShapeDtypeStruct((b, t, LANES), F32), jax.ShapeDtypeStruct((1, LANES), F32)),
        grid=(b, nt),
        in_specs=_token_mix_specs(tm, d, layer) + [_const_spec((d, LANES)), _const_spec((1, LANES))],
        out_specs=(row(d), pl.BlockSpec((None, 8, tm), lambda bi, i: (bi * nt + i, 0, 0)), row(LANES),
                   pl.BlockSpec((1, LANES), lambda bi, i: (0, 0))),
        scratch_shapes=[pltpu.VMEM((1, LANES), F32)],
        compiler_params=_cparams(("arbitrary", "arbitrary")),
        name="mix_moe_route",
    )(lru, swa, diff, h, modtab, w_out, ln0_g, ln0_b, router_w, router_b)


def _row_dma_params(vmem=VMEM_LIMIT):
    return pltpu.CompilerParams(dimension_semantics=("arbitrary", "arbitrary"), vmem_limit_bytes=vmem,
                                disable_bounds_checks=True)


def _dispatch_kernel(s1_ref, s2_ref, ends_ref, h_ref, mod_ref, xs_out, w_s, z_s, sem, zsem, *, ctx_len):
    tm = h_ref.shape[0]
    g = pl.program_id(0) * pl.num_programs(1) + pl.program_id(1)
    last = pl.num_programs(0) * pl.num_programs(1) - 1
    cur = g % 2

    @pl.when(g == 0)
    def _():
        z_s[...] = jnp.zeros(z_s.shape, F32)
        n_experts = ends_ref.shape[0]

        def zero_tile(tile):
            row0 = pl.multiple_of(tile * MOE_TILE, MOE_TILE)
            return pltpu.make_async_copy(z_s, xs_out.at[pl.ds(row0, MOE_TILE), :], zsem)

        for e in range(n_experts):
            zero_tile(jnp.maximum(ends_ref[e] - 1, 0)).start()
        for e in range(n_experts):
            zero_tile(0).wait()

        def zero_unused(tile, carry):
            zero_tile(tile).start()
            zero_tile(tile).wait()
            return carry

        lax.fori_loop(ends_ref[n_experts - 1], xs_out.shape[0] // MOE_TILE, zero_unused, 0)

    def drain(buf):
        for k in range(TOP_K):
            pltpu.make_async_copy(w_s.at[buf], xs_out.at[pl.ds(0, tm), :], sem.at[buf, k]).wait()

    @pl.when(g >= 2)
    def _():
        drain(cur)

    is_ctx = _is_ctx_rows(tm, ctx_len)
    h = h_ref[...]
    w_s[cur] = h * (1.0 + _mod_rows(mod_ref, 4, is_ctx)) + _mod_rows(mod_ref, 3, is_ctx)

    def row_copy(r, slots, k):
        return pltpu.make_async_copy(w_s.at[cur, pl.ds(r, 1), :], xs_out.at[pl.ds(slots[0, r], 1), :],
                                     sem.at[cur, k])

    def issue(r, carry):
        row_copy(r, s1_ref, 0).start()
        row_copy(r, s2_ref, 1).start(priority=1)
        return carry

    lax.fori_loop(0, tm, issue, 0, unroll=8)

    @pl.when(g == last)
    def _():
        @pl.when(g >= 1)
        def _():
            drain(1 - cur)
        drain(cur)


def moe_dispatch(h, modtab, slot1, slot2, ends, n_slots, ctx_len):
    b, t, d = h.shape
    tm = _row_tile(t)
    nt = t // tm
    kern = functools.partial(_dispatch_kernel, ctx_len=ctx_len)
    slot_spec = pl.BlockSpec((None, 1, tm), lambda bi, i: (bi * nt + i, 0, 0), memory_space=pltpu.SMEM)
    return pl.pallas_call(
        kern,
        out_shape=jax.ShapeDtypeStruct((n_slots, d), F32),
        grid=(b, nt),
        in_specs=[slot_spec, slot_spec, pl.BlockSpec(memory_space=pltpu.SMEM),
                  pl.BlockSpec((None, tm, d), lambda bi, i: (bi, i, 0)),
                  pl.BlockSpec((None, 2, 6, d), lambda bi, i: (bi, 0, 0, 0))],
        out_specs=pl.BlockSpec(memory_space=pl.ANY),
        scratch_shapes=[pltpu.VMEM((2, tm, d), F32), pltpu.VMEM((MOE_TILE, d), F32),
                        pltpu.SemaphoreType.DMA((2, TOP_K)), pltpu.SemaphoreType.DMA(())],
        compiler_params=_row_dma_params(),
        name="moe_dispatch",
    )(slot1.reshape(b * nt, 1, tm), slot2.reshape(b * nt, 1, tm), ends, h, modtab)


def _gffn_kernel(te_ref, na_ref, xs_ref, w1_ref, w3_ref, w2_ref, ys_ref, g_s):
    del te_ref
    i = pl.program_id(0)

    @pl.when(i < na_ref[0])
    def _():
        _swiglu_hidden(xs_ref[...].astype(BF16), w1_ref, w3_ref, g_s)
        ys_ref[...] = jnp.dot(g_s[...], w2_ref[...], preferred_element_type=F32)

    @pl.when(i >= na_ref[0])
    def _():
        ys_ref[...] = jnp.zeros(ys_ref.shape, ys_ref.dtype)


def moe_grouped_ffn(xs, tile_expert, n_active, w1, w3, w2, j):
    n_slots, dh = xs.shape
    _, _, d, ff = w1.shape
    tm = MOE_TILE
    n_tiles = n_slots // tm
    return pl.pallas_call(
        _gffn_kernel,
        out_shape=jax.ShapeDtypeStruct((n_slots, dh), F32),
        grid_spec=pltpu.PrefetchScalarGridSpec(
            num_scalar_prefetch=2,
            grid=(n_tiles,),
            in_specs=[pl.BlockSpec((tm, dh), lambda i, te, na: (jnp.minimum(i, na[0] - 1), 0)),
                      pl.BlockSpec((None, None, d, ff), lambda i, te, na: (j, te[i], 0, 0)),
                      pl.BlockSpec((None, None, d, ff), lambda i, te, na: (j, te[i], 0, 0)),
                      pl.BlockSpec((None, None, ff, d), lambda i, te, na: (j, te[i], 0, 0))],
            out_specs=pl.BlockSpec((tm, dh), lambda i, te, na: (i, 0)),
            scratch_shapes=[pltpu.VMEM((tm, ff), BF16)]),
        compiler_params=_cparams(("arbitrary",)),
        name="moe_grouped_ffn",
    )(tile_expert, n_active, xs, w1, w3, w2)


def _combine_kernel(s1_ref, s2_ref, n1_ref, n2_ref, ys_hbm, h_ref, mod_ref, rg_ref, g_ref, b_ref, o_ref, y_s, sem,
                    *, alpha, ctx_len, tile0):
    tm = h_ref.shape[0]
    g = pl.program_id(0) * pl.num_programs(1) + pl.program_id(1)
    last = pl.num_programs(0) * pl.num_programs(1) - 1
    cur = g % 2

    def gather(slots1, slots2, buf):
        def row_copy(r, slots, k):
            return pltpu.make_async_copy(ys_hbm.at[pl.ds(slots[0, r], 1), :], y_s.at[buf, k, pl.ds(r, 1), :],
                                         sem.at[buf, k])

        def issue(r, carry):
            row_copy(r, slots1, 0).start()
            row_copy(r, slots2, 1).start(priority=1)
            return carry

        lax.fori_loop(0, tm, issue, 0, unroll=8)

    @pl.when(g == 0)
    def _():
        gather(s1_ref, s2_ref, cur)

    @pl.when(g < last)
    def _():
        gather(n1_ref, n2_ref, 1 - cur)

    for k in range(TOP_K):
        pltpu.make_async_copy(ys_hbm.at[pl.ds(0, tm), :], y_s.at[cur, k], sem.at[cur, k]).wait()

    is_ctx = ((pl.program_id(1) + tile0) * tm + lax.broadcasted_iota(jnp.int32, (tm, 1), 0)) < ctx_len
    f = rg_ref[:, 0:1] * y_s[cur, 0] + rg_ref[:, 1:2] * y_s[cur, 1]
    o_ref[...] = _deepnorm_ln(h_ref[...], f, _mod_rows(mod_ref, 5, is_ctx), g_ref[...], b_ref[...], alpha)


def moe_combine(ys, slot1, slot2, gates, h, modtab, ln_g, ln_b, alpha, ctx_len, tm, row0):
    b, t, d = h.shape
    assert row0 % tm == 0 and (t - row0) % tm == 0
    tile0 = row0 // tm
    n = (t - row0) // tm
    steps = b * n
    kern = functools.partial(_combine_kernel, alpha=alpha, ctx_len=ctx_len, tile0=tile0)
    cur_spec = pl.BlockSpec((None, 1, tm), lambda bi, i: (bi * n + i, 0, 0), memory_space=pltpu.SMEM)
    nxt_spec = pl.BlockSpec((None, 1, tm), lambda bi, i: (jnp.minimum(bi * n + i + 1, steps - 1), 0, 0),
                            memory_space=pltpu.SMEM)
    row = lambda w: pl.BlockSpec((None, tm, w), lambda bi, i: (bi, i + tile0, 0))
    s1 = slot1[:, row0:].reshape(steps, 1, tm)
    s2 = slot2[:, row0:].reshape(steps, 1, tm)
    return pl.pallas_call(
        kern,
        out_shape=jax.ShapeDtypeStruct((b, t - row0, d), F32),
        grid=(b, n),
        in_specs=[cur_spec, cur_spec, nxt_spec, nxt_spec, pl.BlockSpec(memory_space=pl.ANY), row(d),
                  pl.BlockSpec((None, 2, 6, d), lambda bi, i: (bi, 0, 0, 0)), row(LANES),
                  _const_spec((1, d)), _const_spec((1, d))],
        out_specs=pl.BlockSpec((None, tm, d), lambda bi, i: (bi, i, 0)),
        scratch_shapes=[pltpu.VMEM((2, TOP_K, tm, d), F32), pltpu.SemaphoreType.DMA((2, TOP_K))],
        compiler_params=_row_dma_params(),
        name="moe_combine",
    )(s1, s2, s1, s2, ys, h, modtab, gates, ln_g, ln_b)


def mix_moe_ffn(mixed, h, modtab, w_out, layer, ln0_g, ln0_b, router_w, router_b, w1, w3, w2, j, ln1_g, ln1_b,
                alpha, ctx_len, latent_only):
    b, t, d = h.shape
    n_experts = w1.shape[1]
    n_tiles = -(-(TOP_K * b * t) // MOE_TILE) + n_experts
    h, ri, gates, cnt = mix_moe_route(*mixed, h, modtab, w_out, layer, ln0_g, ln0_b, router_w, router_b, alpha,
                                      ctx_len)
    ln_g, ln_b = ln1_g, ln1_b
    counts = cnt[0, :n_experts].astype(jnp.int32)
    tiles_e = (counts + MOE_TILE - 1) // MOE_TILE
    ends = jnp.cumsum(tiles_e)
    offs = (ends - tiles_e) * MOE_TILE
    n_active = ends[-1:]
    tile_ids = jnp.minimum(jnp.arange(n_tiles, dtype=jnp.int32), n_active - 1)
    tile_expert = jnp.sum((ends[None, :] <= tile_ids[:, None]).astype(jnp.int32), axis=1)

    def slots(e, r):
        off = jnp.zeros_like(r)
        for k in range(n_experts):
            off = jnp.where(e == k, offs[k], off)
        return (off + r).reshape(b, t)

    slot1 = slots(ri[:, ROUTE_E1, :], ri[:, ROUTE_R1, :])
    slot2 = slots(ri[:, ROUTE_E2, :], ri[:, ROUTE_R2, :])
    xs = moe_dispatch(h, modtab, slot1, slot2, ends.astype(jnp.int32), n_tiles * MOE_TILE, ctx_len)
    ys = moe_grouped_ffn(xs, tile_expert, n_active.astype(jnp.int32), w1, w3, w2, j)
    tm, row0 = (TOK_TILE, ctx_len) if latent_only else (_row_tile(t), 0)
    return moe_combine(ys, slot1, slot2, gates, h, modtab, ln_g, ln_b, alpha, ctx_len, tm, row0)


def _rope_table(rows, ctx_len):
    def table(rot_dim):
        n_freq = rot_dim // 4
        inv = ROPE_BASE ** (-jnp.arange(n_freq, dtype=F32) / n_freq)
        row = jnp.repeat(jnp.arange(rows, dtype=F32), GRID_W)
        col = jnp.tile(jnp.arange(GRID_W, dtype=F32), rows)
        ang = jnp.concatenate([row[:, None] * inv, col[:, None] * inv], -1)
        cos, sin = jnp.cos(ang), jnp.sin(ang)
        reps = LANES // rot_dim
        cos_t = jnp.tile(jnp.concatenate([cos, cos], -1), (1, reps))
        sin_t = jnp.tile(jnp.concatenate([-sin, sin], -1), (1, reps))
        ident = (jnp.ones((ctx_len, LANES), F32), jnp.zeros((ctx_len, LANES), F32))
        return jnp.concatenate([ident[0], cos_t], 0), jnp.concatenate([ident[1], sin_t], 0)
    c64, s64 = table(HEAD_DIM)
    c32, s32 = table(DIFF_QK)
    return jnp.concatenate([c64, s64, c32, s32], axis=-1)


def _gate_dense(gate_w):
    cols = []
    for d in range(2):
        for g in range(2):
            cols.append(jax.scipy.linalg.block_diag(*[gate_w[d, g, k] for k in range(LRU_BLOCKS)]))
    return jnp.concatenate(cols, axis=1)


def kernel(x, c, ctx, c_ctx, ada_w, ada_b, w_in, w_out, lru_conv_w, lru_conv_b, lru_gate_w, lru_gate_b, lru_lam,
           swa_sink, diff_lam, diff_norm_g, ln_g, ln_b, ffn_w1, ffn_w3, ffn_w2, moe_router_w, moe_router_b,
           moe_w1, moe_w3, moe_w2):
    b, s, d = x.shape
    ctx_len = ctx.shape[1]
    depth = ada_w.shape[0]
    n_experts = moe_router_w.shape[-1]
    alpha = (2.0 * depth) ** 0.25
    assert s % GRID_W == 0 and s % TOK_TILE == 0 and ctx_len == TOK_TILE

    h = jnp.concatenate([ctx, x], axis=1)
    ropetab = _rope_table(s // GRID_W, ctx_len)

    rows = 8 * ((b + 1 + 7) // 8)
    cvec = jnp.zeros((rows, d), F32).at[:b].set(c).at[b].set(c_ctx)
    mods = ada_modulation(cvec, ada_w, ada_b).reshape(depth, rows, 6, d)
    modtab = jnp.stack([jnp.broadcast_to(mods[:, b:b + 1], (depth, b, 6, d)), mods[:, :b]], axis=2)

    sink_rows = jnp.repeat((swa_sink * math.log2(math.e)).reshape(depth, SWA_KV_HEADS, 1, SWA_GROUP), ATTN_BLOCK,
                           axis=-1)

    w_in, w_out, ffn_w1, ffn_w3, ffn_w2, moe_w1, moe_w3, moe_w2 = (
        w.astype(BF16) for w in (w_in, w_out, ffn_w1, ffn_w3, ffn_w2, moe_w1, moe_w3, moe_w2))

    for layer in range(depth):
        lam_init = 0.8 - 0.6 * math.exp(-0.3 * layer)
        mt = modtab[layer]
        lru_xg, attn_t = input_projection(h, mt, w_in, layer, ropetab)
        lru = rglru_mix(lru_xg, lru_conv_w[layer], lru_conv_b[layer].reshape(1, LRU_WIDTH),
                        _gate_dense(lru_gate_w[layer]).astype(BF16), lru_gate_b[layer].reshape(1, 4 * LRU_WIDTH),
                        lru_lam[layer], ctx_len)
        swa = windowed_attention(attn_t, sink_rows[layer])
        dif = differential_attention(attn_t, diff_lam[layer], diff_norm_g[layer], lam_init, layer < depth - 1)
        j = layer // 2
        ln0 = (ln_g[layer, 0:1], ln_b[layer, 0:1])
        ln1 = (ln_g[layer, 1:2], ln_b[layer, 1:2])
        if layer % 2 == 0:
            h = mix_dense_ffn(lru, swa, dif, h, mt, w_out, layer, *ln0, ffn_w1, ffn_w3, ffn_w2, j, *ln1, alpha, ctx_len)
        else:
            rw = jnp.zeros((d, LANES), F32).at[:, :n_experts].set(moe_router_w[j]).astype(BF16)
            rb = jnp.full((1, LANES), NEG_INF, F32).at[0, :n_experts].set(moe_router_b[j])
            h = mix_moe_ffn((lru, swa, dif), h, mt, w_out, layer, *ln0, rw, rb, moe_w1, moe_w3, moe_w2, j, *ln1,
                            alpha, ctx_len, latent_only=layer == depth - 1)
    return h if depth % 2 == 0 else h[:, ctx_len:, :]
```
